```python
import jax, jax.numpy as jnp
from jax import lax
import numpy as np

D_MODEL = 2048
BATCH = 4
SEQ = 2048
DEPTH = 4
DEC_BATCH = 32
DEC_SEQ = 8
PAST_LEN = 16384
PAGE_SIZE = 128

BRANCH = D_MODEL // 2
N_BRANCH = 3
A_HEADS = 16
A_KV_HEADS = 4
A_HEAD_DIM = BRANCH // A_HEADS
A_GROUP = A_HEADS // A_KV_HEADS
WINDOW = 128
ROPE_THETA = 10000.0
POOL_WINDOWS = (2, 4, 8, 16)
POOL_GROUPS = 4
POOL_GROUP_DIM = BRANCH // POOL_GROUPS
POOL_HIST = 15
C_HEADS = 4
C_KEY = BRANCH // 2
C_DK = C_KEY // C_HEADS
C_DV = BRANCH // C_HEADS
C_GATE_RANK = 16
C_GATE_TAU = 16.0
GLA_CHUNK = 64
EPS = 1e-6

SPLITS = (BRANCH, A_KV_HEADS * A_HEAD_DIM, A_KV_HEADS * A_HEAD_DIM, BRANCH,
          BRANCH, BRANCH,
          C_KEY, C_KEY, BRANCH, C_GATE_RANK, BRANCH,
          N_BRANCH * D_MODEL)
IN_COLS = sum(SPLITS)

kernel_name = 'hybrid_swa_pool_gla_gated_step'


def rms_norm(x, g):
    xf = x.astype(jnp.float32)
    y = xf * lax.rsqrt(jnp.mean(xf * xf, axis=-1, keepdims=True) + EPS)
    return (y * g.astype(jnp.float32)).astype(x.dtype)


def rope(x, pos):
    d = x.shape[-1]
    half = d // 2
    inv = jnp.power(ROPE_THETA, -jnp.arange(half, dtype=jnp.float32) * (2.0 / d))
    ang = pos.astype(jnp.float32)[:, None] * inv[None, :]
    cos = jnp.cos(ang)[None, :, None, :]
    sin = jnp.sin(ang)[None, :, None, :]
    xf = x.astype(jnp.float32)
    x1, x2 = xf[..., :half], xf[..., half:]
    return jnp.concatenate([x1 * cos - x2 * sin, x2 * cos + x1 * sin], axis=-1).astype(x.dtype)


def sink_attention(q, k, v, q_pos, k_pos, sinks):
    s = jnp.einsum('bnqhgd,bnkhd->bnhgqk', q, k, preferred_element_type=jnp.float32) * (A_HEAD_DIM ** -0.5)
    diff = q_pos[:, :, None] - k_pos[:, None, :]
    mask = (k_pos[:, None, :] >= 0) & (diff >= 0) & (diff <= WINDOW)
    s = jnp.where(mask[None, :, None, None], s, -jnp.inf)
    sink = sinks.astype(jnp.float32).reshape(1, 1, A_KV_HEADS, A_GROUP, 1, 1)
    m = jnp.maximum(jnp.max(s, axis=-1, keepdims=True), sink)
    p = jnp.exp(s - m)
    den = jnp.sum(p, axis=-1, keepdims=True) + jnp.exp(sink - m)
    p = (p / den).astype(v.dtype)
    return jnp.einsum('bnhgqk,bnkhd->bnqhgd', p, v)


def multiscale_pool(u, prev, start, w_pool, scale):
    bsz, L, _ = u.shape
    ext = jnp.concatenate([prev.astype(u.dtype), u], axis=1)
    cs = jnp.pad(jnp.cumsum(ext.astype(jnp.float32), axis=1), ((0, 0), (1, 0), (0, 0)))
    pos = start + jnp.arange(L)
    uf = u.astype(jnp.float32)
    hi = POOL_HIST + 1
    outs = []
    for g, w in enumerate(POOL_WINDOWS):
        sl = slice(g * POOL_GROUP_DIM, (g + 1) * POOL_GROUP_DIM)
        win_sum = cs[:, hi:hi + L, sl] - cs[:, hi - w:hi - w + L, sl]
        count = jnp.minimum(w, pos + 1).astype(jnp.float32)[None, :, None]
        outs.append(win_sum / count - uf[:, :, sl])
    d = jnp.stack(outs, axis=2).astype(u.dtype)
    y = jnp.einsum('blgc,gce->blge', d, w_pool).reshape(bsz, L, BRANCH)
    return y * scale, ext[:, -POOL_HIST:]


def gla_chunked(q, k, v, log_a, s0):
    bsz, L, H, _ = q.shape
    C = min(GLA_CHUNK, L)
    pad = (-L) % C
    nc = (L + pad) // C

    def prep(t):
        t = jnp.pad(t.astype(jnp.float32), ((0, 0), (0, pad), (0, 0), (0, 0)))
        return t.reshape(bsz, nc, C, H, t.shape[-1]).transpose(1, 0, 3, 2, 4)

    qc, kc, vc, ac = prep(q), prep(k), prep(v), prep(log_a)
    causal = jnp.tril(jnp.ones((C, C), dtype=bool))[:, :, None]

    def step(S, inp):
        qi, ki, vi, ai = inp
        b = jnp.cumsum(ai, axis=2)
        o = jnp.einsum('bhtd,bhde->bhte', qi * jnp.exp(b), S)
        decay = jnp.exp(jnp.where(causal, b[:, :, :, None, :] - b[:, :, None, :, :], -jnp.inf))
        att = jnp.einsum('bhtd,bhsd,bhtsd->bhts', qi, ki, decay)
        o = o + jnp.einsum('bhts,bhse->bhte', att, vi)
        b_last = b[:, :, -1:, :]
        S = jnp.exp(b_last[:, :, 0, :])[..., None] * S + jnp.einsum('bhsd,bhse->bhde', ki * jnp.exp(b_last - b), vi)
        return S, o

    S, o = lax.scan(step, s0.astype(jnp.float32), (qc, kc, vc, ac))
    o = o.transpose(1, 0, 3, 2, 4).reshape(bsz, L + pad, H, v.shape[-1])[:, :L]
    return o.astype(q.dtype), S


def trunk_layer(x, start, kv_buf, pool_prev, gla_state, norm_g, w_in, q_norm_g, k_norm_g, sinks,
                w_pool, pool_scale, w_gate_lr, b_gate, gla_norm_g, w_branch, w_out):
    bsz, L, _ = x.shape
    h = rms_norm(x, norm_g)
    proj = h @ w_in
    (a_q, a_k, a_v, a_z, b_u, b_z, c_q, c_k, c_v, c_lr, c_z, gates) = jnp.split(
        proj, np.cumsum(SPLITS)[:-1].tolist(), axis=-1)
    pos = start + jnp.arange(L)

    q = rope(rms_norm(a_q.reshape(bsz, L, A_HEADS, A_HEAD_DIM), q_norm_g), pos)
    k = rope(rms_norm(a_k.reshape(bsz, L, A_KV_HEADS, A_HEAD_DIM), k_norm_g), pos)
    v = a_v.reshape(bsz, L, A_KV_HEADS, A_HEAD_DIM)
    if kv_buf is None:
        nb = L // WINDOW
        qb = q.reshape(bsz, nb, WINDOW, A_KV_HEADS, A_GROUP, A_HEAD_DIM)

        def band(t):
            tp = jnp.pad(t, ((0, 0), (WINDOW, 0), (0, 0), (0, 0))).reshape(bsz, nb + 1, WINDOW, A_KV_HEADS, A_HEAD_DIM)
            return jnp.concatenate([tp[:, :-1], tp[:, 1:]], axis=2)

        kb, vb = band(k), band(v)
        q_pos = pos.reshape(nb, WINDOW)
        k_pos = (jnp.arange(nb)[:, None] - 1) * WINDOW + jnp.arange(2 * WINDOW)[None, :] + start
        new_k, new_v = k[:, -WINDOW:], v[:, -WINDOW:]
    else:
        ck, cv = kv_buf
        kk = jnp.concatenate([ck.astype(k.dtype), k], axis=1)
        vv = jnp.concatenate([cv.astype(v.dtype), v], axis=1)
        qb = q.reshape(bsz, 1, L, A_KV_HEADS, A_GROUP, A_HEAD_DIM)
        kb, vb = kk[:, None], vv[:, None]
        q_pos = pos[None]
        k_pos = jnp.concatenate([start - WINDOW + jnp.arange(WINDOW), pos])[None]
        new_k, new_v = kk[:, -WINDOW:], vv[:, -WINDOW:]
    o_a = sink_attention(qb, kb, vb, q_pos, k_pos, sinks).reshape(bsz, L, BRANCH)

    o_b, new_pool = multiscale_pool(b_u, pool_prev, start, w_pool, pool_scale)

    gq = c_q.reshape(bsz, L, C_HEADS, C_DK) * (C_DK ** -0.5)
    gk = c_k.reshape(bsz, L, C_HEADS, C_DK)
    gv = c_v.reshape(bsz, L, C_HEADS, C_DV)
    log_a = jax.nn.log_sigmoid((c_lr @ w_gate_lr + b_gate).astype(jnp.float32)) / C_GATE_TAU
    o_c, new_S = gla_chunked(gq, gk, gv, log_a.reshape(bsz, L, C_HEADS, C_DK), gla_state)
    o_c = rms_norm(o_c, gla_norm_g).reshape(bsz, L, BRANCH)

    branches = jnp.stack([o_a * jax.nn.silu(a_z), o_b * jax.nn.silu(b_z), o_c * jax.nn.silu(c_z)], axis=2)
    br = jnp.einsum('blnc,ncd->blnd', branches, w_branch)
    g = jax.nn.sigmoid(gates.reshape(bsz, L, N_BRANCH, D_MODEL).astype(jnp.float32))
    merged = jnp.sum(g * br.astype(jnp.float32), axis=2).astype(x.dtype)
    y = x + merged @ w_out
    return y, new_k, new_v, new_pool, new_S.astype(x.dtype)


def setup_inputs(seed: int = 0) -> dict:
    key = jax.random.key(seed)
    ks = jax.random.split(key, 20)
    f32 = jnp.float32
    nrm = lambda k, shape: jax.random.normal(k, shape, dtype=f32)
    return {
        'x_prompt': nrm(ks[0], (BATCH, SEQ, D_MODEL)),
        'x_sample': nrm(ks[1], (DEC_BATCH, DEC_SEQ, D_MODEL)),
        'cache_a_k': nrm(ks[2], (DEPTH, DEC_BATCH, WINDOW, A_KV_HEADS, A_HEAD_DIM)),
        'cache_a_v': nrm(ks[3], (DEPTH, DEC_BATCH, WINDOW, A_KV_HEADS, A_HEAD_DIM)),
        'state_pool': nrm(ks[4], (DEPTH, DEC_BATCH, POOL_HIST, BRANCH)),
        'state_gla': nrm(ks[5], (DEPTH, DEC_BATCH, C_HEADS, C_DK, C_DV)),
        'norm_g': 1.0 + 0.02 * nrm(ks[6], (DEPTH, D_MODEL)),
        'w_in': nrm(ks[7], (DEPTH, D_MODEL, IN_COLS)) * D_MODEL ** -0.5,
        'q_norm_g': 1.0 + 0.02 * nrm(ks[8], (DEPTH, A_HEAD_DIM)),
        'k_norm_g': 1.0 + 0.02 * nrm(ks[9], (DEPTH, A_HEAD_DIM)),
        'sinks': nrm(ks[10], (DEPTH, A_HEADS)),
        'w_pool': nrm(ks[11], (DEPTH, POOL_GROUPS, POOL_GROUP_DIM, POOL_GROUP_DIM)) * POOL_GROUP_DIM ** -0.5,
        'pool_scale': 1.0 + 0.02 * nrm(ks[12], (DEPTH, BRANCH)),
        'w_gate_lr': nrm(ks[13], (DEPTH, C_GATE_RANK, C_KEY)) * C_GATE_RANK ** -0.5,
        'b_gate': 0.1 * nrm(ks[14], (DEPTH, C_KEY)),
        'gla_norm_g': 1.0 + 0.02 * nrm(ks[15], (DEPTH, C_DV)),
        'w_branch': nrm(ks[16], (DEPTH, N_BRANCH, BRANCH, D_MODEL)) * BRANCH ** -0.5,
        'w_out': nrm(ks[17], (DEPTH, D_MODEL, D_MODEL)) * D_MODEL ** -0.5,
    }


def reference(x_prompt, x_sample, cache_a_k, cache_a_v, state_pool, state_gla, norm_g, w_in, q_norm_g,
              k_norm_g, sinks, w_pool, pool_scale, w_gate_lr, b_gate, gla_norm_g, w_branch, w_out):
    yp, ys = x_prompt, x_sample
    pool0 = jnp.zeros((x_prompt.shape[0], POOL_HIST, BRANCH), x_prompt.dtype)
    gla0 = jnp.zeros((x_prompt.shape[0], C_HEADS, C_DK, C_DV), jnp.float32)
    pk, pv, pp, ps, sk, sv, sp, ss = [], [], [], [], [], [], [], []
    for l in range(DEPTH):
        params = (norm_g[l], w_in[l], q_norm_g[l], k_norm_g[l], sinks[l], w_pool[l], pool_scale[l],
                  w_gate_lr[l], b_gate[l], gla_norm_g[l], w_branch[l], w_out[l])
        yp, a, b, c, d = trunk_layer(yp, 0, None, pool0, gla0, *params)
        pk.append(a); pv.append(b); pp.append(c); ps.append(d)
        ys, a, b, c, d = trunk_layer(ys, PAST_LEN, (cache_a_k[l], cache_a_v[l]), state_pool[l], state_gla[l], *params)
        sk.append(a); sv.append(b); sp.append(c); ss.append(d)
    return (yp, ys, jnp.stack(pk), jnp.stack(pv), jnp.stack(pp), jnp.stack(ps),
            jnp.stack(sk), jnp.stack(sv), jnp.stack(sp), jnp.stack(ss))
```

```python
import functools

import numpy as np
import jax
import jax.numpy as jnp
from jax import lax
from jax.experimental import pallas as pl
from jax.experimental.pallas import tpu as pltpu

F32 = jnp.float32
BF = jnp.bfloat16

D_MODEL = 2048
BATCH = 4
SEQ = 2048
DEPTH = 4
DEC_BATCH = 32
DEC_SEQ = 8
PAST_LEN = 16384
BRANCH = D_MODEL // 2
N_BRANCH = 3
A_HEADS = 16
A_KV_HEADS = 4
A_HEAD_DIM = 64
A_GROUP = A_HEADS // A_KV_HEADS
WINDOW = 128
ROPE_THETA = 10000.0
POOL_WINDOWS = (2, 4, 8, 16)
POOL_GROUP_DIM = 256
POOL_HIST = 15
C_HEADS = 4
C_KEY = 512
C_DK = 128
C_DV = 256
C_GATE_RANK = 16
C_GATE_TAU = 16.0
EPS = 1e-6
KV_W = A_KV_HEADS * A_HEAD_DIM

T_P = BATCH * SEQ
T_S = DEC_BATCH * DEC_SEQ
T_ALL = T_P + T_S

_O_AQ, _O_AK, _O_AV, _O_AZ = 0, 1024, 1280, 1536
_O_BU, _O_BZ = 2560, 3584
_O_CQ, _O_CK, _O_CV, _O_CLR, _O_CZ = 4608, 5120, 5632, 6656, 6672
_O_G = 7696
P_AQ, P_AZ, P_BU, P_BZ, P_CV, P_CZ = 0, 1024, 2048, 3072, 4096, 5120
P_AK, P_AV, P_CQ, P_CK, P_G = 6144, 6400, 6656, 7168, 7680
P_COLS = P_G + N_BRANCH * D_MODEL

TM = 768
TN_IN = 512
TN_MRG = 512
TN_OUT = 512
ATT_BLK = WINDOW
POOL_TP = 512
POOL_HALO = 16
GLA_C = 64
GLA_NSUB = 8
GLA_SC = GLA_C * GLA_NSUB
SMP_BS = 2
SMP_GB = 8
VMEM_LIMIT = 48 * 1024 * 1024
NEG_BIG = -1e30

PROJ_DT = F32


def _cparams(n_axes):
    return pltpu.CompilerParams(
        dimension_semantics=("arbitrary",) * n_axes, vmem_limit_bytes=VMEM_LIMIT)


def _sigmoid(x):
    return 1.0 / (1.0 + jnp.exp(-x))


def _silu(x):
    return x * _sigmoid(x)


def _dot(a, b):
    return jnp.dot(a, b, preferred_element_type=F32)


def _dot_nt(a, b):
    return lax.dot_general(a, b, (((1,), (1,)), ((), ())), preferred_element_type=F32)


def _dot_tn(a, b):
    return lax.dot_general(a, b, (((0,), (0,)), ((), ())), preferred_element_type=F32)


def _split3(x):
    hi = x.astype(BF)
    r1 = x - hi.astype(F32)
    mid = r1.astype(BF)
    lo = (r1 - mid.astype(F32)).astype(BF)
    return hi, mid, lo


def _inproj_kernel(x_ref, g_ref, w_ref, wlr_ref, wg_ref, bg_ref, proj_ref, loga_ref, h_scr):
    @pl.when(pl.program_id(1) == 0)
    def _():
        xf = x_ref[...]
        ms = jnp.mean(xf * xf, axis=-1, keepdims=True)
        h = (xf * lax.rsqrt(ms + EPS) * g_ref[...]).astype(BF)
        h_scr[...] = h
        lr = _dot(h, wlr_ref[...])
        z = _dot(lr.astype(BF), wg_ref[...]) + bg_ref[...]
        log_sig = jnp.minimum(z, 0.0) - jnp.log1p(jnp.exp(-jnp.abs(z)))
        loga_ref[...] = log_sig * (1.0 / C_GATE_TAU)

    proj_ref[...] = _dot(h_scr[...], w_ref[...]).astype(proj_ref.dtype)


def _inproj(x, g, w_main, w_lr, w_g, b_g):
    grid = (T_ALL // TM, P_COLS // TN_IN)
    return pl.pallas_call(
        _inproj_kernel,
        grid=grid,
        in_specs=[
            pl.BlockSpec((TM, D_MODEL), lambda i, j: (i, 0)),
            pl.BlockSpec((1, D_MODEL), lambda i, j: (0, 0)),
            pl.BlockSpec((D_MODEL, TN_IN), lambda i, j: (0, j)),
            pl.BlockSpec((D_MODEL, 128), lambda i, j: (0, 0)),
            pl.BlockSpec((128, C_KEY), lambda i, j: (0, 0)),
            pl.BlockSpec((1, C_KEY), lambda i, j: (0, 0)),
        ],
        out_specs=[
            pl.BlockSpec((TM, TN_IN), lambda i, j: (i, j)),
            pl.BlockSpec((TM, C_KEY), lambda i, j: (i, 0)),
        ],
        out_shape=[
            jax.ShapeDtypeStruct((T_ALL, P_COLS), PROJ_DT),
            jax.ShapeDtypeStruct((T_ALL, C_KEY), F32),
        ],
        scratch_shapes=[pltpu.VMEM((TM, D_MODEL), BF)],
        compiler_params=_cparams(2),
    )(x, g, w_main, w_lr, w_g, b_g)


def _head_norm(x, bd, g_row):
    outs = []
    for c in range(x.shape[1] // 256):
        xc = x[:, c * 256:(c + 1) * 256]
        ss = _dot((xc * xc).astype(BF), bd)
        outs.append(xc * lax.rsqrt(ss * (1.0 / A_HEAD_DIM) + EPS) * g_row)
    return outs


def _rope128(xc, cos, sin, first_half):
    swapped = jnp.where(first_half, pltpu.roll(xc, 96, 1), pltpu.roll(xc, 32, 1))
    return xc * cos + swapped * sin


def _norm_rope(x, bd, g_row, cos, sin, first_half):
    chunks = []
    for blk in _head_norm(x, bd, g_row):
        for c in range(2):
            chunks.append(_rope128(blk[:, c * 128:(c + 1) * 128], cos, sin, first_half))
    return chunks


def _dup_half(chunk, rolled, lo_half, use_low):
    return jnp.where(lo_half, chunk, rolled) if use_low else jnp.where(lo_half, rolled, chunk)


def _attn_prompt_kernel(sinks_ref, q_ref, z_ref, k_ref, v_ref, cos_ref, sin_ref, bd_ref, qg_ref, kg_ref,
                        o_ref, knew_ref, kprev, vprev):
    n = pl.program_id(1)
    R = ATT_BLK
    lane = lax.broadcasted_iota(jnp.int32, (R, 128), 1)
    row = lax.broadcasted_iota(jnp.int32, (R, 128), 0)
    lo_half = lane < 64
    first_half = (lane & 32) == 0
    cos = cos_ref[...]
    sin = sin_ref[...]
    bd = bd_ref[...]

    @pl.when(n == 0)
    def _():
        kprev[...] = jnp.zeros_like(kprev)
        vprev[...] = jnp.zeros_like(vprev)

    kch = _norm_rope(k_ref[...].astype(F32), bd, kg_ref[...], cos, sin, first_half)
    knew_ref[0] = jnp.concatenate(kch, axis=1)
    vraw = v_ref[...].astype(F32)
    vch = [vraw[:, :128], vraw[:, 128:]]
    kcur, vcur = [], []
    for h in range(A_KV_HEADS):
        c, low = h // 2, (h % 2 == 0)
        kcur.append(_dup_half(kch[c], pltpu.roll(kch[c], 64, 1), lo_half, low).astype(BF))
        vcur.append(_dup_half(vch[c], pltpu.roll(vch[c], 64, 1), lo_half, low).astype(BF))

    qch = _norm_rope(q_ref[...].astype(F32), bd, qg_ref[...], cos, sin, first_half)
    mask_prev = jnp.logical_and(lane >= row, n > 0)
    mask_cur = lane <= row

    for c in range(A_HEADS // 2):
        o_pair = []
        for half in range(2):
            i = 2 * c + half
            h = i // A_GROUP
            sel = lo_half if half == 0 else jnp.logical_not(lo_half)
            qm = jnp.where(sel, qch[c] * (A_HEAD_DIM ** -0.5), 0.0).astype(BF)
            s_p = jnp.where(mask_prev, _dot_nt(qm, kprev[h]), NEG_BIG)
            s_c = jnp.where(mask_cur, _dot_nt(qm, kcur[h]), NEG_BIG)
            sink = sinks_ref[i]
            m = jnp.maximum(jnp.maximum(jnp.max(s_p, axis=-1, keepdims=True),
                                        jnp.max(s_c, axis=-1, keepdims=True)), sink)
            p_p = jnp.exp(s_p - m)
            p_c = jnp.exp(s_c - m)
            den = (jnp.sum(p_p, axis=-1, keepdims=True) + jnp.sum(p_c, axis=-1, keepdims=True)
                   + jnp.exp(sink - m))
            pv = _dot(p_p.astype(BF), vprev[h]) + _dot(p_c.astype(BF), vcur[h])
            o_pair.append(pv / den)
        o = jnp.where(lo_half, o_pair[0], o_pair[1])
        zc = z_ref[:, c * 128:(c + 1) * 128].astype(F32)
        o_ref[:, c * 128:(c + 1) * 128] = (o * _silu(zc)).astype(o_ref.dtype)

    for h in range(A_KV_HEADS):
        kprev[h] = kcur[h]
        vprev[h] = vcur[h]


def _attn_prompt(proj, sinks, cos, sin, bd, qg, kg):
    nb = SEQ // ATT_BLK
    rb = lambda b, n: b * nb + n
    return pl.pallas_call(
        _attn_prompt_kernel,
        grid=(BATCH, nb),
        in_specs=[
            pl.BlockSpec(memory_space=pltpu.SMEM),
            pl.BlockSpec((ATT_BLK, BRANCH), lambda b, n: (rb(b, n), P_AQ // BRANCH)),
            pl.BlockSpec((ATT_BLK, BRANCH), lambda b, n: (rb(b, n), P_AZ // BRANCH)),
            pl.BlockSpec((ATT_BLK, KV_W), lambda b, n: (rb(b, n), P_AK // KV_W)),
            pl.BlockSpec((ATT_BLK, KV_W), lambda b, n: (rb(b, n), P_AV // KV_W)),
            pl.BlockSpec((ATT_BLK, 128), lambda b, n: (n, 0)),
            pl.BlockSpec((ATT_BLK, 128), lambda b, n: (n, 0)),
            pl.BlockSpec((256, 256), lambda b, n: (0, 0)),
            pl.BlockSpec((1, 256), lambda b, n: (0, 0)),
            pl.BlockSpec((1, 256), lambda b, n: (0, 0)),
        ],
        out_specs=[
            pl.BlockSpec((ATT_BLK, BRANCH), lambda b, n: (rb(b, n), 0)),
            pl.BlockSpec((1, ATT_BLK, KV_W), lambda b, n: (b, 0, 0)),
        ],
        out_shape=[
            jax.ShapeDtypeStruct((T_ALL, BRANCH), BF),
            jax.ShapeDtypeStruct((BATCH, WINDOW, KV_W), F32),
        ],
        scratch_shapes=[pltpu.VMEM((A_KV_HEADS, ATT_BLK, 128), BF),
                        pltpu.VMEM((A_KV_HEADS, ATT_BLK, 128), BF)],
        compiler_params=_cparams(2),
    )(sinks, proj, proj, proj, proj, cos, sin, bd, qg, kg)


def _attn_sample_kernel(sinks_ref, oa_in_ref, q_ref, z_ref, k_ref, v_ref, ck_ref, cv_ref, cos_ref, sin_ref,
                        bd_ref, qg_ref, kg_ref, o_ref, nk_ref, nv_ref, kall, vall):
    del oa_in_ref
    L = DEC_SEQ
    R = SMP_BS * L
    lane = lax.broadcasted_iota(jnp.int32, (R, 128), 1)
    first_half = (lane & 32) == 0
    lane8 = lax.broadcasted_iota(jnp.int32, (L, 128), 1)
    lo8 = lane8 < 64
    lane_c = lax.broadcasted_iota(jnp.int32, (WINDOW, 128), 1)
    lo_c = lane_c < 64
    cos = cos_ref[...]
    sin = sin_ref[...]
    bd = bd_ref[...]

    kall[...] = jnp.zeros_like(kall)
    vall[...] = jnp.zeros_like(vall)

    kch = _norm_rope(k_ref[...].astype(F32), bd, kg_ref[...], cos, sin, first_half)
    vraw = v_ref[...].astype(F32)
    vch = [vraw[:, :128], vraw[:, 128:]]
    qch = _norm_rope(q_ref[...].astype(F32), bd, qg_ref[...], cos, sin, first_half)

    rows = A_GROUP * L
    srow = lax.broadcasted_iota(jnp.int32, (rows, 2 * WINDOW), 0)
    scol = lax.broadcasted_iota(jnp.int32, (rows, 2 * WINDOW), 1)
    t = srow & (L - 1)
    mask = jnp.logical_or(jnp.logical_and(scol < WINDOW, scol >= t),
                          jnp.logical_and(scol >= WINDOW, (scol - WINDOW) <= t))
    hrow = lax.broadcasted_iota(jnp.int32, (rows, 1), 0) >> (L.bit_length() - 1)
    z_all = z_ref[...].astype(F32)
    o_rows = []

    for bi in range(SMP_BS):
        r0 = bi * L
        knew = [kc[r0:r0 + L] for kc in kch]
        vnew = [vc[r0:r0 + L] for vc in vch]
        ck = ck_ref[bi]
        cv = cv_ref[bi]
        nk_ref[bi, 0:WINDOW - L] = ck[L:WINDOW]
        nv_ref[bi, 0:WINDOW - L] = cv[L:WINDOW]
        nk_ref[bi, WINDOW - L:WINDOW] = jnp.concatenate(knew, axis=1)
        nv_ref[bi, WINDOW - L:WINDOW] = jnp.concatenate(vnew, axis=1)
        for h in range(A_KV_HEADS):
            c, low = h // 2, (h % 2 == 0)
            ckc = ck[:, c * 128:(c + 1) * 128]
            cvc = cv[:, c * 128:(c + 1) * 128]
            kall[h, 0:WINDOW] = _dup_half(ckc, pltpu.roll(ckc, 64, 1), lo_c, low)
            vall[h, 0:WINDOW] = _dup_half(cvc, pltpu.roll(cvc, 64, 1), lo_c, low)
            kall[h, WINDOW:WINDOW + L] = _dup_half(knew[c], pltpu.roll(knew[c], 64, 1), lo8, low)
            vall[h, WINDOW:WINDOW + L] = _dup_half(vnew[c], pltpu.roll(vnew[c], 64, 1), lo8, low)

        o_chunks = []
        for h in range(A_KV_HEADS):
            qs = []
            for gi in range(A_GROUP):
                i = h * A_GROUP + gi
                sel = lo8 if i % 2 == 0 else jnp.logical_not(lo8)
                qs.append(jnp.where(sel, qch[i // 2][r0:r0 + L] * (A_HEAD_DIM ** -0.5), 0.0))
            qm = jnp.concatenate(qs, axis=0).astype(BF)
            s = jnp.where(mask, _dot_nt(qm, kall[h].astype(BF)), NEG_BIG)
            sink = jnp.zeros((rows, 1), F32)
            for gi in range(A_GROUP):
                sink = jnp.where(hrow == gi, sinks_ref[h * A_GROUP + gi], sink)
            m = jnp.maximum(jnp.max(s, axis=-1, keepdims=True), sink)
            p = jnp.exp(s - m)
            den = jnp.sum(p, axis=-1, keepdims=True) + jnp.exp(sink - m)
            pv = _dot(p.astype(BF), vall[h].astype(BF)) / den
            for cc in range(2):
                o_chunks.append(jnp.where(lo8, pv[(2 * cc) * L:(2 * cc + 1) * L],
                                          pv[(2 * cc + 1) * L:(2 * cc + 2) * L]))
        o = jnp.concatenate(o_chunks, axis=1)
        o_rows.append(o * _silu(z_all[r0:r0 + L]))
    o_ref[...] = jnp.concatenate(o_rows, axis=0).astype(o_ref.dtype)


def _attn_sample(oa, proj, cache_k, cache_v, sinks, cos, sin, bd, qg, kg):
    R = SMP_BS * DEC_SEQ
    base = T_P // R
    return pl.pallas_call(
        _attn_sample_kernel,
        grid=(DEC_BATCH // SMP_BS,),
        in_specs=[
            pl.BlockSpec(memory_space=pltpu.SMEM),
            pl.BlockSpec(memory_space=pl.ANY),
            pl.BlockSpec((R, BRANCH), lambda b: (base + b, P_AQ // BRANCH)),
            pl.BlockSpec((R, BRANCH), lambda b: (base + b, P_AZ // BRANCH)),
            pl.BlockSpec((R, KV_W), lambda b: (base + b, P_AK // KV_W)),
            pl.BlockSpec((R, KV_W), lambda b: (base + b, P_AV // KV_W)),
            pl.BlockSpec((SMP_BS, WINDOW, KV_W), lambda b: (b, 0, 0)),
            pl.BlockSpec((SMP_BS, WINDOW, KV_W), lambda b: (b, 0, 0)),
            pl.BlockSpec((R, 128), lambda b: (0, 0)),
            pl.BlockSpec((R, 128), lambda b: (0, 0)),
            pl.BlockSpec((256, 256), lambda b: (0, 0)),
            pl.BlockSpec((1, 256), lambda b: (0, 0)),
            pl.BlockSpec((1, 256), lambda b: (0, 0)),
        ],
        out_specs=[
            pl.BlockSpec((R, BRANCH), lambda b: (base + b, 0)),
            pl.BlockSpec((SMP_BS, WINDOW, KV_W), lambda b: (b, 0, 0)),
            pl.BlockSpec((SMP_BS, WINDOW, KV_W), lambda b: (b, 0, 0)),
        ],
        out_shape=[
            jax.ShapeDtypeStruct((T_ALL, BRANCH), BF),
            jax.ShapeDtypeStruct((DEC_BATCH, WINDOW, KV_W), F32),
            jax.ShapeDtypeStruct((DEC_BATCH, WINDOW, KV_W), F32),
        ],
        scratch_shapes=[pltpu.VMEM((A_KV_HEADS, 2 * WINDOW, 128), F32),
                        pltpu.VMEM((A_KV_HEADS, 2 * WINDOW, 128), F32)],
        input_output_aliases={1: 0},
        compiler_params=_cparams(1),
    )(sinks, oa, proj, proj, proj, proj, cache_k, cache_v, cos, sin, bd, qg, kg)


def _pool_compute(ext, u_rows, start_pos, wp_ref, scale_ref, z, n_rows):
    pos = start_pos + lax.broadcasted_iota(jnp.int32, (n_rows, 1), 0)
    outs = []
    for g, w in enumerate(POOL_WINDOWS):
        sl = slice(g * POOL_GROUP_DIM, (g + 1) * POOL_GROUP_DIM)
        acc = ext[:, sl]
        span = 1
        while span < w:
            acc = acc + pltpu.roll(acc, span, 0)
            span *= 2
        win_sum = acc[POOL_HALO:]
        count = jnp.minimum(w, pos + 1).astype(F32)
        d = (win_sum / count - u_rows[:, sl]).astype(BF)
        outs.append(_dot(d, wp_ref[g]))
    y = jnp.concatenate(outs, axis=1) * scale_ref[...]
    return y * _silu(z)


def _pool_prompt_kernel(u_ref, z_ref, wp_ref, scale_ref, o_ref, tail_ref, ext):
    t = pl.program_id(1)

    @pl.when(t == 0)
    def _():
        ext[0:POOL_HALO] = jnp.zeros((POOL_HALO, BRANCH), F32)

    u = u_ref[...].astype(F32)
    ext[POOL_HALO:] = u
    out = _pool_compute(ext[...], u, t * POOL_TP, wp_ref, scale_ref, z_ref[...].astype(F32), POOL_TP)
    o_ref[...] = out.astype(o_ref.dtype)
    tail = ext[POOL_TP:POOL_TP + POOL_HALO]
    tail_ref[0] = tail
    ext[0:POOL_HALO] = tail


def _pool_prompt(proj, w_pool, scale):
    nt = SEQ // POOL_TP
    return pl.pallas_call(
        _pool_prompt_kernel,
        grid=(BATCH, nt),
        in_specs=[
            pl.BlockSpec((POOL_TP, BRANCH), lambda b, t: (b * nt + t, P_BU // BRANCH)),
            pl.BlockSpec((POOL_TP, BRANCH), lambda b, t: (b * nt + t, P_BZ // BRANCH)),
            pl.BlockSpec((len(POOL_WINDOWS), POOL_GROUP_DIM, POOL_GROUP_DIM), lambda b, t: (0, 0, 0)),
            pl.BlockSpec((1, BRANCH), lambda b, t: (0, 0)),
        ],
        out_specs=[
            pl.BlockSpec((POOL_TP, BRANCH), lambda b, t: (b * nt + t, 0)),
            pl.BlockSpec((1, POOL_HALO, BRANCH), lambda b, t: (b, 0, 0)),
        ],
        out_shape=[
            jax.ShapeDtypeStruct((T_ALL, BRANCH), BF),
            jax.ShapeDtypeStruct((BATCH, POOL_HALO, BRANCH), F32),
        ],
        scratch_shapes=[pltpu.VMEM((POOL_HALO + POOL_TP, BRANCH), F32)],
        compiler_params=_cparams(2),
    )(proj, proj, w_pool, scale)


def _pool_sample_kernel(ob_in_ref, u_ref, z_ref, prev_ref, wp_ref, scale_ref, o_ref, tail_ref):
    del ob_in_ref
    L = DEC_SEQ
    u_all = u_ref[...].astype(F32)
    z_all = z_ref[...].astype(F32)
    outs = []
    for bi in range(SMP_BS):
        u = u_all[bi * L:(bi + 1) * L]
        ext = jnp.concatenate([prev_ref[bi], u], axis=0)
        outs.append(_pool_compute(ext, u, PAST_LEN, wp_ref, scale_ref, z_all[bi * L:(bi + 1) * L], L))
        tail_ref[bi] = ext[L:L + POOL_HALO]
    o_ref[...] = jnp.concatenate(outs, axis=0).astype(o_ref.dtype)


def _pool_sample(ob, proj, prev16, w_pool, scale):
    R = SMP_BS * DEC_SEQ
    base = T_P // R
    return pl.pallas_call(
        _pool_sample_kernel,
        grid=(DEC_BATCH // SMP_BS,),
        in_specs=[
            pl.BlockSpec(memory_space=pl.ANY),
            pl.BlockSpec((R, BRANCH), lambda b: (base + b, P_BU // BRANCH)),
            pl.BlockSpec((R, BRANCH), lambda b: (base + b, P_BZ // BRANCH)),
            pl.BlockSpec((SMP_BS, POOL_HALO, BRANCH), lambda b: (b, 0, 0)),
            pl.BlockSpec((len(POOL_WINDOWS), POOL_GROUP_DIM, POOL_GROUP_DIM), lambda b: (0, 0, 0)),
            pl.BlockSpec((1, BRANCH), lambda b: (0, 0)),
        ],
        out_specs=[
            pl.BlockSpec((R, BRANCH), lambda b: (base + b, 0)),
            pl.BlockSpec((SMP_BS, POOL_HALO, BRANCH), lambda b: (b, 0, 0)),
        ],
        out_shape=[
            jax.ShapeDtypeStruct((T_ALL, BRANCH), BF),
            jax.ShapeDtypeStruct((DEC_BATCH, POOL_HALO, BRANCH), F32),
        ],
        input_output_aliases={0: 0},
        compiler_params=_cparams(1),
    )(ob, proj, proj, prev16, w_pool, scale)


def _gla_chunk_out(q, k, v_bf, b, s_bf, tri, g_row, z):
    c = q.shape[0]
    b_mid = b[c // 2 - 1:c // 2]
    o = _dot((q * jnp.exp(b)).astype(BF), s_bf)
    q2 = (q * jnp.exp(b - b_mid)).astype(BF)
    k2 = (k * jnp.exp(b_mid - b)).astype(BF)
    att = jnp.where(tri, _dot_nt(q2, k2), 0.0).astype(BF)
    o = o + _dot(att, v_bf)
    ms = jnp.mean(o * o, axis=-1, keepdims=True)
    on = o * lax.rsqrt(ms + EPS) * g_row
    return on * _silu(z)


def _gla_prompt_kernel(q_ref, k_ref, v_ref, z_ref, la_ref, bt_ref, ones_ref, g_ref, o_ref, snew_ref,
                       s_scr, u_scr, d_scr, sc_scr):
    @pl.when(pl.program_id(2) == 0)
    def _():
        s_scr[...] = jnp.zeros_like(s_scr)

    C = GLA_C
    parts = _split3(la_ref[...])
    bt = bt_ref[...]
    b_all = _dot(bt, parts[0]) + _dot(bt, parts[1]) + _dot(bt, parts[2])
    q = q_ref[...].astype(F32) * (C_DK ** -0.5)
    k = k_ref[...].astype(F32)
    v = v_ref[...].astype(BF)
    ones = ones_ref[...]
    tri = (lax.broadcasted_iota(jnp.int32, (C, C), 0) >= lax.broadcasted_iota(jnp.int32, (C, C), 1))

    for c in range(GLA_NSUB):
        sl = slice(c * C, (c + 1) * C)
        b = b_all[sl]
        kd = (k[sl] * jnp.exp(b[C - 1:C] - b)).astype(BF)
        u_scr[c] = _dot_tn(kd, v[sl])
        dsum = _dot_tn(parts[0][sl], ones) + _dot_tn(parts[1][sl], ones) + _dot_tn(parts[2][sl], ones)
        d_scr[c] = jnp.exp(dsum)

    s = s_scr[...]
    for c in range(GLA_NSUB):
        sc_scr[c] = s.astype(BF)
        d = d_scr[c]
        s = jnp.concatenate([d, d], axis=1) * s + u_scr[c]
    s_scr[...] = s
    snew_ref[0, 0] = s

    g_row = g_ref[...]
    for c in range(GLA_NSUB):
        sl = slice(c * C, (c + 1) * C)
        out = _gla_chunk_out(q[sl], k[sl], v[sl], b_all[sl], sc_scr[c], tri, g_row,
                             z_ref[sl, :].astype(F32))
        o_ref[sl, :] = out.astype(o_ref.dtype)


def _gla_prompt(proj, loga, bt, ones, g):
    ns = SEQ // GLA_SC
    rb = lambda b, h, s: b * ns + s
    return pl.pallas_call(
        _gla_prompt_kernel,
        grid=(BATCH, C_HEADS, ns),
        in_specs=[
            pl.BlockSpec((GLA_SC, C_DK), lambda b, h, s: (rb(b, h, s), P_CQ // C_DK + h)),
            pl.BlockSpec((GLA_SC, C_DK), lambda b, h, s: (rb(b, h, s), P_CK // C_DK + h)),
            pl.BlockSpec((GLA_SC, C_DV), lambda b, h, s: (rb(b, h, s), P_CV // C_DV + h)),
            pl.BlockSpec((GLA_SC, C_DV), lambda b, h, s: (rb(b, h, s), P_CZ // C_DV + h)),
            pl.BlockSpec((GLA_SC, C_DK), lambda b, h, s: (rb(b, h, s), h)),
            pl.BlockSpec((GLA_SC, GLA_SC), lambda b, h, s: (0, 0)),
            pl.BlockSpec((GLA_C, 128), lambda b, h, s: (0, 0)),
            pl.BlockSpec((1, C_DV), lambda b, h, s: (0, 0)),
        ],
        out_specs=[
            pl.BlockSpec((GLA_SC, C_DV), lambda b, h, s: (rb(b, h, s), h)),
            pl.BlockSpec((1, 1, C_DK, C_DV), lambda b, h, s: (b, h, 0, 0)),
        ],
        out_shape=[
            jax.ShapeDtypeStruct((T_ALL, BRANCH), BF),
            jax.ShapeDtypeStruct((BATCH, C_HEADS, C_DK, C_DV), F32),
        ],
        scratch_shapes=[
            pltpu.VMEM((C_DK, C_DV), F32),
            pltpu.VMEM((GLA_NSUB, C_DK, C_DV), F32),
            pltpu.VMEM((GLA_NSUB, C_DK, 128), F32),
            pltpu.VMEM((GLA_NSUB, C_DK, C_DV), BF),
        ],
        compiler_params=_cparams(3),
    )(proj, proj, proj, proj, loga, bt, ones, g)


def _gla_sample_kernel(oc_in_ref, q_ref, k_ref, v_ref, z_ref, la_ref, s0_ref, bt_ref, ones_ref, g_ref,
                       o_ref, snew_ref):
    del oc_in_ref
    C = SMP_GB * DEC_SEQ
    L = DEC_SEQ
    la = la_ref[...]
    parts = _split3(la)
    bt = bt_ref[...]
    b = _dot(bt, parts[0]) + _dot(bt, parts[1]) + _dot(bt, parts[2])
    q = q_ref[...].astype(F32) * (C_DK ** -0.5)
    k = k_ref[...].astype(F32)
    v = v_ref[...].astype(BF)
    ones = ones_ref[...]
    row = lax.broadcasted_iota(jnp.int32, (C, C), 0)
    col = lax.broadcasted_iota(jnp.int32, (C, C), 1)
    shift = L.bit_length() - 1
    tri = jnp.logical_and(row >= col, (row >> shift) == (col >> shift))
    rowk = lax.broadcasted_iota(jnp.int32, (C, C_DK), 0) >> shift

    b_last = jnp.zeros_like(b)
    for gi in range(SMP_GB):
        b_last = jnp.where(rowk == gi, b[(gi + 1) * L - 1:(gi + 1) * L], b_last)
    kd = k * jnp.exp(b_last - b)
    qe = q * jnp.exp(b)

    o_inter = jnp.zeros((C, C_DV), F32)
    for gi in range(SMP_GB):
        own = rowk == gi
        s0 = s0_ref[gi, 0]
        o_inter = o_inter + _dot(jnp.where(own, qe, 0.0).astype(BF), s0.astype(BF))
        u = _dot_tn(jnp.where(own, kd, 0.0).astype(BF), v)
        own_parts = _split3(jnp.where(own, la, 0.0))
        dsum = _dot_tn(own_parts[0], ones) + _dot_tn(own_parts[1], ones) + _dot_tn(own_parts[2], ones)
        d = jnp.exp(dsum)
        snew_ref[gi, 0] = jnp.concatenate([d, d], axis=1) * s0 + u

    b_mid = jnp.zeros_like(b)
    for gi in range(SMP_GB):
        b_mid = jnp.where(rowk == gi, b[gi * L + L // 2 - 1:gi * L + L // 2], b_mid)
    q2 = (q * jnp.exp(b - b_mid)).astype(BF)
    k2 = (k * jnp.exp(b_mid - b)).astype(BF)
    att = jnp.where(tri, _dot_nt(q2, k2), 0.0).astype(BF)
    o = o_inter + _dot(att, v)
    ms = jnp.mean(o * o, axis=-1, keepdims=True)
    on = o * lax.rsqrt(ms + EPS) * g_ref[...]
    o_ref[...] = (on * _silu(z_ref[...].astype(F32))).astype(o_ref.dtype)


def _gla_sample(oc, proj, loga, s0, bt, ones, g):
    C = SMP_GB * DEC_SEQ
    base = T_P // C
    return pl.pallas_call(
        _gla_sample_kernel,
        grid=(DEC_BATCH // SMP_GB, C_HEADS),
        in_specs=[
            pl.BlockSpec(memory_space=pl.ANY),
            pl.BlockSpec((C, C_DK), lambda b, h: (base + b, P_CQ // C_DK + h)),
            pl.BlockSpec((C, C_DK), lambda b, h: (base + b, P_CK // C_DK + h)),
            pl.BlockSpec((C, C_DV), lambda b, h: (base + b, P_CV // C_DV + h)),
            pl.BlockSpec((C, C_DV), lambda b, h: (base + b, P_CZ // C_DV + h)),
            pl.BlockSpec((C, C_DK), lambda b, h: (base + b, h)),
            pl.BlockSpec((SMP_GB, 1, C_DK, C_DV), lambda b, h: (b, h, 0, 0)),
            pl.BlockSpec((C, C), lambda b, h: (0, 0)),
            pl.BlockSpec((C, 128), lambda b, h: (0, 0)),
            pl.BlockSpec((1, C_DV), lambda b, h: (0, 0)),
        ],
        out_specs=[
            pl.BlockSpec((C, C_DV), lambda b, h: (base + b, h)),
            pl.BlockSpec((SMP_GB, 1, C_DK, C_DV), lambda b, h: (b, h, 0, 0)),
        ],
        out_shape=[
            jax.ShapeDtypeStruct((T_ALL, BRANCH), BF),
            jax.ShapeDtypeStruct((DEC_BATCH, C_HEADS, C_DK, C_DV), F32),
        ],
        input_output_aliases={0: 0},
        compiler_params=_cparams(2),
    )(oc, proj, proj, proj, proj, loga, s0, bt, ones, g)


def _merge_kernel(ba_ref, bb_ref, bc_ref, wb_ref, ga_ref, gb_ref, gc_ref, o_ref):
    acc = _sigmoid(ga_ref[...].astype(F32)) * _dot(ba_ref[...], wb_ref[0])
    acc = acc + _sigmoid(gb_ref[...].astype(F32)) * _dot(bb_ref[...], wb_ref[1])
    acc = acc + _sigmoid(gc_ref[...].astype(F32)) * _dot(bc_ref[...], wb_ref[2])
    o_ref[...] = acc.astype(o_ref.dtype)


def _merge(ba, bb, bc, w_branch, proj):
    gcol = lambda n: (P_G + n * D_MODEL) // TN_MRG
    bspec = pl.BlockSpec((TM, BRANCH), lambda i, j: (i, 0))
    return pl.pallas_call(
        _merge_kernel,
        grid=(T_ALL // TM, D_MODEL // TN_MRG),
        in_specs=[
            bspec, bspec, bspec,
            pl.BlockSpec((N_BRANCH, BRANCH, TN_MRG), lambda i, j: (0, 0, j)),
            pl.BlockSpec((TM, TN_MRG), lambda i, j: (i, gcol(0) + j)),
            pl.BlockSpec((TM, TN_MRG), lambda i, j: (i, gcol(1) + j)),
            pl.BlockSpec((TM, TN_MRG), lambda i, j: (i, gcol(2) + j)),
        ],
        out_specs=pl.BlockSpec((TM, TN_MRG), lambda i, j: (i, j)),
        out_shape=jax.ShapeDtypeStruct((T_ALL, D_MODEL), BF),
        compiler_params=_cparams(2),
    )(ba, bb, bc, w_branch, proj, proj, proj)


def _outproj_kernel(m_ref, w_ref, x_ref, y_ref):
    y_ref[...] = x_ref[...] + _dot(m_ref[...], w_ref[...])


def _outproj(merged, w_out, x):
    return pl.pallas_call(
        _outproj_kernel,
        grid=(T_ALL // TM, D_MODEL // TN_OUT),
        in_specs=[
            pl.BlockSpec((TM, D_MODEL), lambda i, j: (i, 0)),
            pl.BlockSpec((D_MODEL, TN_OUT), lambda i, j: (0, j)),
            pl.BlockSpec((TM, TN_OUT), lambda i, j: (i, j)),
        ],
        out_specs=pl.BlockSpec((TM, TN_OUT), lambda i, j: (i, j)),
        out_shape=jax.ShapeDtypeStruct((T_ALL, D_MODEL), F32),
        compiler_params=_cparams(2),
    )(merged, w_out, x)


def _rope_tables(pos):
    half = A_HEAD_DIM // 2
    inv = jnp.power(ROPE_THETA, -jnp.arange(half, dtype=F32) * (2.0 / A_HEAD_DIM))
    ang = pos.astype(F32)[:, None] * inv[None, :]
    cos, sin = jnp.cos(ang), jnp.sin(ang)
    cos128 = jnp.concatenate([cos, cos, cos, cos], axis=1)
    sin128 = jnp.concatenate([-sin, sin, -sin, sin], axis=1)
    return cos128, sin128


def _block_tril(n, blk):
    r = np.arange(n)
    m = (r[:, None] >= r[None, :]) & ((r[:, None] // blk) == (r[None, :] // blk))
    return jnp.asarray(m.astype(np.float32), dtype=BF)


def _reorder_w_in(w_in):
    seg = lambda o, n: w_in[:, :, o:o + n]
    main = jnp.concatenate([
        seg(_O_AQ, 1024), seg(_O_AZ, 1024), seg(_O_BU, 1024), seg(_O_BZ, 1024), seg(_O_CV, 1024),
        seg(_O_CZ, 1024), seg(_O_AK, 256), seg(_O_AV, 256), seg(_O_CQ, 512), seg(_O_CK, 512),
        seg(_O_G, N_BRANCH * D_MODEL)], axis=-1).astype(BF)
    lr = jnp.pad(seg(_O_CLR, C_GATE_RANK), ((0, 0), (0, 0), (0, 128 - C_GATE_RANK))).astype(BF)
    return main, lr


def kernel(x_prompt, x_sample, cache_a_k, cache_a_v, state_pool, state_gla, norm_g, w_in, q_norm_g, k_norm_g,
           sinks, w_pool, pool_scale, w_gate_lr, b_gate, gla_norm_g, w_branch, w_out):
    w_main, w_lr = _reorder_w_in(w_in)
    w_g = jnp.pad(w_gate_lr, ((0, 0), (0, 128 - C_GATE_RANK), (0, 0))).astype(BF)
    w_branch_bf = w_branch.astype(BF)
    w_out_bf = w_out.astype(BF)
    w_pool_bf = w_pool.astype(BF)
    cos_p, sin_p = _rope_tables(jnp.arange(SEQ))
    cos_s, sin_s = _rope_tables(PAST_LEN + jnp.arange(DEC_SEQ))
    cos_s = jnp.tile(cos_s, (SMP_BS, 1))
    sin_s = jnp.tile(sin_s, (SMP_BS, 1))
    lane = np.arange(256)
    bd = jnp.asarray((lane[:, None] // A_HEAD_DIM == lane[None, :] // A_HEAD_DIM).astype(np.float32), dtype=BF)
    bt_p = _block_tril(GLA_SC, GLA_C)
    bt_s = _block_tril(SMP_GB * DEC_SEQ, DEC_SEQ)
    ones_c = jnp.ones((GLA_C, 128), BF)
    cache_k = cache_a_k.reshape(DEPTH, DEC_BATCH, WINDOW, KV_W)
    cache_v = cache_a_v.reshape(DEPTH, DEC_BATCH, WINDOW, KV_W)
    pool_prev = jnp.pad(state_pool, ((0, 0), (0, 0), (1, 0), (0, 0)))

    x = jnp.concatenate([x_prompt.reshape(T_P, D_MODEL), x_sample.reshape(T_S, D_MODEL)], axis=0)
    pk, pv, pp, ps, sk, sv, sp, ss = [], [], [], [], [], [], [], []
    for l in range(DEPTH):
        qg = jnp.tile(q_norm_g[l], 256 // A_HEAD_DIM)[None, :]
        kg = jnp.tile(k_norm_g[l], 256 // A_HEAD_DIM)[None, :]
        proj, loga = _inproj(x, norm_g[l][None, :], w_main[l], w_lr[l], w_g[l], b_gate[l][None, :])

        oa, knew_p = _attn_prompt(proj, sinks[l], cos_p, sin_p, bd, qg, kg)
        oa, knew_s, vnew_s = _attn_sample(oa, proj, cache_k[l], cache_v[l], sinks[l], cos_s, sin_s, bd, qg, kg)
        ob, tail_p = _pool_prompt(proj, w_pool_bf[l], pool_scale[l][None, :])
        ob, tail_s = _pool_sample(ob, proj, pool_prev[l], w_pool_bf[l], pool_scale[l][None, :])
        oc, s_p = _gla_prompt(proj, loga, bt_p, ones_c, gla_norm_g[l][None, :])
        oc, s_s = _gla_sample(oc, proj, loga, state_gla[l], bt_s, ones_c, gla_norm_g[l][None, :])

        merged = _merge(oa, ob, oc, w_branch_bf[l], proj)
        x = _outproj(merged, w_out_bf[l], x)

        vp = proj[:T_P, P_AV:P_AV + KV_W].astype(F32).reshape(BATCH, SEQ, KV_W)[:, SEQ - WINDOW:]
        pk.append(knew_p.reshape(BATCH, WINDOW, A_KV_HEADS, A_HEAD_DIM))
        pv.append(vp.reshape(BATCH, WINDOW, A_KV_HEADS, A_HEAD_DIM))
        pp.append(tail_p[:, 1:])
        ps.append(s_p)
        sk.append(knew_s.reshape(DEC_BATCH, WINDOW, A_KV_HEADS, A_HEAD_DIM))
        sv.append(vnew_s.reshape(DEC_BATCH, WINDOW, A_KV_HEADS, A_HEAD_DIM))
        sp.append(tail_s[:, 1:])
        ss.append(s_s)

    yp = x[:T_P].reshape(BATCH, SEQ, D_MODEL)
    ys = x[T_P:].reshape(DEC_BATCH, DEC_SEQ, D_MODEL)
    return (yp, ys, jnp.stack(pk), jnp.stack(pv), jnp.stack(pp), jnp.stack(ps),
            jnp.stack(sk), jnp.stack(sv), jnp.stack(sp), jnp.stack(ss))
```

```python
import numpy as np
import jax
import jax.numpy as jnp
from jax import lax
from jax.experimental import pallas as pl
from jax.experimental.pallas import tpu as pltpu

F32 = jnp.float32
BF = jnp.bfloat16

D_MODEL = 2048
BATCH = 4
SEQ = 2048
DEPTH = 4
DEC_BATCH = 32
DEC_SEQ = 8
PAST_LEN = 16384
BRANCH = D_MODEL // 2
N_BRANCH = 3
A_HEADS = 16
A_KV_HEADS = 4
A_HEAD_DIM = 64
A_GROUP = A_HEADS // A_KV_HEADS
WINDOW = 128
ROPE_THETA = 10000.0
POOL_WINDOWS = (2, 4, 8, 16)
POOL_GROUP_DIM = 256
POOL_HIST = 15
C_HEADS = 4
C_KEY = 512
C_DK = 128
C_DV = 256
C_GATE_RANK = 16
C_GATE_TAU = 16.0
EPS = 1e-6
KV_W = A_KV_HEADS * A_HEAD_DIM
IN_COLS = 13840

T_P = BATCH * SEQ
T_S = DEC_BATCH * DEC_SEQ
T_ALL = T_P + T_S

A_AQ, A_AK, A_AV, A_AZ = 0, 1024, 1280, 1536
A_BU, A_BZ = 2560, 3584
A_CQ, A_CK, A_CV = 4608, 5120, 5632
A_COLS = 6656
LR_SHIFT = C_GATE_RANK
B_COLS = IN_COLS - A_COLS - LR_SHIFT
B_G, B_CZ = 0, N_BRANCH * D_MODEL

LANES = 128
TM = 768
TN_A = 1664
TN_B = 1024
W_SUB = 512
CAST_ROWS = 256
TM_OUT = 256
TM_NORM = 768
ATT_BLK = WINDOW
POOL_TP = 512
POOL_HALO = 16
GLA_C = 64
GLA_NSUB = 8
GLA_SC = GLA_C * GLA_NSUB
SMP_BS = 2
SMP_GB = 8
VMEM_LIMIT = 56 * 1024 * 1024
NEG_BIG = -1e30


def _cparams(n_axes):
    return pltpu.CompilerParams(
        dimension_semantics=("arbitrary",) * n_axes, vmem_limit_bytes=VMEM_LIMIT)


def _sigmoid(x):
    return 1.0 / (1.0 + jnp.exp(-x))


def _silu(x):
    return x * _sigmoid(x)


def _dot(a, b):
    return jnp.dot(a, b, preferred_element_type=F32)


def _dot_nt(a, b):
    return lax.dot_general(a, b, (((1,), (1,)), ((), ())), preferred_element_type=F32)


def _dot_tn(a, b):
    return lax.dot_general(a, b, (((0,), (0,)), ((), ())), preferred_element_type=F32)


def _split3(x):
    hi = x.astype(BF)
    r1 = x - hi.astype(F32)
    mid = r1.astype(BF)
    lo = (r1 - mid.astype(F32)).astype(BF)
    return hi, mid, lo


def _any_spec():
    return pl.BlockSpec(memory_space=pl.ANY)


def _row_spec(l, width):
    return pl.BlockSpec((None, 1, width), lambda *a: (l, 0, 0))


def _norm_gate(xf, g_ref, wlr_ref, wg_ref, bg_ref):
    ms = jnp.mean(xf * xf, axis=-1, keepdims=True)
    h = (xf * lax.rsqrt(ms + EPS) * g_ref[...]).astype(BF)
    lr = _dot(h, wlr_ref[...])
    z = _dot(lr.astype(BF), wg_ref[...]) + bg_ref[...]
    log_sig = jnp.minimum(z, 0.0) - jnp.log1p(jnp.exp(-jnp.abs(z)))
    return h, log_sig * (1.0 / C_GATE_TAU)


def _norm_specs(l, imap):
    return [
        _row_spec(l, D_MODEL),
        pl.BlockSpec((None, D_MODEL, LANES), lambda *a: (l, 0, 0)),
        pl.BlockSpec((None, LANES, C_KEY), lambda *a: (l, 0, 0)),
        _row_spec(l, C_KEY),
    ]


def _norm_kernel(x_ref, g_ref, wlr_ref, wg_ref, bg_ref, h_ref, loga_ref):
    h, loga = _norm_gate(x_ref[...], g_ref, wlr_ref, wg_ref, bg_ref)
    h_ref[...] = h
    loga_ref[...] = loga


def _norm(x, l, norm_g, w_lr, w_g, b_gate):
    return pl.pallas_call(
        _norm_kernel,
        grid=(T_ALL // TM_NORM,),
        in_specs=[pl.BlockSpec((TM_NORM, D_MODEL), lambda i: (i, 0))] + _norm_specs(l, None),
        out_specs=[pl.BlockSpec((TM_NORM, D_MODEL), lambda i: (i, 0)),
                   pl.BlockSpec((TM_NORM, C_KEY), lambda i: (i, 0))],
        out_shape=[jax.ShapeDtypeStruct((T_ALL, D_MODEL), BF),
                   jax.ShapeDtypeStruct((T_ALL, C_KEY), F32)],
        compiler_params=_cparams(1),
    )(x, norm_g, w_lr, w_g, b_gate)


def _inproj_a_kernel(h_ref, w_ref, o_ref, w_scr):
    @pl.when(pl.program_id(1) == 0)
    def _():
        for r in range(0, D_MODEL, CAST_ROWS):
            w_scr[r:r + CAST_ROWS, :] = w_ref[r:r + CAST_ROWS, :].astype(BF)

    o_ref[...] = _dot(h_ref[...], w_scr[...]).astype(o_ref.dtype)


def _inproj_a(h, w_in, l):
    return pl.pallas_call(
        _inproj_a_kernel,
        grid=(A_COLS // TN_A, T_ALL // TM),
        in_specs=[
            pl.BlockSpec((TM, D_MODEL), lambda j, i: (i, 0)),
            pl.BlockSpec((None, D_MODEL, TN_A), lambda j, i: (l, 0, j)),
        ],
        out_specs=pl.BlockSpec((TM, TN_A), lambda j, i: (i, j)),
        out_shape=jax.ShapeDtypeStruct((T_ALL, A_COLS), BF),
        scratch_shapes=[pltpu.VMEM((D_MODEL, TN_A), BF)],
        compiler_params=_cparams(2),
    )(h, w_in)


def _inproj_b_kernel(h_ref, w0_ref, w1_ref, w2_ref, o_ref, w_scr):
    @pl.when(pl.program_id(1) == 0)
    def _():
        keep = lax.broadcasted_iota(jnp.int32, (CAST_ROWS, LANES), 1) < (LANES - LR_SHIFT)
        nsub = W_SUB // LANES
        for r in range(0, D_MODEL, CAST_ROWS):
            rows = slice(r, r + CAST_ROWS)
            pieces = [w0_ref[rows, c * LANES:(c + 1) * LANES] for c in range(nsub)]
            pieces += [w1_ref[rows, c * LANES:(c + 1) * LANES] for c in range(nsub)]
            pieces.append(w2_ref[rows, :])
            rolled = [pltpu.roll(p, LANES - LR_SHIFT, 1) for p in pieces]
            for c in range(TN_B // LANES):
                w_scr[rows, c * LANES:(c + 1) * LANES] = jnp.where(keep, rolled[c], rolled[c + 1]).astype(BF)

    o_ref[...] = _dot(h_ref[...], w_scr[...]).astype(o_ref.dtype)


def _inproj_b(h, w_in, l):
    nt = B_COLS // TN_B
    sub0 = A_COLS // W_SUB
    per = TN_B // W_SUB
    tail0 = (A_COLS + TN_B) // LANES
    return pl.pallas_call(
        _inproj_b_kernel,
        grid=(nt, T_ALL // TM),
        in_specs=[
            pl.BlockSpec((TM, D_MODEL), lambda j, i: (i, 0)),
            pl.BlockSpec((None, D_MODEL, W_SUB), lambda j, i: (l, 0, sub0 + per * j)),
            pl.BlockSpec((None, D_MODEL, W_SUB), lambda j, i: (l, 0, sub0 + per * j + 1)),
            pl.BlockSpec((None, D_MODEL, LANES), lambda j, i: (l, 0, tail0 + (TN_B // LANES) * j)),
        ],
        out_specs=pl.BlockSpec((TM, TN_B), lambda j, i: (i, (j + nt - 1) % nt)),
        out_shape=jax.ShapeDtypeStruct((T_ALL, B_COLS), BF),
        scratch_shapes=[pltpu.VMEM((D_MODEL, TN_B), BF)],
        compiler_params=_cparams(2),
    )(h, w_in, w_in, w_in)


def _head_norm(x, bd, g_row):
    outs = []
    for c in range(x.shape[1] // 256):
        xc = x[:, c * 256:(c + 1) * 256]
        ss = _dot((xc * xc).astype(BF), bd)
        outs.append(xc * lax.rsqrt(ss * (1.0 / A_HEAD_DIM) + EPS) * g_row)
    return outs


def _rope128(xc, cos, sin, first_half):
    swapped = jnp.where(first_half, pltpu.roll(xc, 96, 1), pltpu.roll(xc, 32, 1))
    return xc * cos + swapped * sin


def _norm_rope(x, bd, g_row, cos, sin, first_half):
    chunks = []
    for blk in _head_norm(x, bd, g_row):
        for c in range(2):
            chunks.append(_rope128(blk[:, c * 128:(c + 1) * 128], cos, sin, first_half))
    return chunks


def _dup_half(chunk, rolled, lo_half, use_low):
    return jnp.where(lo_half, chunk, rolled) if use_low else jnp.where(lo_half, rolled, chunk)


def _attn_prompt_kernel(l, sinks_ref, q_ref, z0_ref, z1_ref, k_ref, v_ref, cos_ref, sin_ref, bd_ref, qg_ref,
                        kg_ref, o_ref, knew_ref, kprev, vprev_e, vprev_o):
    n = pl.program_id(1)
    R = ATT_BLK
    lane = lax.broadcasted_iota(jnp.int32, (R, 128), 1)
    lo_half = lane < 64
    hi_half = jnp.logical_not(lo_half)
    first_half = (lane & 32) == 0
    cos = cos_ref[...]
    sin = sin_ref[...]
    bd = bd_ref[...]

    @pl.when(n == 0)
    def _():
        kprev[...] = jnp.zeros_like(kprev)
        vprev_e[...] = jnp.zeros_like(vprev_e)
        vprev_o[...] = jnp.zeros_like(vprev_o)

    kch = _norm_rope(k_ref[...].astype(F32), bd, kg_ref[...], cos, sin, first_half)
    knew_ref[0] = jnp.concatenate(kch, axis=1)
    vraw = v_ref[...].astype(F32)
    vch = [vraw[:, :128], vraw[:, 128:]]
    kcur, vcur_e, vcur_o = [], [], []
    for h in range(A_KV_HEADS):
        c, low = h // 2, (h % 2 == 0)
        kcur.append(_dup_half(kch[c], pltpu.roll(kch[c], 64, 1), lo_half, low).astype(BF))
        v_lo = vch[c] if low else pltpu.roll(vch[c], 64, 1)
        v_hi = pltpu.roll(vch[c], 64, 1) if low else vch[c]
        vcur_e.append(jnp.where(lo_half, v_lo, 1.0).astype(BF))
        vcur_o.append(jnp.where(lo_half, 1.0, v_hi).astype(BF))

    qch = _norm_rope(q_ref[...].astype(F32), bd, qg_ref[...], cos, sin, first_half)

    G = A_GROUP * R
    srow = lax.broadcasted_iota(jnp.int32, (G, 128), 0)
    scol = lax.broadcasted_iota(jnp.int32, (G, 128), 1)
    qi = srow & (R - 1)
    mask_prev = jnp.logical_and(scol >= qi, n > 0)
    mask_cur = scol <= qi
    blk_row = lax.broadcasted_iota(jnp.int32, (G, 1), 0) >> (R.bit_length() - 1)

    for h in range(A_KV_HEADS):
        qa = qch[2 * h] * (A_HEAD_DIM ** -0.5)
        qb = qch[2 * h + 1] * (A_HEAD_DIM ** -0.5)
        q_stack = jnp.concatenate([jnp.where(lo_half, qa, 0.0), jnp.where(lo_half, qb, 0.0),
                                   jnp.where(hi_half, qa, 0.0), jnp.where(hi_half, qb, 0.0)],
                                  axis=0).astype(BF)
        head_of_blk = (4 * h, 4 * h + 2, 4 * h + 1, 4 * h + 3)
        sink = jnp.zeros((G, 1), F32)
        for bi, hd in enumerate(head_of_blk):
            sink = jnp.where(blk_row == bi, sinks_ref[l, hd], sink)
        s_p = jnp.where(mask_prev, _dot_nt(q_stack, kprev[h]), NEG_BIG)
        s_c = jnp.where(mask_cur, _dot_nt(q_stack, kcur[h]), NEG_BIG)
        m = jnp.maximum(jnp.max(jnp.maximum(s_p, s_c), axis=-1, keepdims=True), sink)
        p_p = jnp.exp(s_p - m).astype(BF)
        p_c = jnp.exp(s_c - m).astype(BF)
        e_sink = jnp.exp(sink - m)
        half = G // 2
        pv_e = _dot(p_p[:half], vprev_e[h]) + _dot(p_c[:half], vcur_e[h])
        pv_o = _dot(p_p[half:], vprev_o[h]) + _dot(p_c[half:], vcur_o[h])
        for cc in range(2):
            rows_e = slice(cc * R, (cc + 1) * R)
            ev, od = pv_e[rows_e], pv_o[rows_e]
            num = jnp.where(lo_half, ev, od)
            den = pltpu.roll(jnp.where(lo_half, od, ev), 64, 1)
            den = den + jnp.where(lo_half, e_sink[rows_e], e_sink[half + cc * R:half + (cc + 1) * R])
            c = 2 * h + cc
            z_ref = z0_ref if c < 4 else z1_ref
            zc = z_ref[:, (c % 4) * 128:(c % 4 + 1) * 128].astype(F32)
            o_ref[:, c * 128:(c + 1) * 128] = (num / den * _silu(zc)).astype(o_ref.dtype)

    for h in range(A_KV_HEADS):
        kprev[h] = kcur[h]
        vprev_e[h] = vcur_e[h]
        vprev_o[h] = vcur_o[h]


def _attn_prompt(proj_a, l, sinks, cos, sin, bd, qg, kg):
    nb = SEQ // ATT_BLK
    rb = lambda b, n: b * nb + n
    kern = lambda *refs: _attn_prompt_kernel(l, *refs)
    return pl.pallas_call(
        kern,
        grid=(BATCH, nb),
        in_specs=[
            pl.BlockSpec(memory_space=pltpu.SMEM),
            pl.BlockSpec((ATT_BLK, BRANCH), lambda b, n: (rb(b, n), A_AQ // BRANCH)),
            pl.BlockSpec((ATT_BLK, 512), lambda b, n: (rb(b, n), A_AZ // 512)),
            pl.BlockSpec((ATT_BLK, 512), lambda b, n: (rb(b, n), A_AZ // 512 + 1)),
            pl.BlockSpec((ATT_BLK, KV_W), lambda b, n: (rb(b, n), A_AK // KV_W)),
            pl.BlockSpec((ATT_BLK, KV_W), lambda b, n: (rb(b, n), A_AV // KV_W)),
            pl.BlockSpec((ATT_BLK, 128), lambda b, n: (n, 0)),
            pl.BlockSpec((ATT_BLK, 128), lambda b, n: (n, 0)),
            pl.BlockSpec((256, 256), lambda b, n: (0, 0)),
            _row_spec(l, 256),
            _row_spec(l, 256),
        ],
        out_specs=[
            pl.BlockSpec((ATT_BLK, BRANCH), lambda b, n: (rb(b, n), 0)),
            pl.BlockSpec((1, ATT_BLK, KV_W), lambda b, n: (b, 0, 0)),
        ],
        out_shape=[
            jax.ShapeDtypeStruct((T_ALL, BRANCH), BF),
            jax.ShapeDtypeStruct((BATCH, WINDOW, KV_W), F32),
        ],
        scratch_shapes=[pltpu.VMEM((A_KV_HEADS, ATT_BLK, 128), BF),
                        pltpu.VMEM((A_KV_HEADS, ATT_BLK, 128), BF),
                        pltpu.VMEM((A_KV_HEADS, ATT_BLK, 128), BF)],
        compiler_params=_cparams(2),
    )(sinks, proj_a, proj_a, proj_a, proj_a, proj_a, cos, sin, bd, qg, kg)


def _attn_sample_kernel(l, n_alias, *refs):
    refs = refs[n_alias:]
    (sinks_ref, q_ref, z0_ref, z1_ref, k_ref, v_ref, ck_ref, cv_ref, cos_ref, sin_ref,
     bd_ref, qg_ref, kg_ref, o_ref, nk_ref, nv_ref, kall, vall) = refs
    L = DEC_SEQ
    R = SMP_BS * L
    lane = lax.broadcasted_iota(jnp.int32, (R, 128), 1)
    first_half = (lane & 32) == 0
    lane8 = lax.broadcasted_iota(jnp.int32, (L, 128), 1)
    lo8 = lane8 < 64
    lane_c = lax.broadcasted_iota(jnp.int32, (WINDOW, 128), 1)
    lo_c = lane_c < 64
    cos = cos_ref[...]
    sin = sin_ref[...]
    bd = bd_ref[...]

    kall[...] = jnp.zeros_like(kall)
    vall[...] = jnp.zeros_like(vall)

    kch = _norm_rope(k_ref[...].astype(F32), bd, kg_ref[...], cos, sin, first_half)
    vraw = v_ref[...].astype(F32)
    vch = [vraw[:, :128], vraw[:, 128:]]
    qch = _norm_rope(q_ref[...].astype(F32), bd, qg_ref[...], cos, sin, first_half)

    rows = A_GROUP * L
    srow = lax.broadcasted_iota(jnp.int32, (rows, 2 * WINDOW), 0)
    scol = lax.broadcasted_iota(jnp.int32, (rows, 2 * WINDOW), 1)
    t = srow & (L - 1)
    mask = jnp.logical_or(jnp.logical_and(scol < WINDOW, scol >= t),
                          jnp.logical_and(scol >= WINDOW, (scol - WINDOW) <= t))
    hrow = lax.broadcasted_iota(jnp.int32, (rows, 1), 0) >> (L.bit_length() - 1)
    z_all = jnp.concatenate([z0_ref[...], z1_ref[...]], axis=1).astype(F32)
    o_rows = []

    for bi in range(SMP_BS):
        r0 = bi * L
        knew = [kc[r0:r0 + L] for kc in kch]
        vnew = [vc[r0:r0 + L] for vc in vch]
        ck = ck_ref[bi]
        cv = cv_ref[bi]
        nk_ref[bi, 0:WINDOW - L] = ck[L:WINDOW]
        nv_ref[bi, 0:WINDOW - L] = cv[L:WINDOW]
        nk_ref[bi, WINDOW - L:WINDOW] = jnp.concatenate(knew, axis=1)
        nv_ref[bi, WINDOW - L:WINDOW] = jnp.concatenate(vnew, axis=1)
        for h in range(A_KV_HEADS):
            c, low = h // 2, (h % 2 == 0)
            ckc = ck[:, c * 128:(c + 1) * 128]
            cvc = cv[:, c * 128:(c + 1) * 128]
            kall[h, 0:WINDOW] = _dup_half(ckc, pltpu.roll(ckc, 64, 1), lo_c, low)
            vall[h, 0:WINDOW] = _dup_half(cvc, pltpu.roll(cvc, 64, 1), lo_c, low)
            kall[h, WINDOW:WINDOW + L] = _dup_half(knew[c], pltpu.roll(knew[c], 64, 1), lo8, low)
            vall[h, WINDOW:WINDOW + L] = _dup_half(vnew[c], pltpu.roll(vnew[c], 64, 1), lo8, low)

        o_chunks = []
        for h in range(A_KV_HEADS):
            qs = []
            for gi in range(A_GROUP):
                i = h * A_GROUP + gi
                sel = lo8 if i % 2 == 0 else jnp.logical_not(lo8)
                qs.append(jnp.where(sel, qch[i // 2][r0:r0 + L] * (A_HEAD_DIM ** -0.5), 0.0))
            qm = jnp.concatenate(qs, axis=0).astype(BF)
            s = jnp.where(mask, _dot_nt(qm, kall[h].astype(BF)), NEG_BIG)
            sink = jnp.zeros((rows, 1), F32)
            for gi in range(A_GROUP):
                sink = jnp.where(hrow == gi, sinks_ref[l, h * A_GROUP + gi], sink)
            m = jnp.maximum(jnp.max(s, axis=-1, keepdims=True), sink)
            p = jnp.exp(s - m)
            den = jnp.sum(p, axis=-1, keepdims=True) + jnp.exp(sink - m)
            pv = _dot(p.astype(BF), vall[h].astype(BF)) / den
            for cc in range(2):
                o_chunks.append(jnp.where(lo8, pv[(2 * cc) * L:(2 * cc + 1) * L],
                                          pv[(2 * cc + 1) * L:(2 * cc + 2) * L]))
        o = jnp.concatenate(o_chunks, axis=1)
        o_rows.append(o * _silu(z_all[r0:r0 + L]))
    o_ref[...] = jnp.concatenate(o_rows, axis=0).astype(o_ref.dtype)


def _attn_sample(oa, nk_buf, nv_buf, proj_a, l, cache_k, cache_v, sinks, cos, sin, bd, qg, kg):
    R = SMP_BS * DEC_SEQ
    base = T_P // R
    alias_in = [oa] + ([nk_buf, nv_buf] if l > 0 else [])
    n_alias = len(alias_in)
    kern = lambda *refs: _attn_sample_kernel(l, n_alias, *refs)
    cache_spec = pl.BlockSpec((None, SMP_BS, WINDOW, KV_W), lambda b: (l, b, 0, 0))
    return pl.pallas_call(
        kern,
        grid=(DEC_BATCH // SMP_BS,),
        in_specs=[_any_spec()] * n_alias + [
            pl.BlockSpec(memory_space=pltpu.SMEM),
            pl.BlockSpec((R, BRANCH), lambda b: (base + b, A_AQ // BRANCH)),
            pl.BlockSpec((R, 512), lambda b: (base + b, A_AZ // 512)),
            pl.BlockSpec((R, 512), lambda b: (base + b, A_AZ // 512 + 1)),
            pl.BlockSpec((R, KV_W), lambda b: (base + b, A_AK // KV_W)),
            pl.BlockSpec((R, KV_W), lambda b: (base + b, A_AV // KV_W)),
            cache_spec, cache_spec,
            pl.BlockSpec((R, 128), lambda b: (0, 0)),
            pl.BlockSpec((R, 128), lambda b: (0, 0)),
            pl.BlockSpec((256, 256), lambda b: (0, 0)),
            _row_spec(l, 256),
            _row_spec(l, 256),
        ],
        out_specs=[pl.BlockSpec((R, BRANCH), lambda b: (base + b, 0)), cache_spec, cache_spec],
        out_shape=[
            jax.ShapeDtypeStruct((T_ALL, BRANCH), BF),
            jax.ShapeDtypeStruct((DEPTH, DEC_BATCH, WINDOW, KV_W), F32),
            jax.ShapeDtypeStruct((DEPTH, DEC_BATCH, WINDOW, KV_W), F32),
        ],
        scratch_shapes=[pltpu.VMEM((A_KV_HEADS, 2 * WINDOW, 128), F32),
                        pltpu.VMEM((A_KV_HEADS, 2 * WINDOW, 128), F32)],
        input_output_aliases={i: i for i in range(n_alias)},
        compiler_params=_cparams(1),
    )(*alias_in, sinks, proj_a, proj_a, proj_a, proj_a, proj_a, cache_k, cache_v, cos, sin, bd, qg, kg)


def _pool_compute(ext, u_rows, start_pos, wp_ref, scale_ref, z, n_rows):
    pos = start_pos + lax.broadcasted_iota(jnp.int32, (n_rows, 1), 0)
    outs = []
    for g, w in enumerate(POOL_WINDOWS):
        sl = slice(g * POOL_GROUP_DIM, (g + 1) * POOL_GROUP_DIM)
        acc = ext[:, sl]
        span = 1
        while span < w:
            acc = acc + pltpu.roll(acc, span, 0)
            span *= 2
        win_sum = acc[POOL_HALO:]
        count = jnp.minimum(w, pos + 1).astype(F32)
        d = (win_sum / count - u_rows[:, sl]).astype(BF)
        outs.append(_dot(d, wp_ref[g]))
    y = jnp.concatenate(outs, axis=1) * scale_ref[...]
    return y * _silu(z)


def _pool_prompt_kernel(u0_ref, u1_ref, z0_ref, z1_ref, wp_ref, scale_ref, o_ref, tail_ref, ext):
    t = pl.program_id(1)

    @pl.when(t == 0)
    def _():
        ext[0:POOL_HALO] = jnp.zeros((POOL_HALO, BRANCH), F32)

    u = jnp.concatenate([u0_ref[...], u1_ref[...]], axis=1).astype(F32)
    z = jnp.concatenate([z0_ref[...], z1_ref[...]], axis=1).astype(F32)
    ext[POOL_HALO:] = u
    out = _pool_compute(ext[...], u, t * POOL_TP, wp_ref, scale_ref, z, POOL_TP)
    o_ref[...] = out.astype(o_ref.dtype)
    tail = ext[POOL_TP:POOL_TP + POOL_HALO]
    tail_ref[0] = tail
    ext[0:POOL_HALO] = tail


def _pool_w_specs(l):
    ng = len(POOL_WINDOWS)
    return [pl.BlockSpec((None, ng, POOL_GROUP_DIM, POOL_GROUP_DIM), lambda *a: (l, 0, 0, 0)),
            _row_spec(l, BRANCH)]


def _pool_prompt(proj_a, l, w_pool, scale):
    nt = SEQ // POOL_TP
    half = lambda off, k: pl.BlockSpec((POOL_TP, 512), lambda b, t: (b * nt + t, off // 512 + k))
    return pl.pallas_call(
        _pool_prompt_kernel,
        grid=(BATCH, nt),
        in_specs=[half(A_BU, 0), half(A_BU, 1), half(A_BZ, 0), half(A_BZ, 1)] + _pool_w_specs(l),
        out_specs=[
            pl.BlockSpec((POOL_TP, BRANCH), lambda b, t: (b * nt + t, 0)),
            pl.BlockSpec((1, POOL_HALO, BRANCH), lambda b, t: (b, 0, 0)),
        ],
        out_shape=[
            jax.ShapeDtypeStruct((T_ALL, BRANCH), BF),
            jax.ShapeDtypeStruct((BATCH, POOL_HALO, BRANCH), F32),
        ],
        scratch_shapes=[pltpu.VMEM((POOL_HALO + POOL_TP, BRANCH), F32)],
        compiler_params=_cparams(2),
    )(proj_a, proj_a, proj_a, proj_a, w_pool, scale)


def _pool_sample_kernel(ob_in_ref, u0_ref, u1_ref, z0_ref, z1_ref, prev_ref, wp_ref, scale_ref, o_ref, tail_ref):
    del ob_in_ref
    L = DEC_SEQ
    u_all = jnp.concatenate([u0_ref[...], u1_ref[...]], axis=1).astype(F32)
    z_all = jnp.concatenate([z0_ref[...], z1_ref[...]], axis=1).astype(F32)
    outs = []
    for bi in range(SMP_BS):
        u = u_all[bi * L:(bi + 1) * L]
        ext = jnp.concatenate([prev_ref[bi], u], axis=0)
        outs.append(_pool_compute(ext, u, PAST_LEN, wp_ref, scale_ref, z_all[bi * L:(bi + 1) * L], L))
        tail_ref[bi] = ext[L:L + POOL_HALO]
    o_ref[...] = jnp.concatenate(outs, axis=0).astype(o_ref.dtype)


def _pool_sample(ob, proj_a, l, prev16, w_pool, scale):
    R = SMP_BS * DEC_SEQ
    base = T_P // R
    half = lambda off, k: pl.BlockSpec((R, 512), lambda b: (base + b, off // 512 + k))
    return pl.pallas_call(
        _pool_sample_kernel,
        grid=(DEC_BATCH // SMP_BS,),
        in_specs=[_any_spec(), half(A_BU, 0), half(A_BU, 1), half(A_BZ, 0), half(A_BZ, 1),
                  pl.BlockSpec((None, SMP_BS, POOL_HALO, BRANCH), lambda b: (l, b, 0, 0))] + _pool_w_specs(l),
        out_specs=[
            pl.BlockSpec((R, BRANCH), lambda b: (base + b, 0)),
            pl.BlockSpec((SMP_BS, POOL_HALO, BRANCH), lambda b: (b, 0, 0)),
        ],
        out_shape=[
            jax.ShapeDtypeStruct((T_ALL, BRANCH), BF),
            jax.ShapeDtypeStruct((DEC_BATCH, POOL_HALO, BRANCH), F32),
        ],
        input_output_aliases={0: 0},
        compiler_params=_cparams(1),
    )(ob, proj_a, proj_a, proj_a, proj_a, prev16, w_pool, scale)


def _gla_chunk_out(q, k, v_bf, b, s_bf, tri, g_row, z):
    c = q.shape[0]
    b_mid = b[c // 2 - 1:c // 2]
    o = _dot((q * jnp.exp(b)).astype(BF), s_bf)
    q2 = (q * jnp.exp(b - b_mid)).astype(BF)
    k2 = (k * jnp.exp(b_mid - b)).astype(BF)
    att = jnp.where(tri, _dot_nt(q2, k2), 0.0).astype(BF)
    o = o + _dot(att, v_bf)
    ms = jnp.mean(o * o, axis=-1, keepdims=True)
    on = o * lax.rsqrt(ms + EPS) * g_row
    return on * _silu(z)


def _gla_prompt_kernel(n_alias, *refs):
    refs = refs[n_alias:]
    (q_ref, k_ref, v_ref, z_ref, la_ref, bt_ref, ones_ref, g_ref, o_ref, snew_ref,
     s_scr, u_scr, d_scr, sc_scr) = refs

    @pl.when(pl.program_id(2) == 0)
    def _():
        s_scr[...] = jnp.zeros_like(s_scr)

    C = GLA_C
    parts = _split3(la_ref[...])
    bt = bt_ref[...]
    b_all = _dot(bt, parts[0]) + _dot(bt, parts[1]) + _dot(bt, parts[2])
    q = q_ref[...].astype(F32) * (C_DK ** -0.5)
    k = k_ref[...].astype(F32)
    v = v_ref[...].astype(BF)
    ones = ones_ref[...]
    tri = (lax.broadcasted_iota(jnp.int32, (C, C), 0) >= lax.broadcasted_iota(jnp.int32, (C, C), 1))

    for c in range(GLA_NSUB):
        sl = slice(c * C, (c + 1) * C)
        b = b_all[sl]
        kd = (k[sl] * jnp.exp(b[C - 1:C] - b)).astype(BF)
        u_scr[c] = _dot_tn(kd, v[sl])
        dsum = _dot_tn(parts[0][sl], ones) + _dot_tn(parts[1][sl], ones) + _dot_tn(parts[2][sl], ones)
        d_scr[c] = jnp.exp(dsum)

    s = s_scr[...]
    for c in range(GLA_NSUB):
        sc_scr[c] = s.astype(BF)
        d = d_scr[c]
        s = jnp.concatenate([d, d], axis=1) * s + u_scr[c]
    s_scr[...] = s
    snew_ref[0, 0] = s

    g_row = g_ref[...]
    for c in range(GLA_NSUB):
        sl = slice(c * C, (c + 1) * C)
        out = _gla_chunk_out(q[sl], k[sl], v[sl], b_all[sl], sc_scr[c], tri, g_row,
                             z_ref[sl, :].astype(F32))
        o_ref[sl, :] = out.astype(o_ref.dtype)


def _gla_prompt(s_buf, proj_a, proj_b, loga, l, bt, ones, g):
    ns = SEQ // GLA_SC
    rb = lambda b, h, s: b * ns + s
    alias_in = [s_buf] if l > 0 else []
    n_alias = len(alias_in)
    kern = lambda *refs: _gla_prompt_kernel(n_alias, *refs)
    return pl.pallas_call(
        kern,
        grid=(BATCH, C_HEADS, ns),
        in_specs=[_any_spec()] * n_alias + [
            pl.BlockSpec((GLA_SC, C_DK), lambda b, h, s: (rb(b, h, s), A_CQ // C_DK + h)),
            pl.BlockSpec((GLA_SC, C_DK), lambda b, h, s: (rb(b, h, s), A_CK // C_DK + h)),
            pl.BlockSpec((GLA_SC, C_DV), lambda b, h, s: (rb(b, h, s), A_CV // C_DV + h)),
            pl.BlockSpec((GLA_SC, C_DV), lambda b, h, s: (rb(b, h, s), B_CZ // C_DV + h)),
            pl.BlockSpec((GLA_SC, C_DK), lambda b, h, s: (rb(b, h, s), h)),
            pl.BlockSpec((GLA_SC, GLA_SC), lambda b, h, s: (0, 0)),
            pl.BlockSpec((GLA_C, 128), lambda b, h, s: (0, 0)),
            _row_spec(l, C_DV),
        ],
        out_specs=[
            pl.BlockSpec((GLA_SC, C_DV), lambda b, h, s: (rb(b, h, s), h)),
            pl.BlockSpec((None, 1, 1, C_DK, C_DV), lambda b, h, s: (l, b, h, 0, 0)),
        ],
        out_shape=[
            jax.ShapeDtypeStruct((T_ALL, BRANCH), BF),
            jax.ShapeDtypeStruct((DEPTH, BATCH, C_HEADS, C_DK, C_DV), F32),
        ],
        scratch_shapes=[
            pltpu.VMEM((C_DK, C_DV), F32),
            pltpu.VMEM((GLA_NSUB, C_DK, C_DV), F32),
            pltpu.VMEM((GLA_NSUB, C_DK, 128), F32),
            pltpu.VMEM((GLA_NSUB, C_DK, C_DV), BF),
        ],
        input_output_aliases={0: 1} if l > 0 else {},
        compiler_params=_cparams(3),
    )(*alias_in, proj_a, proj_a, proj_a, proj_b, loga, bt, ones, g)


def _gla_sample_kernel(n_alias, *refs):
    refs = refs[n_alias:]
    q_ref, k_ref, v_ref, z_ref, la_ref, s0_ref, bt_ref, ones_ref, g_ref, o_ref, snew_ref = refs
    C = SMP_GB * DEC_SEQ
    L = DEC_SEQ
    la = la_ref[...]
    parts = _split3(la)
    bt = bt_ref[...]
    b = _dot(bt, parts[0]) + _dot(bt, parts[1]) + _dot(bt, parts[2])
    q = q_ref[...].astype(F32) * (C_DK ** -0.5)
    k = k_ref[...].astype(F32)
    v = v_ref[...].astype(BF)
    ones = ones_ref[...]
    row = lax.broadcasted_iota(jnp.int32, (C, C), 0)
    col = lax.broadcasted_iota(jnp.int32, (C, C), 1)
    shift = L.bit_length() - 1
    tri = jnp.logical_and(row >= col, (row >> shift) == (col >> shift))
    rowk = lax.broadcasted_iota(jnp.int32, (C, C_DK), 0) >> shift

    b_last = jnp.zeros_like(b)
    for gi in range(SMP_GB):
        b_last = jnp.where(rowk == gi, b[(gi + 1) * L - 1:(gi + 1) * L], b_last)
    kd = k * jnp.exp(b_last - b)
    qe = q * jnp.exp(b)

    o_inter = jnp.zeros((C, C_DV), F32)
    for gi in range(SMP_GB):
        own = rowk == gi
        s0 = s0_ref[gi, 0]
        o_inter = o_inter + _dot(jnp.where(own, qe, 0.0).astype(BF), s0.astype(BF))
        u = _dot_tn(jnp.where(own, kd, 0.0).astype(BF), v)
        own_parts = _split3(jnp.where(own, la, 0.0))
        dsum = _dot_tn(own_parts[0], ones) + _dot_tn(own_parts[1], ones) + _dot_tn(own_parts[2], ones)
        d = jnp.exp(dsum)
        snew_ref[gi, 0] = jnp.concatenate([d, d], axis=1) * s0 + u

    b_mid = jnp.zeros_like(b)
    for gi in range(SMP_GB):
        b_mid = jnp.where(rowk == gi, b[gi * L + L // 2 - 1:gi * L + L // 2], b_mid)
    q2 = (q * jnp.exp(b - b_mid)).astype(BF)
    k2 = (k * jnp.exp(b_mid - b)).astype(BF)
    att = jnp.where(tri, _dot_nt(q2, k2), 0.0).astype(BF)
    o = o_inter + _dot(att, v)
    ms = jnp.mean(o * o, axis=-1, keepdims=True)
    on = o * lax.rsqrt(ms + EPS) * g_ref[...]
    o_ref[...] = (on * _silu(z_ref[...].astype(F32))).astype(o_ref.dtype)


def _gla_sample(oc, s_buf, proj_a, proj_b, loga, l, s0, bt, ones, g):
    C = SMP_GB * DEC_SEQ
    base = T_P // C
    alias_in = [oc] + ([s_buf] if l > 0 else [])
    n_alias = len(alias_in)
    kern = lambda *refs: _gla_sample_kernel(n_alias, *refs)
    state_spec = pl.BlockSpec((None, SMP_GB, 1, C_DK, C_DV), lambda b, h: (l, b, h, 0, 0))
    return pl.pallas_call(
        kern,
        grid=(DEC_BATCH // SMP_GB, C_HEADS),
        in_specs=[_any_spec()] * n_alias + [
            pl.BlockSpec((C, C_DK), lambda b, h: (base + b, A_CQ // C_DK + h)),
            pl.BlockSpec((C, C_DK), lambda b, h: (base + b, A_CK // C_DK + h)),
            pl.BlockSpec((C, C_DV), lambda b, h: (base + b, A_CV // C_DV + h)),
            pl.BlockSpec((C, C_DV), lambda b, h: (base + b, B_CZ // C_DV + h)),
            pl.BlockSpec((C, C_DK), lambda b, h: (base + b, h)),
            state_spec,
            pl.BlockSpec((C, C), lambda b, h: (0, 0)),
            pl.BlockSpec((C, 128), lambda b, h: (0, 0)),
            _row_spec(l, C_DV),
        ],
        out_specs=[pl.BlockSpec((C, C_DV), lambda b, h: (base + b, h)), state_spec],
        out_shape=[
            jax.ShapeDtypeStruct((T_ALL, BRANCH), BF),
            jax.ShapeDtypeStruct((DEPTH, DEC_BATCH, C_HEADS, C_DK, C_DV), F32),
        ],
        input_output_aliases={i: i for i in range(n_alias)},
        compiler_params=_cparams(2),
    )(*alias_in, proj_a, proj_a, proj_a, proj_b, loga, s0, bt, ones, g)


def _merge_out_kernel(with_next, ba_ref, bb_ref, bc_ref, g0_ref, g1_ref, g2_ref, x_ref, wb_ref, wo_ref, *rest):
    acc = _sigmoid(g0_ref[...].astype(F32)) * _dot(ba_ref[...], wb_ref[0])
    acc = acc + _sigmoid(g1_ref[...].astype(F32)) * _dot(bb_ref[...], wb_ref[1])
    acc = acc + _sigmoid(g2_ref[...].astype(F32)) * _dot(bc_ref[...], wb_ref[2])
    y = x_ref[...] + _dot(acc.astype(BF), wo_ref[...])
    if with_next:
        g_ref, wlr_ref, wg_ref, bg_ref, y_ref, h_ref, loga_ref = rest
        h, loga = _norm_gate(y, g_ref, wlr_ref, wg_ref, bg_ref)
        h_ref[...] = h
        loga_ref[...] = loga
    else:
        (y_ref,) = rest
    y_ref[...] = y


def _merge_out(ba, bb, bc, proj_b, x, l, w_branch, w_out, norm_g, w_lr, w_g, b_gate):
    with_next = l + 1 < DEPTH
    kern = lambda *refs: _merge_out_kernel(with_next, *refs)
    row = lambda w: pl.BlockSpec((TM_OUT, w), lambda i: (i, 0))
    gate = lambda n: pl.BlockSpec((TM_OUT, D_MODEL), lambda i: (i, B_G // D_MODEL + n))
    resident = pl.Buffered(1)
    in_specs = [
        row(BRANCH), row(BRANCH), row(BRANCH), gate(0), gate(1), gate(2), row(D_MODEL),
        pl.BlockSpec((None, N_BRANCH, BRANCH, D_MODEL), lambda i: (l, 0, 0, 0), pipeline_mode=resident),
        pl.BlockSpec((None, D_MODEL, D_MODEL), lambda i: (l, 0, 0), pipeline_mode=resident),
    ]
    args = [ba, bb, bc, proj_b, proj_b, proj_b, x, w_branch, w_out]
    out_specs = [row(D_MODEL)]
    out_shape = [jax.ShapeDtypeStruct((T_ALL, D_MODEL), F32)]
    if with_next:
        in_specs += _norm_specs(l + 1, None)
        args += [norm_g, w_lr, w_g, b_gate]
        out_specs += [row(D_MODEL), row(C_KEY)]
        out_shape += [jax.ShapeDtypeStruct((T_ALL, D_MODEL), BF), jax.ShapeDtypeStruct((T_ALL, C_KEY), F32)]
    return pl.pallas_call(
        kern,
        grid=(T_ALL // TM_OUT,),
        in_specs=in_specs,
        out_specs=out_specs,
        out_shape=out_shape,
        compiler_params=_cparams(1),
    )(*args)


def _rope_tables(pos):
    half = A_HEAD_DIM // 2
    inv = jnp.power(ROPE_THETA, -jnp.arange(half, dtype=F32) * (2.0 / A_HEAD_DIM))
    ang = pos.astype(F32)[:, None] * inv[None, :]
    cos, sin = jnp.cos(ang), jnp.sin(ang)
    cos128 = jnp.concatenate([cos, cos, cos, cos], axis=1)
    sin128 = jnp.concatenate([-sin, sin, -sin, sin], axis=1)
    return cos128, sin128


def _block_tril(n, blk):
    r = np.arange(n)
    m = (r[:, None] >= r[None, :]) & ((r[:, None] // blk) == (r[None, :] // blk))
    return jnp.asarray(m.astype(np.float32), dtype=BF)


def kernel(x_prompt, x_sample, cache_a_k, cache_a_v, state_pool, state_gla, norm_g, w_in, q_norm_g, k_norm_g,
           sinks, w_pool, pool_scale, w_gate_lr, b_gate, gla_norm_g, w_branch, w_out):
    w_lr = jnp.pad(w_in[:, :, A_COLS:A_COLS + C_GATE_RANK],
                   ((0, 0), (0, 0), (0, LANES - C_GATE_RANK))).astype(BF)
    w_g = jnp.pad(w_gate_lr, ((0, 0), (0, LANES - C_GATE_RANK), (0, 0))).astype(BF)
    w_branch_bf = w_branch.astype(BF)
    w_out_bf = w_out.astype(BF)
    w_pool_bf = w_pool.astype(BF)
    qg = jnp.tile(q_norm_g, (1, 256 // A_HEAD_DIM))[:, None, :]
    kg = jnp.tile(k_norm_g, (1, 256 // A_HEAD_DIM))[:, None, :]
    norm_g = norm_g[:, None, :]
    b_gate = b_gate[:, None, :]
    pool_scale = pool_scale[:, None, :]
    gla_norm_g = gla_norm_g[:, None, :]
    cos_p, sin_p = _rope_tables(jnp.arange(SEQ))
    cos_s, sin_s = _rope_tables(PAST_LEN + jnp.arange(DEC_SEQ))
    cos_s = jnp.tile(cos_s, (SMP_BS, 1))
    sin_s = jnp.tile(sin_s, (SMP_BS, 1))
    lane = np.arange(256)
    bd = jnp.asarray((lane[:, None] // A_HEAD_DIM == lane[None, :] // A_HEAD_DIM).astype(np.float32), dtype=BF)
    bt_p = _block_tril(GLA_SC, GLA_C)
    bt_s = _block_tril(SMP_GB * DEC_SEQ, DEC_SEQ)
    ones_c = jnp.ones((GLA_C, 128), BF)
    cache_k = cache_a_k.reshape(DEPTH, DEC_BATCH, WINDOW, KV_W)
    cache_v = cache_a_v.reshape(DEPTH, DEC_BATCH, WINDOW, KV_W)
    pool_prev = jnp.pad(state_pool, ((0, 0), (0, 0), (1, 0), (0, 0)))

    x = jnp.concatenate([x_prompt.reshape(T_P, D_MODEL), x_sample.reshape(T_S, D_MODEL)], axis=0)
    h, loga = _norm(x, 0, norm_g, w_lr, w_g, b_gate)
    pk, pv, pp, sp = [], [], [], []
    nk_buf = nv_buf = sp_buf = ss_buf = None
    for l in range(DEPTH):
        proj_a = _inproj_a(h, w_in, l)
        proj_b = _inproj_b(h, w_in, l)

        oa, knew_p = _attn_prompt(proj_a, l, sinks, cos_p, sin_p, bd, qg, kg)
        oa, nk_buf, nv_buf = _attn_sample(oa, nk_buf, nv_buf, proj_a, l, cache_k, cache_v, sinks,
                                          cos_s, sin_s, bd, qg, kg)
        ob, tail_p = _pool_prompt(proj_a, l, w_pool_bf, pool_scale)
        ob, tail_s = _pool_sample(ob, proj_a, l, pool_prev, w_pool_bf, pool_scale)
        oc, sp_buf = _gla_prompt(sp_buf, proj_a, proj_b, loga, l, bt_p, ones_c, gla_norm_g)
        oc, ss_buf = _gla_sample(oc, ss_buf, proj_a, proj_b, loga, l, state_gla, bt_s, ones_c, gla_norm_g)

        outs = _merge_out(oa, ob, oc, proj_b, x, l, w_branch_bf, w_out_bf, norm_g, w_lr, w_g, b_gate)
        if l + 1 < DEPTH:
            x, h, loga = outs
        else:
            (x,) = outs

        vp = proj_a[:T_P, A_AV:A_AV + KV_W].reshape(BATCH, SEQ, KV_W)[:, SEQ - WINDOW:].astype(F32)
        pk.append(knew_p)
        pv.append(vp)
        pp.append(tail_p[:, 1:])
        sp.append(tail_s[:, 1:])

    kv5 = lambda a, nb: a.reshape(DEPTH, nb, WINDOW, A_KV_HEADS, A_HEAD_DIM)
    yp = x[:T_P].reshape(BATCH, SEQ, D_MODEL)
    ys = x[T_P:].reshape(DEC_BATCH, DEC_SEQ, D_MODEL)
    return (yp, ys, kv5(jnp.stack(pk), BATCH), kv5(jnp.stack(pv), BATCH), jnp.stack(pp), sp_buf,
            kv5(nk_buf, DEC_BATCH), kv5(nv_buf, DEC_BATCH), jnp.stack(sp), ss_buf)
```

```python
import numpy as np
import jax
import jax.numpy as jnp
from jax import lax
from jax.experimental import pallas as pl
from jax.experimental.pallas import tpu as pltpu

F32 = jnp.float32
BF = jnp.bfloat16

D_MODEL = 2048
BATCH = 4
SEQ = 2048
DEPTH = 4
DEC_BATCH = 32
DEC_SEQ = 8
PAST_LEN = 16384
BRANCH = D_MODEL // 2
N_BRANCH = 3
A_HEADS = 16
A_KV_HEADS = 4
A_HEAD_DIM = 64
A_GROUP = A_HEADS // A_KV_HEADS
WINDOW = 128
ROPE_THETA = 10000.0
POOL_WINDOWS = (2, 4, 8, 16)
POOL_GROUP_DIM = 256
POOL_HIST = 15
C_HEADS = 4
C_KEY = 512
C_DK = 128
C_DV = 256
C_GATE_RANK = 16
C_GATE_TAU = 16.0
EPS = 1e-6
KV_W = A_KV_HEADS * A_HEAD_DIM
IN_COLS = 13840

T_P = BATCH * SEQ
T_S = DEC_BATCH * DEC_SEQ
T_ALL = T_P + T_S

A_AQ, A_AK, A_AV, A_AZ = 0, 1024, 1280, 1536
A_BU, A_BZ = 2560, 3584
A_CQ, A_CK, A_CV = 4608, 5120, 5632
A_COLS = 6656
LR_SHIFT = C_GATE_RANK
B_COLS = IN_COLS - A_COLS - LR_SHIFT
B_G, B_CZ = 0, N_BRANCH * D_MODEL

LANES = 128
TM = 768
TN_A = 1664
TN_B = 1024
W_SUB = 512
CAST_ROWS = 128
TM_OUT = 256
N_P_TILES = T_P // TM_OUT
ATT_BLK = WINDOW
POOL_TP = 512
POOL_HALO = 16
GLA_C = 64
GLA_NSUB = 8
GLA_SC = GLA_C * GLA_NSUB
SMP_BS = 4
SMP_GB = 8
VMEM_LIMIT = 56 * 1024 * 1024
NEG_BIG = -1e30


def _cparams(n_axes):
    return pltpu.CompilerParams(
        dimension_semantics=("arbitrary",) * n_axes, vmem_limit_bytes=VMEM_LIMIT)


def _sigmoid(x):
    return 1.0 / (1.0 + jnp.exp(-x))


def _silu(x):
    return x * _sigmoid(x)


def _dot(a, b):
    return jnp.dot(a, b, preferred_element_type=F32)


def _dot_nt(a, b):
    return lax.dot_general(a, b, (((1,), (1,)), ((), ())), preferred_element_type=F32)


def _dot_tn(a, b):
    return lax.dot_general(a, b, (((0,), (0,)), ((), ())), preferred_element_type=F32)


def _split3(x):
    hi = x.astype(BF)
    r1 = x - hi.astype(F32)
    mid = r1.astype(BF)
    lo = (r1 - mid.astype(F32)).astype(BF)
    return hi, mid, lo


def _any_spec():
    return pl.BlockSpec(memory_space=pl.ANY)


def _row_spec(l, width):
    return pl.BlockSpec((None, 1, width), lambda *a: (l, 0, 0))


def _norm_gate(xf, g_ref, wlr_ref, wg_ref, bg_ref):
    ms = jnp.mean(xf * xf, axis=-1, keepdims=True)
    h = (xf * lax.rsqrt(ms + EPS) * g_ref[...]).astype(BF)
    lr = _dot(h, wlr_ref[...])
    z = _dot(lr.astype(BF), wg_ref[...]) + bg_ref[...]
    log_sig = jnp.minimum(z, 0.0) - jnp.log1p(jnp.exp(-jnp.abs(z)))
    return h, log_sig * (1.0 / C_GATE_TAU)


def _norm_specs(l, imap):
    return [
        _row_spec(l, D_MODEL),
        pl.BlockSpec((None, D_MODEL, LANES), lambda *a: (l, 0, 0)),
        pl.BlockSpec((None, LANES, C_KEY), lambda *a: (l, 0, 0)),
        _row_spec(l, C_KEY),
    ]


def _split_rows_specs(width):
    return [pl.BlockSpec((TM_OUT, width), lambda i: (jnp.minimum(i, N_P_TILES - 1), 0)),
            pl.BlockSpec((TM_OUT, width), lambda i: (0, 0))]


def _pick_rows(xp_ref, xs_ref):
    return jnp.where(pl.program_id(0) < N_P_TILES, xp_ref[...], xs_ref[...])


def _norm_kernel(xp_ref, xs_ref, g_ref, wlr_ref, wg_ref, bg_ref, h_ref, loga_ref):
    h, loga = _norm_gate(_pick_rows(xp_ref, xs_ref), g_ref, wlr_ref, wg_ref, bg_ref)
    h_ref[...] = h
    loga_ref[...] = loga


def _norm(xp, xs, l, norm_g, w_lr, w_g, b_gate):
    return pl.pallas_call(
        _norm_kernel,
        grid=(T_ALL // TM_OUT,),
        in_specs=_split_rows_specs(D_MODEL) + _norm_specs(l, None),
        out_specs=[pl.BlockSpec((TM_OUT, D_MODEL), lambda i: (i, 0)),
                   pl.BlockSpec((TM_OUT, C_KEY), lambda i: (i, 0))],
        out_shape=[jax.ShapeDtypeStruct((T_ALL, D_MODEL), BF),
                   jax.ShapeDtypeStruct((T_ALL, C_KEY), F32)],
        compiler_params=_cparams(1),
    )(xp, xs, norm_g, w_lr, w_g, b_gate)


def _inproj_a_kernel(h_ref, w_ref, o_ref, w_scr):
    @pl.when(pl.program_id(1) == 0)
    def _():
        for r in range(0, TN_A, CAST_ROWS):
            w_scr[r:r + CAST_ROWS, :] = w_ref[r:r + CAST_ROWS, :].astype(BF)

    o_ref[...] = _dot_nt(h_ref[...], w_scr[...]).astype(o_ref.dtype)


def _inproj_a(h, w_t, l):
    return pl.pallas_call(
        _inproj_a_kernel,
        grid=(A_COLS // TN_A, T_ALL // TM),
        in_specs=[
            pl.BlockSpec((TM, D_MODEL), lambda j, i: (i, 0)),
            pl.BlockSpec((None, TN_A, D_MODEL), lambda j, i: (l, j, 0)),
        ],
        out_specs=pl.BlockSpec((TM, TN_A), lambda j, i: (i, j)),
        out_shape=jax.ShapeDtypeStruct((T_ALL, A_COLS), BF),
        scratch_shapes=[pltpu.VMEM((TN_A, D_MODEL), BF)],
        compiler_params=_cparams(2),
    )(h, w_t)


def _copy_cast_rows(dst, dst0, src, src0, n):
    for r in range(0, n, CAST_ROWS):
        m = min(CAST_ROWS, n - r)
        dst[dst0 + r:dst0 + r + m, :] = src[src0 + r:src0 + r + m, :].astype(BF)


def _inproj_b_kernel(h_ref, w0_ref, w1_ref, w2_ref, o_ref, w_scr):
    @pl.when(pl.program_id(1) == 0)
    def _():
        _copy_cast_rows(w_scr, 0, w0_ref, LR_SHIFT, W_SUB - LR_SHIFT)
        _copy_cast_rows(w_scr, W_SUB - LR_SHIFT, w1_ref, 0, W_SUB)
        _copy_cast_rows(w_scr, 2 * W_SUB - LR_SHIFT, w2_ref, 0, LR_SHIFT)

    o_ref[...] = _dot_nt(h_ref[...], w_scr[...]).astype(o_ref.dtype)


def _inproj_b(h, w_t, l):
    assert TN_B == 2 * W_SUB
    nt = B_COLS // TN_B
    sub0 = A_COLS // W_SUB
    tail0 = (A_COLS + TN_B) // LR_SHIFT
    return pl.pallas_call(
        _inproj_b_kernel,
        grid=(nt, T_ALL // TM),
        in_specs=[
            pl.BlockSpec((TM, D_MODEL), lambda j, i: (i, 0)),
            pl.BlockSpec((None, W_SUB, D_MODEL), lambda j, i: (l, sub0 + 2 * j, 0)),
            pl.BlockSpec((None, W_SUB, D_MODEL), lambda j, i: (l, sub0 + 2 * j + 1, 0)),
            pl.BlockSpec((None, LR_SHIFT, D_MODEL), lambda j, i: (l, tail0 + (TN_B // LR_SHIFT) * j, 0)),
        ],
        out_specs=pl.BlockSpec((TM, TN_B), lambda j, i: (i, (j + nt - 1) % nt)),
        out_shape=jax.ShapeDtypeStruct((T_ALL, B_COLS), BF),
        scratch_shapes=[pltpu.VMEM((TN_B, D_MODEL), BF)],
        compiler_params=_cparams(2),
    )(h, w_t, w_t, w_t)


def _head_norm(x, bd, g_row):
    outs = []
    for c in range(x.shape[1] // 256):
        xc = x[:, c * 256:(c + 1) * 256]
        ss = _dot((xc * xc).astype(BF), bd)
        outs.append(xc * lax.rsqrt(ss * (1.0 / A_HEAD_DIM) + EPS) * g_row)
    return outs


def _rope128(xc, cos, sin, first_half):
    swapped = jnp.where(first_half, pltpu.roll(xc, 96, 1), pltpu.roll(xc, 32, 1))
    return xc * cos + swapped * sin


def _norm_rope(x, bd, g_row, cos, sin, first_half):
    chunks = []
    for blk in _head_norm(x, bd, g_row):
        for c in range(2):
            chunks.append(_rope128(blk[:, c * 128:(c + 1) * 128], cos, sin, first_half))
    return chunks


def _dup_half(chunk, rolled, lo_half, use_low):
    return jnp.where(lo_half, chunk, rolled) if use_low else jnp.where(lo_half, rolled, chunk)


def _attn_prompt_kernel(l, sinks_ref, q_ref, z0_ref, z1_ref, k_ref, v_ref, cos_ref, sin_ref, bd_ref, qg_ref,
                        kg_ref, o_ref, knew_ref, kprev, vprev_e, vprev_o):
    n = pl.program_id(1)
    R = ATT_BLK
    lane = lax.broadcasted_iota(jnp.int32, (R, 128), 1)
    lo_half = lane < 64
    hi_half = jnp.logical_not(lo_half)
    first_half = (lane & 32) == 0
    cos = cos_ref[...]
    sin = sin_ref[...]
    bd = bd_ref[...]

    @pl.when(n == 0)
    def _():
        kprev[...] = jnp.zeros_like(kprev)
        vprev_e[...] = jnp.zeros_like(vprev_e)
        vprev_o[...] = jnp.zeros_like(vprev_o)

    kch = _norm_rope(k_ref[...].astype(F32), bd, kg_ref[...], cos, sin, first_half)
    knew_ref[0] = jnp.concatenate(kch, axis=1)
    vraw = v_ref[...].astype(F32)
    vch = [vraw[:, :128], vraw[:, 128:]]
    kcur, vcur_e, vcur_o = [], [], []
    for h in range(A_KV_HEADS):
        c, low = h // 2, (h % 2 == 0)
        kcur.append(_dup_half(kch[c], pltpu.roll(kch[c], 64, 1), lo_half, low).astype(BF))
        v_lo = vch[c] if low else pltpu.roll(vch[c], 64, 1)
        v_hi = pltpu.roll(vch[c], 64, 1) if low else vch[c]
        vcur_e.append(jnp.where(lo_half, v_lo, 1.0).astype(BF))
        vcur_o.append(jnp.where(lo_half, 1.0, v_hi).astype(BF))

    qch = _norm_rope(q_ref[...].astype(F32), bd, qg_ref[...], cos, sin, first_half)

    G = A_GROUP * R
    srow = lax.broadcasted_iota(jnp.int32, (G, 128), 0)
    scol = lax.broadcasted_iota(jnp.int32, (G, 128), 1)
    qi = srow & (R - 1)
    bias_prev = jnp.where(jnp.logical_and(scol >= qi, n > 0), 0.0, NEG_BIG)
    bias_cur = jnp.where(scol <= qi, 0.0, NEG_BIG)

    for h in range(A_KV_HEADS):
        qa = qch[2 * h] * (A_HEAD_DIM ** -0.5)
        qb = qch[2 * h + 1] * (A_HEAD_DIM ** -0.5)
        q_stack = jnp.concatenate([jnp.where(lo_half, qa, 0.0), jnp.where(lo_half, qb, 0.0),
                                   jnp.where(hi_half, qa, 0.0), jnp.where(hi_half, qb, 0.0)],
                                  axis=0).astype(BF)
        head_of_blk = (4 * h, 4 * h + 2, 4 * h + 1, 4 * h + 3)
        s_p = _dot_nt(q_stack, kprev[h]) + bias_prev
        s_c = _dot_nt(q_stack, kcur[h]) + bias_cur
        row_max = jnp.max(jnp.maximum(s_p, s_c), axis=-1, keepdims=True)
        m_blk = [jnp.maximum(row_max[bi * R:(bi + 1) * R], sinks_ref[l, hd]) for bi, hd in enumerate(head_of_blk)]
        e_sink = [jnp.exp(sinks_ref[l, hd] - m_blk[bi]) for bi, hd in enumerate(head_of_blk)]
        m = jnp.concatenate(m_blk, axis=0)
        p_p = jnp.exp(s_p - m).astype(BF)
        p_c = jnp.exp(s_c - m).astype(BF)
        half = G // 2
        pv_e = _dot(p_p[:half], vprev_e[h]) + _dot(p_c[:half], vcur_e[h])
        pv_o = _dot(p_p[half:], vprev_o[h]) + _dot(p_c[half:], vcur_o[h])
        for cc in range(2):
            rows_e = slice(cc * R, (cc + 1) * R)
            ev, od = pv_e[rows_e], pv_o[rows_e]
            num = jnp.where(lo_half, ev, od)
            den = pltpu.roll(jnp.where(lo_half, od, ev), 64, 1)
            den = den + jnp.where(lo_half, e_sink[cc], e_sink[2 + cc])
            c = 2 * h + cc
            z_ref = z0_ref if c < 4 else z1_ref
            zc = z_ref[:, (c % 4) * 128:(c % 4 + 1) * 128].astype(F32)
            o_ref[:, c * 128:(c + 1) * 128] = (num / den * _silu(zc)).astype(o_ref.dtype)

    for h in range(A_KV_HEADS):
        kprev[h] = kcur[h]
        vprev_e[h] = vcur_e[h]
        vprev_o[h] = vcur_o[h]


def _attn_prompt(proj_a, l, sinks, cos, sin, bd, qg, kg):
    nb = SEQ // ATT_BLK
    rb = lambda b, n: b * nb + n
    kern = lambda *refs: _attn_prompt_kernel(l, *refs)
    return pl.pallas_call(
        kern,
        grid=(BATCH, nb),
        in_specs=[
            pl.BlockSpec(memory_space=pltpu.SMEM),
            pl.BlockSpec((ATT_BLK, BRANCH), lambda b, n: (rb(b, n), A_AQ // BRANCH)),
            pl.BlockSpec((ATT_BLK, 512), lambda b, n: (rb(b, n), A_AZ // 512)),
            pl.BlockSpec((ATT_BLK, 512), lambda b, n: (rb(b, n), A_AZ // 512 + 1)),
            pl.BlockSpec((ATT_BLK, KV_W), lambda b, n: (rb(b, n), A_AK // KV_W)),
            pl.BlockSpec((ATT_BLK, KV_W), lambda b, n: (rb(b, n), A_AV // KV_W)),
            pl.BlockSpec((ATT_BLK, 128), lambda b, n: (n, 0)),
            pl.BlockSpec((ATT_BLK, 128), lambda b, n: (n, 0)),
            pl.BlockSpec((256, 256), lambda b, n: (0, 0)),
            _row_spec(l, 256),
            _row_spec(l, 256),
        ],
        out_specs=[
            pl.BlockSpec((ATT_BLK, BRANCH), lambda b, n: (rb(b, n), 0)),
            pl.BlockSpec((1, ATT_BLK, KV_W), lambda b, n: (b, 0, 0)),
        ],
        out_shape=[
            jax.ShapeDtypeStruct((T_ALL, BRANCH), BF),
            jax.ShapeDtypeStruct((BATCH, WINDOW, KV_W), F32),
        ],
        scratch_shapes=[pltpu.VMEM((A_KV_HEADS, ATT_BLK, 128), BF),
                        pltpu.VMEM((A_KV_HEADS, ATT_BLK, 128), BF),
                        pltpu.VMEM((A_KV_HEADS, ATT_BLK, 128), BF)],
        compiler_params=_cparams(2),
    )(sinks, proj_a, proj_a, proj_a, proj_a, proj_a, cos, sin, bd, qg, kg)


def _attn_sample_kernel(l, n_alias, *refs):
    refs = refs[n_alias:]
    (sinks_ref, q_ref, z0_ref, z1_ref, k_ref, v_ref, ck_ref, cv_ref, cos_ref, sin_ref,
     bd_ref, qg_ref, kg_ref, o_ref, nk_ref, nv_ref, kall, vall) = refs
    L = DEC_SEQ
    R = SMP_BS * L
    lane = lax.broadcasted_iota(jnp.int32, (R, 128), 1)
    first_half = (lane & 32) == 0
    lane8 = lax.broadcasted_iota(jnp.int32, (L, 128), 1)
    lo8 = lane8 < 64
    lane_c = lax.broadcasted_iota(jnp.int32, (WINDOW, 128), 1)
    lo_c = lane_c < 64
    cos = cos_ref[...]
    sin = sin_ref[...]
    bd = bd_ref[...]

    @pl.when(pl.program_id(0) == 0)
    def _():
        kall[...] = jnp.zeros_like(kall)
        vall[...] = jnp.zeros_like(vall)

    kch = _norm_rope(k_ref[...].astype(F32), bd, kg_ref[...], cos, sin, first_half)
    vraw = v_ref[...].astype(F32)
    vch = [vraw[:, :128], vraw[:, 128:]]
    qch = _norm_rope(q_ref[...].astype(F32), bd, qg_ref[...], cos, sin, first_half)

    rows = A_GROUP * L
    srow = lax.broadcasted_iota(jnp.int32, (rows, 2 * WINDOW), 0)
    scol = lax.broadcasted_iota(jnp.int32, (rows, 2 * WINDOW), 1)
    t = srow & (L - 1)
    mask = jnp.logical_or(jnp.logical_and(scol < WINDOW, scol >= t),
                          jnp.logical_and(scol >= WINDOW, (scol - WINDOW) <= t))
    hrow = lax.broadcasted_iota(jnp.int32, (rows, 1), 0) >> (L.bit_length() - 1)
    z_all = jnp.concatenate([z0_ref[...], z1_ref[...]], axis=1).astype(F32)
    bias = jnp.where(mask, 0.0, NEG_BIG)
    pairs = [(bi, h) for bi in range(SMP_BS) for h in range(A_KV_HEADS)]
    slot = lambda bi, h: bi * A_KV_HEADS + h

    for bi in range(SMP_BS):
        r0 = bi * L
        knew = [kc[r0:r0 + L] for kc in kch]
        vnew = [vc[r0:r0 + L] for vc in vch]
        ck = ck_ref[bi]
        cv = cv_ref[bi]
        nk_ref[bi, 0:WINDOW - L] = ck[L:WINDOW]
        nv_ref[bi, 0:WINDOW - L] = cv[L:WINDOW]
        nk_ref[bi, WINDOW - L:WINDOW] = jnp.concatenate(knew, axis=1)
        nv_ref[bi, WINDOW - L:WINDOW] = jnp.concatenate(vnew, axis=1)
        for h in range(A_KV_HEADS):
            c, low = h // 2, (h % 2 == 0)
            ckc = ck[:, c * 128:(c + 1) * 128]
            cvc = cv[:, c * 128:(c + 1) * 128]
            kall[slot(bi, h), 0:WINDOW] = _dup_half(ckc, pltpu.roll(ckc, 64, 1), lo_c, low)
            vall[slot(bi, h), 0:WINDOW] = _dup_half(cvc, pltpu.roll(cvc, 64, 1), lo_c, low)
            kall[slot(bi, h), WINDOW:WINDOW + L] = _dup_half(knew[c], pltpu.roll(knew[c], 64, 1), lo8, low)
            vall[slot(bi, h), WINDOW:WINDOW + L] = _dup_half(vnew[c], pltpu.roll(vnew[c], 64, 1), lo8, low)

    scores = []
    for bi, h in pairs:
        r0 = bi * L
        qs = []
        for gi in range(A_GROUP):
            i = h * A_GROUP + gi
            sel = lo8 if i % 2 == 0 else jnp.logical_not(lo8)
            qs.append(jnp.where(sel, qch[i // 2][r0:r0 + L] * (A_HEAD_DIM ** -0.5), 0.0))
        qm = jnp.concatenate(qs, axis=0).astype(BF)
        scores.append(_dot_nt(qm, kall[slot(bi, h)].astype(BF)) + bias)

    probs, dens = [], []
    for (bi, h), s in zip(pairs, scores):
        sink = jnp.zeros((rows, 1), F32)
        for gi in range(A_GROUP):
            sink = jnp.where(hrow == gi, sinks_ref[l, h * A_GROUP + gi], sink)
        m = jnp.maximum(jnp.max(s, axis=-1, keepdims=True), sink)
        p = jnp.exp(s - m)
        dens.append(jnp.sum(p, axis=-1, keepdims=True) + jnp.exp(sink - m))
        probs.append(p.astype(BF))

    pvs = [_dot(p, vall[slot(bi, h)].astype(BF)) / den for (bi, h), p, den in zip(pairs, probs, dens)]
    o_rows = []
    for bi in range(SMP_BS):
        o_chunks = []
        for h in range(A_KV_HEADS):
            pv = pvs[slot(bi, h)]
            for cc in range(2):
                o_chunks.append(jnp.where(lo8, pv[(2 * cc) * L:(2 * cc + 1) * L],
                                          pv[(2 * cc + 1) * L:(2 * cc + 2) * L]))
        o = jnp.concatenate(o_chunks, axis=1)
        o_rows.append(o * _silu(z_all[bi * L:(bi + 1) * L]))
    o_ref[...] = jnp.concatenate(o_rows, axis=0).astype(o_ref.dtype)


def _attn_sample(oa, nk_buf, nv_buf, proj_a, l, cache_k, cache_v, sinks, cos, sin, bd, qg, kg):
    R = SMP_BS * DEC_SEQ
    base = T_P // R
    alias_in = [oa] + ([nk_buf, nv_buf] if l > 0 else [])
    n_alias = len(alias_in)
    kern = lambda *refs: _attn_sample_kernel(l, n_alias, *refs)
    cache_spec = pl.BlockSpec((None, SMP_BS, WINDOW, KV_W), lambda b: (l, b, 0, 0))
    return pl.pallas_call(
        kern,
        grid=(DEC_BATCH // SMP_BS,),
        in_specs=[_any_spec()] * n_alias + [
            pl.BlockSpec(memory_space=pltpu.SMEM),
            pl.BlockSpec((R, BRANCH), lambda b: (base + b, A_AQ // BRANCH)),
            pl.BlockSpec((R, 512), lambda b: (base + b, A_AZ // 512)),
            pl.BlockSpec((R, 512), lambda b: (base + b, A_AZ // 512 + 1)),
            pl.BlockSpec((R, KV_W), lambda b: (base + b, A_AK // KV_W)),
            pl.BlockSpec((R, KV_W), lambda b: (base + b, A_AV // KV_W)),
            cache_spec, cache_spec,
            pl.BlockSpec((R, 128), lambda b: (0, 0)),
            pl.BlockSpec((R, 128), lambda b: (0, 0)),
            pl.BlockSpec((256, 256), lambda b: (0, 0)),
            _row_spec(l, 256),
            _row_spec(l, 256),
        ],
        out_specs=[pl.BlockSpec((R, BRANCH), lambda b: (base + b, 0)), cache_spec, cache_spec],
        out_shape=[
            jax.ShapeDtypeStruct((T_ALL, BRANCH), BF),
            jax.ShapeDtypeStruct((DEPTH, DEC_BATCH, WINDOW, KV_W), F32),
            jax.ShapeDtypeStruct((DEPTH, DEC_BATCH, WINDOW, KV_W), F32),
        ],
        scratch_shapes=[pltpu.VMEM((SMP_BS * A_KV_HEADS, 2 * WINDOW, 128), F32),
                        pltpu.VMEM((SMP_BS * A_KV_HEADS, 2 * WINDOW, 128), F32)],
        input_output_aliases={i: i for i in range(n_alias)},
        compiler_params=_cparams(1),
    )(*alias_in, sinks, proj_a, proj_a, proj_a, proj_a, proj_a, cache_k, cache_v, cos, sin, bd, qg, kg)


def _pool_compute(ext, u_rows, start_pos, wp_ref, scale_ref, z, n_rows):
    pos = start_pos + lax.broadcasted_iota(jnp.int32, (n_rows, 1), 0)
    outs = []
    for g, w in enumerate(POOL_WINDOWS):
        sl = slice(g * POOL_GROUP_DIM, (g + 1) * POOL_GROUP_DIM)
        acc = ext[:, sl]
        span = 1
        while span < w:
            acc = acc + pltpu.roll(acc, span, 0)
            span *= 2
        win_sum = acc[POOL_HALO:]
        count = jnp.minimum(w, pos + 1).astype(F32)
        d = (win_sum / count - u_rows[:, sl]).astype(BF)
        outs.append(_dot(d, wp_ref[g]))
    y = jnp.concatenate(outs, axis=1) * scale_ref[...]
    return y * _silu(z)


def _pool_prompt_kernel(u0_ref, u1_ref, z0_ref, z1_ref, wp_ref, scale_ref, o_ref, tail_ref, ext):
    t = pl.program_id(1)

    @pl.when(t == 0)
    def _():
        ext[0:POOL_HALO] = jnp.zeros((POOL_HALO, BRANCH), F32)

    u = jnp.concatenate([u0_ref[...], u1_ref[...]], axis=1).astype(F32)
    z = jnp.concatenate([z0_ref[...], z1_ref[...]], axis=1).astype(F32)
    ext[POOL_HALO:] = u
    out = _pool_compute(ext[...], u, t * POOL_TP, wp_ref, scale_ref, z, POOL_TP)
    o_ref[...] = out.astype(o_ref.dtype)
    tail = ext[POOL_TP:POOL_TP + POOL_HALO]
    tail_ref[0] = tail
    ext[0:POOL_HALO] = tail


def _pool_w_specs(l):
    ng = len(POOL_WINDOWS)
    return [pl.BlockSpec((None, ng, POOL_GROUP_DIM, POOL_GROUP_DIM), lambda *a: (l, 0, 0, 0)),
            _row_spec(l, BRANCH)]


def _pool_prompt(proj_a, l, w_pool, scale):
    nt = SEQ // POOL_TP
    half = lambda off, k: pl.BlockSpec((POOL_TP, 512), lambda b, t: (b * nt + t, off // 512 + k))
    return pl.pallas_call(
        _pool_prompt_kernel,
        grid=(BATCH, nt),
        in_specs=[half(A_BU, 0), half(A_BU, 1), half(A_BZ, 0), half(A_BZ, 1)] + _pool_w_specs(l),
        out_specs=[
            pl.BlockSpec((POOL_TP, BRANCH), lambda b, t: (b * nt + t, 0)),
            pl.BlockSpec((1, POOL_HALO, BRANCH), lambda b, t: (b, 0, 0)),
        ],
        out_shape=[
            jax.ShapeDtypeStruct((T_ALL, BRANCH), BF),
            jax.ShapeDtypeStruct((BATCH, POOL_HALO, BRANCH), F32),
        ],
        scratch_shapes=[pltpu.VMEM((POOL_HALO + POOL_TP, BRANCH), F32)],
        compiler_params=_cparams(2),
    )(proj_a, proj_a, proj_a, proj_a, w_pool, scale)


def _pool_sample_kernel(ob_in_ref, u0_ref, u1_ref, z0_ref, z1_ref, prev_ref, wp_ref, scale_ref, o_ref, tail_ref):
    del ob_in_ref
    L = DEC_SEQ
    u_all = jnp.concatenate([u0_ref[...], u1_ref[...]], axis=1).astype(F32)
    z_all = jnp.concatenate([z0_ref[...], z1_ref[...]], axis=1).astype(F32)
    outs = []
    for bi in range(SMP_BS):
        u = u_all[bi * L:(bi + 1) * L]
        ext = jnp.concatenate([prev_ref[bi], u], axis=0)
        outs.append(_pool_compute(ext, u, PAST_LEN, wp_ref, scale_ref, z_all[bi * L:(bi + 1) * L], L))
        tail_ref[bi] = ext[L:L + POOL_HALO]
    o_ref[...] = jnp.concatenate(outs, axis=0).astype(o_ref.dtype)


def _pool_sample(ob, proj_a, l, prev16, w_pool, scale):
    R = SMP_BS * DEC_SEQ
    base = T_P // R
    half = lambda off, k: pl.BlockSpec((R, 512), lambda b: (base + b, off // 512 + k))
    return pl.pallas_call(
        _pool_sample_kernel,
        grid=(DEC_BATCH // SMP_BS,),
        in_specs=[_any_spec(), half(A_BU, 0), half(A_BU, 1), half(A_BZ, 0), half(A_BZ, 1),
                  pl.BlockSpec((None, SMP_BS, POOL_HALO, BRANCH), lambda b: (l, b, 0, 0))] + _pool_w_specs(l),
        out_specs=[
            pl.BlockSpec((R, BRANCH), lambda b: (base + b, 0)),
            pl.BlockSpec((SMP_BS, POOL_HALO, BRANCH), lambda b: (b, 0, 0)),
        ],
        out_shape=[
            jax.ShapeDtypeStruct((T_ALL, BRANCH), BF),
            jax.ShapeDtypeStruct((DEC_BATCH, POOL_HALO, BRANCH), F32),
        ],
        input_output_aliases={0: 0},
        compiler_params=_cparams(1),
    )(ob, proj_a, proj_a, proj_a, proj_a, prev16, w_pool, scale)


def _chunk_cumsum(x, chunk):
    pos = lax.broadcasted_iota(jnp.int32, x.shape, 0) & (chunk - 1)
    span = 1
    while span < chunk:
        x = x + jnp.where(pos >= span, pltpu.roll(x, span, 0), 0.0)
        span *= 2
    return x


def _chunk_row(x, chunk, r):
    n = x.shape[0] // chunk
    return jnp.concatenate(
        [jnp.broadcast_to(x[c * chunk + r:c * chunk + r + 1], (chunk, x.shape[1])) for c in range(n)], axis=0)


def _gla_prompt_kernel(n_alias, *refs):
    refs = refs[n_alias:]
    q_ref, k_ref, v_ref, z_ref, la_ref, g_ref, o_ref, snew_ref, st_scr = refs
    step = pl.program_id(2)

    @pl.when(step == 0)
    def _():
        st_scr[...] = jnp.zeros_like(st_scr)

    C = GLA_C
    chunks = [slice(c * C, (c + 1) * C) for c in range(GLA_NSUB)]
    b = _chunk_cumsum(la_ref[...], C)
    b_last = _chunk_row(b, C, C - 1)
    b_mid = _chunk_row(b, C, C // 2 - 1)
    q = q_ref[...].astype(F32) * (C_DK ** -0.5)
    k = k_ref[...].astype(F32)
    v = v_ref[...].astype(BF)
    kd = (k * jnp.exp(b_last - b)).astype(BF)
    qe = (q * jnp.exp(b)).astype(BF)
    q2 = (q * jnp.exp(b - b_mid)).astype(BF)
    k2 = (k * jnp.exp(b_mid - b)).astype(BF)

    row = lax.broadcasted_iota(jnp.int32, (GLA_SC, GLA_SC), 0)
    col = lax.broadcasted_iota(jnp.int32, (GLA_SC, GLA_SC), 1)
    visible = (row - col).astype(jnp.uint32) <= (row & (C - 1)).astype(jnp.uint32)
    att = jnp.where(visible, _dot_nt(q2, k2), 0.0).astype(BF)
    o = _dot(att, v)

    ut = [_dot_tn(v[sl], kd[sl]) for sl in chunks]
    st = st_scr[...]
    st_before = []
    for c, sl in enumerate(chunks):
        st_before.append(st.astype(BF))
        st = st * jnp.exp(b[(c + 1) * C - 1:(c + 1) * C]) + ut[c]
    st_scr[...] = st

    @pl.when(step == pl.num_programs(2) - 1)
    def _():
        snew_ref[0, 0] = st.T

    o = o + jnp.concatenate([_dot_nt(qe[sl], st_before[c]) for c, sl in enumerate(chunks)], axis=0)
    ms = jnp.mean(o * o, axis=-1, keepdims=True)
    on = o * lax.rsqrt(ms + EPS) * g_ref[...]
    o_ref[...] = (on * _silu(z_ref[...].astype(F32))).astype(o_ref.dtype)


def _gla_prompt(s_buf, proj_a, proj_b, loga, l, g):
    ns = SEQ // GLA_SC
    rb = lambda b, h, s: b * ns + s
    alias_in = [s_buf] if l > 0 else []
    n_alias = len(alias_in)
    kern = lambda *refs: _gla_prompt_kernel(n_alias, *refs)
    return pl.pallas_call(
        kern,
        grid=(BATCH, C_HEADS, ns),
        in_specs=[_any_spec()] * n_alias + [
            pl.BlockSpec((GLA_SC, C_DK), lambda b, h, s: (rb(b, h, s), A_CQ // C_DK + h)),
            pl.BlockSpec((GLA_SC, C_DK), lambda b, h, s: (rb(b, h, s), A_CK // C_DK + h)),
            pl.BlockSpec((GLA_SC, C_DV), lambda b, h, s: (rb(b, h, s), A_CV // C_DV + h)),
            pl.BlockSpec((GLA_SC, C_DV), lambda b, h, s: (rb(b, h, s), B_CZ // C_DV + h)),
            pl.BlockSpec((GLA_SC, C_DK), lambda b, h, s: (rb(b, h, s), h)),
            _row_spec(l, C_DV),
        ],
        out_specs=[
            pl.BlockSpec((GLA_SC, C_DV), lambda b, h, s: (rb(b, h, s), h)),
            pl.BlockSpec((None, 1, 1, C_DK, C_DV), lambda b, h, s: (l, b, h, 0, 0)),
        ],
        out_shape=[
            jax.ShapeDtypeStruct((T_ALL, BRANCH), BF),
            jax.ShapeDtypeStruct((DEPTH, BATCH, C_HEADS, C_DK, C_DV), F32),
        ],
        scratch_shapes=[pltpu.VMEM((C_DV, C_DK), F32)],
        input_output_aliases={0: 1} if l > 0 else {},
        compiler_params=_cparams(3),
    )(*alias_in, proj_a, proj_a, proj_a, proj_b, loga, g)


def _gla_sample_kernel(n_alias, *refs):
    refs = refs[n_alias:]
    q_ref, k_ref, v_ref, z_ref, la_ref, s0_ref, bt_ref, ones_ref, g_ref, o_ref, snew_ref = refs
    C = SMP_GB * DEC_SEQ
    L = DEC_SEQ
    la = la_ref[...]
    parts = _split3(la)
    bt = bt_ref[...]
    b = _dot(bt, parts[0]) + _dot(bt, parts[1]) + _dot(bt, parts[2])
    q = q_ref[...].astype(F32) * (C_DK ** -0.5)
    k = k_ref[...].astype(F32)
    v = v_ref[...].astype(BF)
    ones = ones_ref[...]
    row = lax.broadcasted_iota(jnp.int32, (C, C), 0)
    col = lax.broadcasted_iota(jnp.int32, (C, C), 1)
    shift = L.bit_length() - 1
    tri = jnp.logical_and(row >= col, (row >> shift) == (col >> shift))
    rowk = lax.broadcasted_iota(jnp.int32, (C, C_DK), 0) >> shift

    b_last = jnp.zeros_like(b)
    for gi in range(SMP_GB):
        b_last = jnp.where(rowk == gi, b[(gi + 1) * L - 1:(gi + 1) * L], b_last)
    kd = k * jnp.exp(b_last - b)
    qe = q * jnp.exp(b)

    o_inter = jnp.zeros((C, C_DV), F32)
    for gi in range(SMP_GB):
        own = rowk == gi
        s0 = s0_ref[gi, 0]
        o_inter = o_inter + _dot(jnp.where(own, qe, 0.0).astype(BF), s0.astype(BF))
        u = _dot_tn(jnp.where(own, kd, 0.0).astype(BF), v)
        own_parts = _split3(jnp.where(own, la, 0.0))
        dsum = _dot_tn(own_parts[0], ones) + _dot_tn(own_parts[1], ones) + _dot_tn(own_parts[2], ones)
        d = jnp.exp(dsum)
        snew_ref[gi, 0] = jnp.concatenate([d, d], axis=1) * s0 + u

    b_mid = jnp.zeros_like(b)
    for gi in range(SMP_GB):
        b_mid = jnp.where(rowk == gi, b[gi * L + L // 2 - 1:gi * L + L // 2], b_mid)
    q2 = (q * jnp.exp(b - b_mid)).astype(BF)
    k2 = (k * jnp.exp(b_mid - b)).astype(BF)
    att = jnp.where(tri, _dot_nt(q2, k2), 0.0).astype(BF)
    o = o_inter + _dot(att, v)
    ms = jnp.mean(o * o, axis=-1, keepdims=True)
    on = o * lax.rsqrt(ms + EPS) * g_ref[...]
    o_ref[...] = (on * _silu(z_ref[...].astype(F32))).astype(o_ref.dtype)


def _gla_sample(oc, s_buf, proj_a, proj_b, loga, l, s0, bt, ones, g):
    C = SMP_GB * DEC_SEQ
    base = T_P // C
    alias_in = [oc] + ([s_buf] if l > 0 else [])
    n_alias = len(alias_in)
    kern = lambda *refs: _gla_sample_kernel(n_alias, *refs)
    state_spec = pl.BlockSpec((None, SMP_GB, 1, C_DK, C_DV), lambda b, h: (l, b, h, 0, 0))
    return pl.pallas_call(
        kern,
        grid=(DEC_BATCH // SMP_GB, C_HEADS),
        in_specs=[_any_spec()] * n_alias + [
            pl.BlockSpec((C, C_DK), lambda b, h: (base + b, A_CQ // C_DK + h)),
            pl.BlockSpec((C, C_DK), lambda b, h: (base + b, A_CK // C_DK + h)),
            pl.BlockSpec((C, C_DV), lambda b, h: (base + b, A_CV // C_DV + h)),
            pl.BlockSpec((C, C_DV), lambda b, h: (base + b, B_CZ // C_DV + h)),
            pl.BlockSpec((C, C_DK), lambda b, h: (base + b, h)),
            state_spec,
            pl.BlockSpec((C, C), lambda b, h: (0, 0)),
            pl.BlockSpec((C, 128), lambda b, h: (0, 0)),
            _row_spec(l, C_DV),
        ],
        out_specs=[pl.BlockSpec((C, C_DV), lambda b, h: (base + b, h)), state_spec],
        out_shape=[
            jax.ShapeDtypeStruct((T_ALL, BRANCH), BF),
            jax.ShapeDtypeStruct((DEPTH, DEC_BATCH, C_HEADS, C_DK, C_DV), F32),
        ],
        input_output_aliases={i: i for i in range(n_alias)},
        compiler_params=_cparams(2),
    )(*alias_in, proj_a, proj_a, proj_a, proj_b, loga, s0, bt, ones, g)


def _merge_out_kernel(first, last, *refs):
    ba_ref, bb_ref, bc_ref, g0_ref, g1_ref, g2_ref = refs[:6]
    refs = refs[6:]
    if first:
        x = _pick_rows(refs[0], refs[1])
        refs = refs[2:]
    else:
        x = refs[0][...]
        refs = refs[1:]
    wb_ref, wo_ref = refs[:2]
    refs = refs[2:]
    acc = _sigmoid(g0_ref[...].astype(F32)) * _dot(ba_ref[...], wb_ref[0])
    acc = acc + _sigmoid(g1_ref[...].astype(F32)) * _dot(bb_ref[...], wb_ref[1])
    acc = acc + _sigmoid(g2_ref[...].astype(F32)) * _dot(bc_ref[...], wb_ref[2])
    y = x + _dot(acc.astype(BF), wo_ref[...])
    if last:
        yp_ref, ys_ref = refs
        i = pl.program_id(0)

        @pl.when(i < N_P_TILES)
        def _():
            yp_ref[...] = y

        @pl.when(i == N_P_TILES)
        def _():
            ys_ref[...] = y
    else:
        g_ref, wlr_ref, wg_ref, bg_ref, y_ref, h_ref, loga_ref = refs
        y_ref[...] = y
        h, loga = _norm_gate(y, g_ref, wlr_ref, wg_ref, bg_ref)
        h_ref[...] = h
        loga_ref[...] = loga


def _merge_out(ba, bb, bc, proj_b, x, l, w_branch, w_out, norm_g, w_lr, w_g, b_gate):
    first, last = l == 0, l + 1 == DEPTH
    kern = lambda *refs: _merge_out_kernel(first, last, *refs)
    row = lambda w: pl.BlockSpec((TM_OUT, w), lambda i: (i, 0))
    gate = lambda n: pl.BlockSpec((TM_OUT, D_MODEL), lambda i: (i, B_G // D_MODEL + n))
    resident = pl.Buffered(1)
    in_specs = [row(BRANCH), row(BRANCH), row(BRANCH), gate(0), gate(1), gate(2)]
    in_specs += _split_rows_specs(D_MODEL) if first else [row(D_MODEL)]
    in_specs += [
        pl.BlockSpec((None, N_BRANCH, BRANCH, D_MODEL), lambda i: (l, 0, 0, 0), pipeline_mode=resident),
        pl.BlockSpec((None, D_MODEL, D_MODEL), lambda i: (l, 0, 0), pipeline_mode=resident),
    ]
    args = [ba, bb, bc, proj_b, proj_b, proj_b] + (list(x) if first else [x]) + [w_branch, w_out]
    if last:
        out_specs = _split_rows_specs(D_MODEL)
        out_shape = [jax.ShapeDtypeStruct((T_P, D_MODEL), F32), jax.ShapeDtypeStruct((T_S, D_MODEL), F32)]
    else:
        in_specs += _norm_specs(l + 1, None)
        args += [norm_g, w_lr, w_g, b_gate]
        out_specs = [row(D_MODEL), row(D_MODEL), row(C_KEY)]
        out_shape = [jax.ShapeDtypeStruct((T_ALL, D_MODEL), F32), jax.ShapeDtypeStruct((T_ALL, D_MODEL), BF),
                     jax.ShapeDtypeStruct((T_ALL, C_KEY), F32)]
    return pl.pallas_call(
        kern,
        grid=(T_ALL // TM_OUT,),
        in_specs=in_specs,
        out_specs=out_specs,
        out_shape=out_shape,
        compiler_params=_cparams(1),
    )(*args)


def _rope_tables(pos):
    half = A_HEAD_DIM // 2
    inv = jnp.power(ROPE_THETA, -jnp.arange(half, dtype=F32) * (2.0 / A_HEAD_DIM))
    ang = pos.astype(F32)[:, None] * inv[None, :]
    cos, sin = jnp.cos(ang), jnp.sin(ang)
    cos128 = jnp.concatenate([cos, cos, cos, cos], axis=1)
    sin128 = jnp.concatenate([-sin, sin, -sin, sin], axis=1)
    return cos128, sin128


def _block_tril(n, blk):
    r = np.arange(n)
    m = (r[:, None] >= r[None, :]) & ((r[:, None] // blk) == (r[None, :] // blk))
    return jnp.asarray(m.astype(np.float32), dtype=BF)


def kernel(x_prompt, x_sample, cache_a_k, cache_a_v, state_pool, state_gla, norm_g, w_in, q_norm_g, k_norm_g,
           sinks, w_pool, pool_scale, w_gate_lr, b_gate, gla_norm_g, w_branch, w_out):
    w_lr = jnp.pad(w_in[:, :, A_COLS:A_COLS + C_GATE_RANK],
                   ((0, 0), (0, 0), (0, LANES - C_GATE_RANK))).astype(BF)
    w_g = jnp.pad(w_gate_lr, ((0, 0), (0, LANES - C_GATE_RANK), (0, 0))).astype(BF)
    w_branch_bf = w_branch.astype(BF)
    w_out_bf = w_out.astype(BF)
    w_pool_bf = w_pool.astype(BF)
    qg = jnp.tile(q_norm_g, (1, 256 // A_HEAD_DIM))[:, None, :]
    kg = jnp.tile(k_norm_g, (1, 256 // A_HEAD_DIM))[:, None, :]
    norm_g = norm_g[:, None, :]
    b_gate = b_gate[:, None, :]
    pool_scale = pool_scale[:, None, :]
    gla_norm_g = gla_norm_g[:, None, :]
    cos_p, sin_p = _rope_tables(jnp.arange(SEQ))
    cos_s, sin_s = _rope_tables(PAST_LEN + jnp.arange(DEC_SEQ))
    cos_s = jnp.tile(cos_s, (SMP_BS, 1))
    sin_s = jnp.tile(sin_s, (SMP_BS, 1))
    lane = np.arange(256)
    bd = jnp.asarray((lane[:, None] // A_HEAD_DIM == lane[None, :] // A_HEAD_DIM).astype(np.float32), dtype=BF)
    bt_s = _block_tril(SMP_GB * DEC_SEQ, DEC_SEQ)
    ones_c = jnp.ones((GLA_C, 128), BF)
    cache_k = cache_a_k.reshape(DEPTH, DEC_BATCH, WINDOW, KV_W)
    cache_v = cache_a_v.reshape(DEPTH, DEC_BATCH, WINDOW, KV_W)
    pool_prev = jnp.pad(state_pool, ((0, 0), (0, 0), (1, 0), (0, 0)))

    w_t = jnp.swapaxes(w_in, 1, 2)
    x = (x_prompt.reshape(T_P, D_MODEL), x_sample.reshape(T_S, D_MODEL))
    h, loga = _norm(x[0], x[1], 0, norm_g, w_lr, w_g, b_gate)
    pk, pv, pp, sp = [], [], [], []
    nk_buf = nv_buf = sp_buf = ss_buf = None
    for l in range(DEPTH):
        proj_a = _inproj_a(h, w_t, l)
        proj_b = _inproj_b(h, w_t, l)

        oa, knew_p = _attn_prompt(proj_a, l, sinks, cos_p, sin_p, bd, qg, kg)
        oa, nk_buf, nv_buf = _attn_sample(oa, nk_buf, nv_buf, proj_a, l, cache_k, cache_v, sinks,
                                          cos_s, sin_s, bd, qg, kg)
        ob, tail_p = _pool_prompt(proj_a, l, w_pool_bf, pool_scale)
        ob, tail_s = _pool_sample(ob, proj_a, l, pool_prev, w_pool_bf, pool_scale)
        oc, sp_buf = _gla_prompt(sp_buf, proj_a, proj_b, loga, l, gla_norm_g)
        oc, ss_buf = _gla_sample(oc, ss_buf, proj_a, proj_b, loga, l, state_gla, bt_s, ones_c, gla_norm_g)

        outs = _merge_out(oa, ob, oc, proj_b, x, l, w_branch_bf, w_out_bf, norm_g, w_lr, w_g, b_gate)
        if l + 1 < DEPTH:
            x, h, loga = outs
        else:
            yp, ys = outs

        vp = proj_a[:T_P, A_AV:A_AV + KV_W].reshape(BATCH, SEQ, KV_W)[:, SEQ - WINDOW:].astype(F32)
        pk.append(knew_p)
        pv.append(vp)
        pp.append(tail_p[:, 1:])
        sp.append(tail_s[:, 1:])

    kv5 = lambda a, nb: a.reshape(DEPTH, nb, WINDOW, A_KV_HEADS, A_HEAD_DIM)
    yp = yp.reshape(BATCH, SEQ, D_MODEL)
    ys = ys.reshape(DEC_BATCH, DEC_SEQ, D_MODEL)
    return (yp, ys, kv5(jnp.stack(pk), BATCH), kv5(jnp.stack(pv), BATCH), jnp.stack(pp), sp_buf,
            kv5(nk_buf, DEC_BATCH), kv5(nv_buf, DEC_BATCH), jnp.stack(sp), ss_buf)
```

```python
import numpy as np
import jax
import jax.numpy as jnp
from jax import lax
from jax.experimental import pallas as pl
from jax.experimental.pallas import tpu as pltpu

F32 = jnp.float32
BF = jnp.bfloat16

D_MODEL = 2048
BATCH = 4
SEQ = 2048
DEPTH = 4
DEC_BATCH = 32
DEC_SEQ = 8
PAST_LEN = 16384
BRANCH = D_MODEL // 2
N_BRANCH = 3
A_HEADS = 16
A_KV_HEADS = 4
A_HEAD_DIM = 64
A_GROUP = A_HEADS // A_KV_HEADS
WINDOW = 128
ROPE_THETA = 10000.0
POOL_WINDOWS = (2, 4, 8, 16)
POOL_GROUP_DIM = 256
POOL_HIST = 15
C_HEADS = 4
C_KEY = 512
C_DK = 128
C_DV = 256
C_GATE_RANK = 16
C_GATE_TAU = 16.0
EPS = 1e-6
KV_W = A_KV_HEADS * A_HEAD_DIM
IN_COLS = 13840

T_P = BATCH * SEQ
T_S = DEC_BATCH * DEC_SEQ
T_ALL = T_P + T_S

A_AQ, A_AK, A_AV, A_AZ = 0, 1024, 1280, 1536
A_BU, A_BZ = 2560, 3584
A_CQ, A_CK, A_CV = 4608, 5120, 5632
A_COLS = 6656
LR_SHIFT = C_GATE_RANK
B_COLS = IN_COLS - A_COLS - LR_SHIFT
B_G, B_CZ = 0, N_BRANCH * D_MODEL

LANES = 128
TM = 1056
TN_A = 1664
TN_B = 1024
W_SUB = 512
CAST_ROWS = 128
TM_OUT = 256
N_P_TILES = T_P // TM_OUT
N_TILES = T_ALL // TM_OUT
ATT_BLK = WINDOW
POOL_TP = 512
POOL_HALO = 16
GLA_C = 64
GLA_NSUB = 8
GLA_SC = GLA_C * GLA_NSUB
GLA_HPS = 4
GLA_VW = 512
GLA_VBLKS = GLA_HPS * C_DV // GLA_VW
SMP_BS = 4
SMP_GB = 8
VMEM_LIMIT = 56 * 1024 * 1024
NEG_BIG = -1e30


def _cparams(n_axes):
    return pltpu.CompilerParams(
        dimension_semantics=("arbitrary",) * n_axes, vmem_limit_bytes=VMEM_LIMIT)


def _sigmoid(x):
    return 0.5 * jnp.tanh(0.5 * x) + 0.5


def _silu(x):
    return x * _sigmoid(x)


def _dot(a, b):
    return jnp.dot(a, b, preferred_element_type=F32)


def _dot_nt(a, b):
    return lax.dot_general(a, b, (((1,), (1,)), ((), ())), preferred_element_type=F32)


def _dot_tn(a, b):
    return lax.dot_general(a, b, (((0,), (0,)), ((), ())), preferred_element_type=F32)


def _split3(x):
    hi = x.astype(BF)
    r1 = x - hi.astype(F32)
    mid = r1.astype(BF)
    lo = (r1 - mid.astype(F32)).astype(BF)
    return hi, mid, lo


def _any_spec():
    return pl.BlockSpec(memory_space=pl.ANY)


def _row_spec(l, width):
    return pl.BlockSpec((None, 1, width), lambda *a: (l, 0, 0))


def _norm_gate(xf, g_ref, wlr_ref, wg_ref, bg_ref):
    ms = jnp.mean(xf * xf, axis=-1, keepdims=True)
    h = (xf * lax.rsqrt(ms + EPS) * g_ref[...]).astype(BF)
    lr = _dot(h, wlr_ref[...])
    z = _dot(lr.astype(BF), wg_ref[...]) + bg_ref[...]
    log_sig = jnp.minimum(z, 0.0) - jnp.log1p(jnp.exp(-jnp.abs(z)))
    return h, log_sig * (1.0 / C_GATE_TAU)


def _norm_specs(l):
    return [
        _row_spec(l, D_MODEL),
        pl.BlockSpec((None, D_MODEL, LANES), lambda *a: (l, 0, 0)),
        pl.BlockSpec((None, LANES, C_KEY), lambda *a: (l, 0, 0)),
        _row_spec(l, C_KEY),
    ]


def _split_rows_specs(width):
    return [pl.BlockSpec((TM_OUT, width), lambda i: (jnp.minimum(i, N_P_TILES - 1), 0)),
            pl.BlockSpec((TM_OUT, width), lambda i: (0, 0))]


def _pick_rows(xp_ref, xs_ref):
    return jnp.where(pl.program_id(0) < N_P_TILES, xp_ref[...], xs_ref[...])


def _norm_kernel(xp_ref, xs_ref, g_ref, wlr_ref, wg_ref, bg_ref, h_ref, loga_ref):
    h, loga = _norm_gate(_pick_rows(xp_ref, xs_ref), g_ref, wlr_ref, wg_ref, bg_ref)
    h_ref[...] = h
    loga_ref[...] = loga


def _norm(xp, xs, l, norm_g, w_lr, w_g, b_gate):
    return pl.pallas_call(
        _norm_kernel,
        grid=(N_TILES,),
        in_specs=_split_rows_specs(D_MODEL) + _norm_specs(l),
        out_specs=[pl.BlockSpec((TM_OUT, D_MODEL), lambda i: (i, 0)),
                   pl.BlockSpec((TM_OUT, C_KEY), lambda i: (i, 0))],
        out_shape=[jax.ShapeDtypeStruct((T_ALL, D_MODEL), BF),
                   jax.ShapeDtypeStruct((T_ALL, C_KEY), F32)],
        compiler_params=_cparams(1),
    )(xp, xs, norm_g, w_lr, w_g, b_gate)


def _inproj_a_kernel(h_ref, w_ref, o_ref, w_scr):
    @pl.when(pl.program_id(1) == 0)
    def _():
        for r in range(0, TN_A, CAST_ROWS):
            w_scr[r:r + CAST_ROWS, :] = w_ref[r:r + CAST_ROWS, :].astype(BF)

    o_ref[...] = _dot_nt(h_ref[...], w_scr[...]).astype(o_ref.dtype)


def _inproj_a(h, w_t, l):
    return pl.pallas_call(
        _inproj_a_kernel,
        grid=(A_COLS // TN_A, T_ALL // TM),
        in_specs=[
            pl.BlockSpec((TM, D_MODEL), lambda j, i: (i, 0)),
            pl.BlockSpec((None, TN_A, D_MODEL), lambda j, i: (l, j, 0)),
        ],
        out_specs=pl.BlockSpec((TM, TN_A), lambda j, i: (i, j)),
        out_shape=jax.ShapeDtypeStruct((T_ALL, A_COLS), BF),
        scratch_shapes=[pltpu.VMEM((TN_A, D_MODEL), BF)],
        compiler_params=_cparams(2),
    )(h, w_t)


def _copy_cast_rows(dst, dst0, src, src0, n):
    for r in range(0, n, CAST_ROWS):
        m = min(CAST_ROWS, n - r)
        dst[dst0 + r:dst0 + r + m, :] = src[src0 + r:src0 + r + m, :].astype(BF)


def _inproj_b_kernel(h_ref, w0_ref, w1_ref, w2_ref, o_ref, w_scr):
    @pl.when(pl.program_id(1) == 0)
    def _():
        _copy_cast_rows(w_scr, 0, w0_ref, LR_SHIFT, W_SUB - LR_SHIFT)
        _copy_cast_rows(w_scr, W_SUB - LR_SHIFT, w1_ref, 0, W_SUB)
        _copy_cast_rows(w_scr, 2 * W_SUB - LR_SHIFT, w2_ref, 0, LR_SHIFT)

    o_ref[...] = _dot_nt(h_ref[...], w_scr[...]).astype(o_ref.dtype)


def _inproj_b(h, w_t, l):
    assert TN_B == 2 * W_SUB
    nt = B_COLS // TN_B
    sub0 = A_COLS // W_SUB
    tail0 = (A_COLS + TN_B) // LR_SHIFT
    return pl.pallas_call(
        _inproj_b_kernel,
        grid=(nt, T_ALL // TM),
        in_specs=[
            pl.BlockSpec((TM, D_MODEL), lambda j, i: (i, 0)),
            pl.BlockSpec((None, W_SUB, D_MODEL), lambda j, i: (l, sub0 + 2 * j, 0)),
            pl.BlockSpec((None, W_SUB, D_MODEL), lambda j, i: (l, sub0 + 2 * j + 1, 0)),
            pl.BlockSpec((None, LR_SHIFT, D_MODEL), lambda j, i: (l, tail0 + (TN_B // LR_SHIFT) * j, 0)),
        ],
        out_specs=pl.BlockSpec((TM, TN_B), lambda j, i: (i, (j + nt - 1) % nt)),
        out_shape=jax.ShapeDtypeStruct((T_ALL, B_COLS), BF),
        scratch_shapes=[pltpu.VMEM((TN_B, D_MODEL), BF)],
        compiler_params=_cparams(2),
    )(h, w_t, w_t, w_t)


def _head_norm(x, bd, g_row):
    outs = []
    for c in range(x.shape[1] // 256):
        xc = x[:, c * 256:(c + 1) * 256]
        ss = _dot((xc * xc).astype(BF), bd)
        outs.append(xc * lax.rsqrt(ss * (1.0 / A_HEAD_DIM) + EPS) * g_row)
    return outs


def _rope128(xc, cos, sin, first_half):
    swapped = jnp.where(first_half, pltpu.roll(xc, 96, 1), pltpu.roll(xc, 32, 1))
    return xc * cos + swapped * sin


def _norm_rope(x, bd, g_row, cos, sin, first_half):
    chunks = []
    for blk in _head_norm(x, bd, g_row):
        for c in range(2):
            chunks.append(_rope128(blk[:, c * 128:(c + 1) * 128], cos, sin, first_half))
    return chunks


def _dup_half(chunk, rolled, lo_half, use_low):
    return jnp.where(lo_half, chunk, rolled) if use_low else jnp.where(lo_half, rolled, chunk)


def _attn_prompt_kernel(l, sinks_ref, q_ref, z0_ref, z1_ref, k_ref, v_ref, cos_ref, sin_ref, bd_ref, qg_ref,
                        kg_ref, o_ref, knew_ref, kprev, vprev_e, vprev_o):
    n = pl.program_id(1)
    R = ATT_BLK
    lane = lax.broadcasted_iota(jnp.int32, (R, 128), 1)
    lo_half = lane < 64
    hi_half = jnp.logical_not(lo_half)
    first_half = (lane & 32) == 0
    cos = cos_ref[...]
    sin = sin_ref[...]
    bd = bd_ref[...]

    @pl.when(n == 0)
    def _():
        kprev[...] = jnp.zeros_like(kprev)
        vprev_e[...] = jnp.zeros_like(vprev_e)
        vprev_o[...] = jnp.zeros_like(vprev_o)

    kch = _norm_rope(k_ref[...].astype(F32), bd, kg_ref[...], cos, sin, first_half)
    knew_ref[0] = jnp.concatenate(kch, axis=1)
    vraw = v_ref[...].astype(F32)
    vch = [vraw[:, :128], vraw[:, 128:]]
    kcur, vcur_e, vcur_o = [], [], []
    for h in range(A_KV_HEADS):
        c, low = h // 2, (h % 2 == 0)
        kcur.append(_dup_half(kch[c], pltpu.roll(kch[c], 64, 1), lo_half, low).astype(BF))
        v_lo = vch[c] if low else pltpu.roll(vch[c], 64, 1)
        v_hi = pltpu.roll(vch[c], 64, 1) if low else vch[c]
        vcur_e.append(jnp.where(lo_half, v_lo, 1.0).astype(BF))
        vcur_o.append(jnp.where(lo_half, 1.0, v_hi).astype(BF))

    qch = _norm_rope(q_ref[...].astype(F32), bd, qg_ref[...], cos, sin, first_half)

    G = A_GROUP * R
    srow = lax.broadcasted_iota(jnp.int32, (G, 128), 0)
    scol = lax.broadcasted_iota(jnp.int32, (G, 128), 1)
    qi = srow & (R - 1)
    bias_prev = jnp.where(jnp.logical_and(scol >= qi, n > 0), 0.0, NEG_BIG)
    bias_cur = jnp.where(scol <= qi, 0.0, NEG_BIG)

    def scores(h):
        qa = qch[2 * h] * (A_HEAD_DIM ** -0.5)
        qb = qch[2 * h + 1] * (A_HEAD_DIM ** -0.5)
        q_stack = jnp.concatenate([jnp.where(lo_half, qa, 0.0), jnp.where(lo_half, qb, 0.0),
                                   jnp.where(hi_half, qa, 0.0), jnp.where(hi_half, qb, 0.0)],
                                  axis=0).astype(BF)
        return _dot_nt(q_stack, kprev[h]) + bias_prev, _dot_nt(q_stack, kcur[h]) + bias_cur

    ahead = scores(0)
    for h in range(A_KV_HEADS):
        s_p, s_c = ahead
        if h + 1 < A_KV_HEADS:
            ahead = scores(h + 1)
        head_of_blk = (4 * h, 4 * h + 2, 4 * h + 1, 4 * h + 3)
        row_max = jnp.max(jnp.maximum(s_p, s_c), axis=-1, keepdims=True)
        m_blk = [jnp.maximum(row_max[bi * R:(bi + 1) * R], sinks_ref[l, hd]) for bi, hd in enumerate(head_of_blk)]
        e_sink = [jnp.exp(sinks_ref[l, hd] - m_blk[bi]) for bi, hd in enumerate(head_of_blk)]
        m = jnp.concatenate(m_blk, axis=0)
        p_p = jnp.exp(s_p - m).astype(BF)
        p_c = jnp.exp(s_c - m).astype(BF)
        half = G // 2
        pv_e = _dot(p_p[:half], vprev_e[h]) + _dot(p_c[:half], vcur_e[h])
        pv_o = _dot(p_p[half:], vprev_o[h]) + _dot(p_c[half:], vcur_o[h])
        for cc in range(2):
            rows_e = slice(cc * R, (cc + 1) * R)
            ev, od = pv_e[rows_e], pv_o[rows_e]
            num = jnp.where(lo_half, ev, od)
            den = pltpu.roll(jnp.where(lo_half, od, ev), 64, 1)
            den = den + jnp.where(lo_half, e_sink[cc], e_sink[2 + cc])
            c = 2 * h + cc
            z_ref = z0_ref if c < 4 else z1_ref
            zc = z_ref[:, (c % 4) * 128:(c % 4 + 1) * 128].astype(F32)
            o_ref[:, c * 128:(c + 1) * 128] = (num / den * _silu(zc)).astype(o_ref.dtype)

    for h in range(A_KV_HEADS):
        kprev[h] = kcur[h]
        vprev_e[h] = vcur_e[h]
        vprev_o[h] = vcur_o[h]


def _attn_prompt(proj_a, l, sinks, cos, sin, bd, qg, kg):
    nb = SEQ // ATT_BLK
    rb = lambda b, n: b * nb + n
    kern = lambda *refs: _attn_prompt_kernel(l, *refs)
    return pl.pallas_call(
        kern,
        grid=(BATCH, nb),
        in_specs=[
            pl.BlockSpec(memory_space=pltpu.SMEM),
            pl.BlockSpec((ATT_BLK, BRANCH), lambda b, n: (rb(b, n), A_AQ // BRANCH)),
            pl.BlockSpec((ATT_BLK, 512), lambda b, n: (rb(b, n), A_AZ // 512)),
            pl.BlockSpec((ATT_BLK, 512), lambda b, n: (rb(b, n), A_AZ // 512 + 1)),
            pl.BlockSpec((ATT_BLK, KV_W), lambda b, n: (rb(b, n), A_AK // KV_W)),
            pl.BlockSpec((ATT_BLK, KV_W), lambda b, n: (rb(b, n), A_AV // KV_W)),
            pl.BlockSpec((ATT_BLK, 128), lambda b, n: (n, 0)),
            pl.BlockSpec((ATT_BLK, 128), lambda b, n: (n, 0)),
            pl.BlockSpec((256, 256), lambda b, n: (0, 0)),
            _row_spec(l, 256),
            _row_spec(l, 256),
        ],
        out_specs=[
            pl.BlockSpec((ATT_BLK, BRANCH), lambda b, n: (rb(b, n), 0)),
            pl.BlockSpec((1, ATT_BLK, KV_W), lambda b, n: (b, 0, 0)),
        ],
        out_shape=[
            jax.ShapeDtypeStruct((T_ALL, BRANCH), BF),
            jax.ShapeDtypeStruct((BATCH, WINDOW, KV_W), F32),
        ],
        scratch_shapes=[pltpu.VMEM((A_KV_HEADS, ATT_BLK, 128), BF),
                        pltpu.VMEM((A_KV_HEADS, ATT_BLK, 128), BF),
                        pltpu.VMEM((A_KV_HEADS, ATT_BLK, 128), BF)],
        compiler_params=_cparams(2),
    )(sinks, proj_a, proj_a, proj_a, proj_a, proj_a, cos, sin, bd, qg, kg)


def _attn_sample_kernel(l, n_alias, *refs):
    refs = refs[n_alias:]
    (sinks_ref, q_ref, z0_ref, z1_ref, k_ref, v_ref, ck_ref, cv_ref, cos_ref, sin_ref,
     bd_ref, qg_ref, kg_ref, o_ref, nk_ref, nv_ref, kall, vall) = refs
    L = DEC_SEQ
    R = SMP_BS * L
    lane = lax.broadcasted_iota(jnp.int32, (R, 128), 1)
    first_half = (lane & 32) == 0
    lane8 = lax.broadcasted_iota(jnp.int32, (L, 128), 1)
    lo8 = lane8 < 64
    lane_c = lax.broadcasted_iota(jnp.int32, (WINDOW, 128), 1)
    lo_c = lane_c < 64
    cos = cos_ref[...]
    sin = sin_ref[...]
    bd = bd_ref[...]

    @pl.when(pl.program_id(0) == 0)
    def _():
        kall[...] = jnp.zeros_like(kall)
        vall[...] = jnp.zeros_like(vall)

    kch = _norm_rope(k_ref[...].astype(F32), bd, kg_ref[...], cos, sin, first_half)
    vraw = v_ref[...].astype(F32)
    vch = [vraw[:, :128], vraw[:, 128:]]
    qch = _norm_rope(q_ref[...].astype(F32), bd, qg_ref[...], cos, sin, first_half)

    rows = A_GROUP * L
    srow = lax.broadcasted_iota(jnp.int32, (rows, 2 * WINDOW), 0)
    scol = lax.broadcasted_iota(jnp.int32, (rows, 2 * WINDOW), 1)
    t = srow & (L - 1)
    mask = jnp.logical_or(jnp.logical_and(scol < WINDOW, scol >= t),
                          jnp.logical_and(scol >= WINDOW, (scol - WINDOW) <= t))
    hrow = lax.broadcasted_iota(jnp.int32, (rows, 1), 0) >> (L.bit_length() - 1)
    z_all = jnp.concatenate([z0_ref[...], z1_ref[...]], axis=1).astype(F32)
    bias = jnp.where(mask, 0.0, NEG_BIG)
    pairs = [(bi, h) for bi in range(SMP_BS) for h in range(A_KV_HEADS)]
    slot = lambda bi, h: bi * A_KV_HEADS + h

    for bi in range(SMP_BS):
        r0 = bi * L
        knew = [kc[r0:r0 + L] for kc in kch]
        vnew = [vc[r0:r0 + L] for vc in vch]
        ck = ck_ref[bi]
        cv = cv_ref[bi]
        nk_ref[bi, 0:WINDOW - L] = ck[L:WINDOW]
        nv_ref[bi, 0:WINDOW - L] = cv[L:WINDOW]
        nk_ref[bi, WINDOW - L:WINDOW] = jnp.concatenate(knew, axis=1)
        nv_ref[bi, WINDOW - L:WINDOW] = jnp.concatenate(vnew, axis=1)
        for h in range(A_KV_HEADS):
            c, low = h // 2, (h % 2 == 0)
            ckc = ck[:, c * 128:(c + 1) * 128]
            cvc = cv[:, c * 128:(c + 1) * 128]
            kall[slot(bi, h), 0:WINDOW] = _dup_half(ckc, pltpu.roll(ckc, 64, 1), lo_c, low)
            vall[slot(bi, h), 0:WINDOW] = _dup_half(cvc, pltpu.roll(cvc, 64, 1), lo_c, low)
            kall[slot(bi, h), WINDOW:WINDOW + L] = _dup_half(knew[c], pltpu.roll(knew[c], 64, 1), lo8, low)
            vall[slot(bi, h), WINDOW:WINDOW + L] = _dup_half(vnew[c], pltpu.roll(vnew[c], 64, 1), lo8, low)

    scores = []
    for bi, h in pairs:
        r0 = bi * L
        qs = []
        for gi in range(A_GROUP):
            i = h * A_GROUP + gi
            sel = lo8 if i % 2 == 0 else jnp.logical_not(lo8)
            qs.append(jnp.where(sel, qch[i // 2][r0:r0 + L] * (A_HEAD_DIM ** -0.5), 0.0))
        qm = jnp.concatenate(qs, axis=0).astype(BF)
        scores.append(_dot_nt(qm, kall[slot(bi, h)].astype(BF)) + bias)

    probs, dens = [], []
    for (bi, h), s in zip(pairs, scores):
        sink = jnp.zeros((rows, 1), F32)
        for gi in range(A_GROUP):
            sink = jnp.where(hrow == gi, sinks_ref[l, h * A_GROUP + gi], sink)
        m = jnp.maximum(jnp.max(s, axis=-1, keepdims=True), sink)
        p = jnp.exp(s - m)
        dens.append(jnp.sum(p, axis=-1, keepdims=True) + jnp.exp(sink - m))
        probs.append(p.astype(BF))

    pvs = [_dot(p, vall[slot(bi, h)].astype(BF)) / den for (bi, h), p, den in zip(pairs, probs, dens)]
    o_rows = []
    for bi in range(SMP_BS):
        o_chunks = []
        for h in range(A_KV_HEADS):
            pv = pvs[slot(bi, h)]
            for cc in range(2):
                o_chunks.append(jnp.where(lo8, pv[(2 * cc) * L:(2 * cc + 1) * L],
                                          pv[(2 * cc + 1) * L:(2 * cc + 2) * L]))
        o = jnp.concatenate(o_chunks, axis=1)
        o_rows.append(o * _silu(z_all[bi * L:(bi + 1) * L]))
    o_ref[...] = jnp.concatenate(o_rows, axis=0).astype(o_ref.dtype)


def _attn_sample(oa, nk_buf, nv_buf, proj_a, l, cache_k, cache_v, sinks, cos, sin, bd, qg, kg):
    R = SMP_BS * DEC_SEQ
    base = T_P // R
    alias_in = [oa] + ([nk_buf, nv_buf] if l > 0 else [])
    n_alias = len(alias_in)
    kern = lambda *refs: _attn_sample_kernel(l, n_alias, *refs)
    cache_spec = pl.BlockSpec((None, SMP_BS, WINDOW, KV_W), lambda b: (l, b, 0, 0))
    return pl.pallas_call(
        kern,
        grid=(DEC_BATCH // SMP_BS,),
        in_specs=[_any_spec()] * n_alias + [
            pl.BlockSpec(memory_space=pltpu.SMEM),
            pl.BlockSpec((R, BRANCH), lambda b: (base + b, A_AQ // BRANCH)),
            pl.BlockSpec((R, 512), lambda b: (base + b, A_AZ // 512)),
            pl.BlockSpec((R, 512), lambda b: (base + b, A_AZ // 512 + 1)),
            pl.BlockSpec((R, KV_W), lambda b: (base + b, A_AK // KV_W)),
            pl.BlockSpec((R, KV_W), lambda b: (base + b, A_AV // KV_W)),
            cache_spec, cache_spec,
            pl.BlockSpec((R, 128), lambda b: (0, 0)),
            pl.BlockSpec((R, 128), lambda b: (0, 0)),
            pl.BlockSpec((256, 256), lambda b: (0, 0)),
            _row_spec(l, 256),
            _row_spec(l, 256),
        ],
        out_specs=[pl.BlockSpec((R, BRANCH), lambda b: (base + b, 0)), cache_spec, cache_spec],
        out_shape=[
            jax.ShapeDtypeStruct((T_ALL, BRANCH), BF),
            jax.ShapeDtypeStruct((DEPTH, DEC_BATCH, WINDOW, KV_W), F32),
            jax.ShapeDtypeStruct((DEPTH, DEC_BATCH, WINDOW, KV_W), F32),
        ],
        scratch_shapes=[pltpu.VMEM((SMP_BS * A_KV_HEADS, 2 * WINDOW, 128), F32),
                        pltpu.VMEM((SMP_BS * A_KV_HEADS, 2 * WINDOW, 128), F32)],
        input_output_aliases={i: i for i in range(n_alias)},
        compiler_params=_cparams(1),
    )(*alias_in, sinks, proj_a, proj_a, proj_a, proj_a, proj_a, cache_k, cache_v, cos, sin, bd, qg, kg)


def _pool_compute(ext, u_rows, start_pos, wp_ref, scale_ref, z, n_rows):
    pos = start_pos + lax.broadcasted_iota(jnp.int32, (n_rows, 1), 0)
    outs = []
    for g, w in enumerate(POOL_WINDOWS):
        sl = slice(g * POOL_GROUP_DIM, (g + 1) * POOL_GROUP_DIM)
        acc = ext[:, sl]
        span = 1
        while span < w:
            acc = acc + pltpu.roll(acc, span, 0)
            span *= 2
        win_sum = acc[POOL_HALO:]
        count = jnp.minimum(w, pos + 1).astype(F32)
        d = (win_sum / count - u_rows[:, sl]).astype(BF)
        outs.append(_dot(d, wp_ref[g]))
    y = jnp.concatenate(outs, axis=1) * scale_ref[...]
    return y * _silu(z)


def _pool_prompt_kernel(u0_ref, u1_ref, z0_ref, z1_ref, wp_ref, scale_ref, o_ref, tail_ref, ext):
    t = pl.program_id(1)

    @pl.when(t == 0)
    def _():
        ext[0:POOL_HALO] = jnp.zeros((POOL_HALO, BRANCH), F32)

    u = jnp.concatenate([u0_ref[...], u1_ref[...]], axis=1).astype(F32)
    z = jnp.concatenate([z0_ref[...], z1_ref[...]], axis=1).astype(F32)
    ext[POOL_HALO:] = u
    out = _pool_compute(ext[...], u, t * POOL_TP, wp_ref, scale_ref, z, POOL_TP)
    o_ref[...] = out.astype(o_ref.dtype)
    tail = ext[POOL_TP:POOL_TP + POOL_HALO]
    tail_ref[0] = tail
    ext[0:POOL_HALO] = tail


def _pool_w_specs(l):
    ng = len(POOL_WINDOWS)
    return [pl.BlockSpec((None, ng, POOL_GROUP_DIM, POOL_GROUP_DIM), lambda *a: (l, 0, 0, 0)),
            _row_spec(l, BRANCH)]


def _pool_prompt(proj_a, l, w_pool, scale):
    nt = SEQ // POOL_TP
    half = lambda off, k: pl.BlockSpec((POOL_TP, 512), lambda b, t: (b * nt + t, off // 512 + k))
    return pl.pallas_call(
        _pool_prompt_kernel,
        grid=(BATCH, nt),
        in_specs=[half(A_BU, 0), half(A_BU, 1), half(A_BZ, 0), half(A_BZ, 1)] + _pool_w_specs(l),
        out_specs=[
            pl.BlockSpec((POOL_TP, BRANCH), lambda b, t: (b * nt + t, 0)),
            pl.BlockSpec((1, POOL_HALO, BRANCH), lambda b, t: (b, 0, 0)),
        ],
        out_shape=[
            jax.ShapeDtypeStruct((T_ALL, BRANCH), BF),
            jax.ShapeDtypeStruct((BATCH, POOL_HALO, BRANCH), F32),
        ],
        scratch_shapes=[pltpu.VMEM((POOL_HALO + POOL_TP, BRANCH), F32)],
        compiler_params=_cparams(2),
    )(proj_a, proj_a, proj_a, proj_a, w_pool, scale)


def _pool_sample_kernel(ob_in_ref, u0_ref, u1_ref, z0_ref, z1_ref, prev_ref, wp_ref, scale_ref, o_ref, tail_ref):
    del ob_in_ref
    L = DEC_SEQ
    u_all = jnp.concatenate([u0_ref[...], u1_ref[...]], axis=1).astype(F32)
    z_all = jnp.concatenate([z0_ref[...], z1_ref[...]], axis=1).astype(F32)
    outs = []
    for bi in range(SMP_BS):
        u = u_all[bi * L:(bi + 1) * L]
        ext = jnp.concatenate([prev_ref[bi], u], axis=0)
        outs.append(_pool_compute(ext, u, PAST_LEN, wp_ref, scale_ref, z_all[bi * L:(bi + 1) * L], L))
        tail_ref[bi] = ext[L:L + POOL_HALO]
    o_ref[...] = jnp.concatenate(outs, axis=0).astype(o_ref.dtype)


def _pool_sample(ob, proj_a, l, prev16, w_pool, scale):
    R = SMP_BS * DEC_SEQ
    base = T_P // R
    half = lambda off, k: pl.BlockSpec((R, 512), lambda b: (base + b, off // 512 + k))
    return pl.pallas_call(
        _pool_sample_kernel,
        grid=(DEC_BATCH // SMP_BS,),
        in_specs=[_any_spec(), half(A_BU, 0), half(A_BU, 1), half(A_BZ, 0), half(A_BZ, 1),
                  pl.BlockSpec((None, SMP_BS, POOL_HALO, BRANCH), lambda b: (l, b, 0, 0))] + _pool_w_specs(l),
        out_specs=[
            pl.BlockSpec((R, BRANCH), lambda b: (base + b, 0)),
            pl.BlockSpec((SMP_BS, POOL_HALO, BRANCH), lambda b: (b, 0, 0)),
        ],
        out_shape=[
            jax.ShapeDtypeStruct((T_ALL, BRANCH), BF),
            jax.ShapeDtypeStruct((DEC_BATCH, POOL_HALO, BRANCH), F32),
        ],
        input_output_aliases={0: 0},
        compiler_params=_cparams(1),
    )(ob, proj_a, proj_a, proj_a, proj_a, prev16, w_pool, scale)


def _chunk_cumsum(x, chunk):
    pos = lax.broadcasted_iota(jnp.int32, x.shape, 0) & (chunk - 1)
    span = 1
    while span < chunk:
        x = x + jnp.where(pos >= span, pltpu.roll(x, span, 0), 0.0)
        span *= 2
    return x


def _chunk_row(x, chunk, r):
    n = x.shape[0] // chunk
    return jnp.concatenate(
        [jnp.broadcast_to(x[c * chunk + r:c * chunk + r + 1], (chunk, x.shape[1])) for c in range(n)], axis=0)


def _gla_prompt_kernel(n_alias, *refs):
    refs = refs[n_alias:]
    q_ref, k_ref = refs[:2]
    v_refs = refs[2:2 + GLA_VBLKS]
    z_ref, la_ref, g_ref, o_ref, snew_ref, st_scr = refs[2 + GLA_VBLKS:]
    step = pl.program_id(2)

    @pl.when(step == 0)
    def _():
        st_scr[...] = jnp.zeros_like(st_scr)

    C = GLA_C
    chunks = [slice(c * C, (c + 1) * C) for c in range(GLA_NSUB)]
    row = lax.broadcasted_iota(jnp.int32, (GLA_SC, GLA_SC), 0)
    col = lax.broadcasted_iota(jnp.int32, (GLA_SC, GLA_SC), 1)
    visible = (row - col).astype(jnp.uint32) <= (row & (C - 1)).astype(jnp.uint32)
    g_row = g_ref[...]

    finals = []
    for hh in range(GLA_HPS):
        kcols = slice(hh * C_DK, (hh + 1) * C_DK)
        vcols = slice(hh * C_DV, (hh + 1) * C_DV)
        b = _chunk_cumsum(la_ref[:, kcols], C)
        b_last = _chunk_row(b, C, C - 1)
        b_mid = _chunk_row(b, C, C // 2 - 1)
        q = q_ref[:, kcols].astype(F32) * (C_DK ** -0.5)
        k = k_ref[:, kcols].astype(F32)
        v0 = (hh * C_DV) % GLA_VW
        v = v_refs[hh * C_DV // GLA_VW][:, v0:v0 + C_DV].astype(BF)
        kd = (k * jnp.exp(b_last - b)).astype(BF)
        qe = (q * jnp.exp(b)).astype(BF)
        q2 = (q * jnp.exp(b - b_mid)).astype(BF)
        k2 = (k * jnp.exp(b_mid - b)).astype(BF)

        att = jnp.where(visible, _dot_nt(q2, k2), 0.0).astype(BF)
        o = _dot(att, v)

        ut = [_dot_tn(v[sl], kd[sl]) for sl in chunks]
        st = st_scr[hh]
        st_before = []
        for c, sl in enumerate(chunks):
            st_before.append(st.astype(BF))
            st = st * jnp.exp(b[(c + 1) * C - 1:(c + 1) * C]) + ut[c]
        st_scr[hh] = st
        finals.append(st)

        o = o + jnp.concatenate([_dot_nt(qe[sl], st_before[c]) for c, sl in enumerate(chunks)], axis=0)
        ms = jnp.mean(o * o, axis=-1, keepdims=True)
        on = o * lax.rsqrt(ms + EPS) * g_row
        o_ref[:, vcols] = (on * _silu(z_ref[:, vcols].astype(F32))).astype(o_ref.dtype)

    @pl.when(step == pl.num_programs(2) - 1)
    def _():
        for hh in range(GLA_HPS):
            snew_ref[0, hh] = finals[hh].T


def _gla_prompt(s_buf, proj_a, proj_b, loga, l, g):
    ns = SEQ // GLA_SC
    rb = lambda b, h, s: b * ns + s
    alias_in = [s_buf] if l > 0 else []
    n_alias = len(alias_in)
    kern = lambda *refs: _gla_prompt_kernel(n_alias, *refs)
    kw, vw = GLA_HPS * C_DK, GLA_HPS * C_DV

    def cols(width, offset, stride=1, extra=0):
        assert offset % width == 0
        return pl.BlockSpec((GLA_SC, width), lambda b, h, s: (rb(b, h, s), offset // width + stride * h + extra))

    return pl.pallas_call(
        kern,
        grid=(BATCH, C_HEADS // GLA_HPS, ns),
        in_specs=[_any_spec()] * n_alias + [cols(kw, A_CQ), cols(kw, A_CK)]
        + [cols(GLA_VW, A_CV, GLA_VBLKS, j) for j in range(GLA_VBLKS)]
        + [cols(vw, B_CZ), cols(kw, 0), _row_spec(l, C_DV)],
        out_specs=[
            pl.BlockSpec((GLA_SC, vw), lambda b, h, s: (rb(b, h, s), h)),
            pl.BlockSpec((None, 1, GLA_HPS, C_DK, C_DV), lambda b, h, s: (l, b, h, 0, 0)),
        ],
        out_shape=[
            jax.ShapeDtypeStruct((T_ALL, BRANCH), BF),
            jax.ShapeDtypeStruct((DEPTH, BATCH, C_HEADS, C_DK, C_DV), F32),
        ],
        scratch_shapes=[pltpu.VMEM((GLA_HPS, C_DV, C_DK), F32)],
        input_output_aliases={0: 1} if l > 0 else {},
        compiler_params=_cparams(3),
    )(*alias_in, *([proj_a] * (2 + GLA_VBLKS)), proj_b, loga, g)


def _gla_sample_kernel(n_alias, *refs):
    refs = refs[n_alias:]
    q_ref, k_ref, v_ref, z_ref, la_ref, s0_ref, bt_ref, ones_ref, g_ref, o_ref, snew_ref = refs
    C = SMP_GB * DEC_SEQ
    L = DEC_SEQ
    la = la_ref[...]
    parts = _split3(la)
    bt = bt_ref[...]
    b = _dot(bt, parts[0]) + _dot(bt, parts[1]) + _dot(bt, parts[2])
    q = q_ref[...].astype(F32) * (C_DK ** -0.5)
    k = k_ref[...].astype(F32)
    v = v_ref[...].astype(BF)
    ones = ones_ref[...]
    row = lax.broadcasted_iota(jnp.int32, (C, C), 0)
    col = lax.broadcasted_iota(jnp.int32, (C, C), 1)
    shift = L.bit_length() - 1
    tri = jnp.logical_and(row >= col, (row >> shift) == (col >> shift))
    rowk = lax.broadcasted_iota(jnp.int32, (C, C_DK), 0) >> shift

    b_last = jnp.zeros_like(b)
    for gi in range(SMP_GB):
        b_last = jnp.where(rowk == gi, b[(gi + 1) * L - 1:(gi + 1) * L], b_last)
    kd = k * jnp.exp(b_last - b)
    qe = q * jnp.exp(b)

    o_inter = jnp.zeros((C, C_DV), F32)
    for gi in range(SMP_GB):
        own = rowk == gi
        s0 = s0_ref[gi, 0]
        o_inter = o_inter + _dot(jnp.where(own, qe, 0.0).astype(BF), s0.astype(BF))
        u = _dot_tn(jnp.where(own, kd, 0.0).astype(BF), v)
        own_parts = _split3(jnp.where(own, la, 0.0))
        dsum = _dot_tn(own_parts[0], ones) + _dot_tn(own_parts[1], ones) + _dot_tn(own_parts[2], ones)
        d = jnp.exp(dsum)
        snew_ref[gi, 0] = jnp.concatenate([d, d], axis=1) * s0 + u

    b_mid = jnp.zeros_like(b)
    for gi in range(SMP_GB):
        b_mid = jnp.where(rowk == gi, b[gi * L + L // 2 - 1:gi * L + L // 2], b_mid)
    q2 = (q * jnp.exp(b - b_mid)).astype(BF)
    k2 = (k * jnp.exp(b_mid - b)).astype(BF)
    att = jnp.where(tri, _dot_nt(q2, k2), 0.0).astype(BF)
    o = o_inter + _dot(att, v)
    ms = jnp.mean(o * o, axis=-1, keepdims=True)
    on = o * lax.rsqrt(ms + EPS) * g_ref[...]
    o_ref[...] = (on * _silu(z_ref[...].astype(F32))).astype(o_ref.dtype)


def _gla_sample(oc, s_buf, proj_a, proj_b, loga, l, s0, bt, ones, g):
    C = SMP_GB * DEC_SEQ
    base = T_P // C
    alias_in = [oc] + ([s_buf] if l > 0 else [])
    n_alias = len(alias_in)
    kern = lambda *refs: _gla_sample_kernel(n_alias, *refs)
    state_spec = pl.BlockSpec((None, SMP_GB, 1, C_DK, C_DV), lambda b, h: (l, b, h, 0, 0))
    return pl.pallas_call(
        kern,
        grid=(DEC_BATCH // SMP_GB, C_HEADS),
        in_specs=[_any_spec()] * n_alias + [
            pl.BlockSpec((C, C_DK), lambda b, h: (base + b, A_CQ // C_DK + h)),
            pl.BlockSpec((C, C_DK), lambda b, h: (base + b, A_CK // C_DK + h)),
            pl.BlockSpec((C, C_DV), lambda b, h: (base + b, A_CV // C_DV + h)),
            pl.BlockSpec((C, C_DV), lambda b, h: (base + b, B_CZ // C_DV + h)),
            pl.BlockSpec((C, C_DK), lambda b, h: (base + b, h)),
            state_spec,
            pl.BlockSpec((C, C), lambda b, h: (0, 0)),
            pl.BlockSpec((C, 128), lambda b, h: (0, 0)),
            _row_spec(l, C_DV),
        ],
        out_specs=[pl.BlockSpec((C, C_DV), lambda b, h: (base + b, h)), state_spec],
        out_shape=[
            jax.ShapeDtypeStruct((T_ALL, BRANCH), BF),
            jax.ShapeDtypeStruct((DEPTH, DEC_BATCH, C_HEADS, C_DK, C_DV), F32),
        ],
        input_output_aliases={i: i for i in range(n_alias)},
        compiler_params=_cparams(2),
    )(*alias_in, proj_a, proj_a, proj_a, proj_b, loga, s0, bt, ones, g)


def _merged_residual(ba_ref, bb_ref, bc_ref, g0_ref, g1_ref, g2_ref, x, wb_ref, wo_ref):
    acc = _sigmoid(g0_ref[...].astype(F32)) * _dot(ba_ref[...], wb_ref[0])
    acc = acc + _sigmoid(g1_ref[...].astype(F32)) * _dot(bb_ref[...], wb_ref[1])
    acc = acc + _sigmoid(g2_ref[...].astype(F32)) * _dot(bc_ref[...], wb_ref[2])
    return x + _dot(acc.astype(BF), wo_ref[...])


def _merge_last_kernel(ba_ref, bb_ref, bc_ref, g0_ref, g1_ref, g2_ref, x_ref, wb_ref, wo_ref, yp_ref, ys_ref):
    y = _merged_residual(ba_ref, bb_ref, bc_ref, g0_ref, g1_ref, g2_ref, x_ref[...], wb_ref, wo_ref)
    i = pl.program_id(0)

    @pl.when(i < N_P_TILES)
    def _():
        yp_ref[...] = y

    @pl.when(i == N_P_TILES)
    def _():
        ys_ref[...] = y


def _merge_next_kernel(first, *refs):
    ba_ref, bb_ref, bc_ref, g0_ref, g1_ref, g2_ref = refs[:6]
    refs = refs[6:]
    if first:
        x = _pick_rows(refs[0], refs[1])
        refs = refs[2:]
    else:
        x = refs[0][...]
        refs = refs[1:]
    wb_ref, wo_ref, g_ref, wlr_ref, wg_ref, bg_ref, y_ref, h_ref, loga_ref = refs
    y = _merged_residual(ba_ref, bb_ref, bc_ref, g0_ref, g1_ref, g2_ref, x, wb_ref, wo_ref)
    y_ref[...] = y
    h, loga = _norm_gate(y, g_ref, wlr_ref, wg_ref, bg_ref)
    h_ref[...] = h
    loga_ref[...] = loga


def _merge_out(ba, bb, bc, proj_b, x, l, w_branch, w_out, norm_g, w_lr, w_g, b_gate):
    first, last = l == 0, l + 1 == DEPTH
    resident = pl.Buffered(1)
    w_specs = [
        pl.BlockSpec((None, N_BRANCH, BRANCH, D_MODEL), lambda i: (l, 0, 0, 0), pipeline_mode=resident),
        pl.BlockSpec((None, D_MODEL, D_MODEL), lambda i: (l, 0, 0), pipeline_mode=resident),
    ]
    row = lambda w: pl.BlockSpec((TM_OUT, w), lambda i: (i, 0))
    gate = lambda n: pl.BlockSpec((TM_OUT, D_MODEL), lambda i: (i, B_G // D_MODEL + n))
    mix_specs = [row(BRANCH)] * N_BRANCH + [gate(n) for n in range(N_BRANCH)]
    if last:
        return pl.pallas_call(
            _merge_last_kernel,
            grid=(N_TILES,),
            in_specs=mix_specs + [row(D_MODEL)] + w_specs,
            out_specs=_split_rows_specs(D_MODEL),
            out_shape=[jax.ShapeDtypeStruct((T_P, D_MODEL), F32), jax.ShapeDtypeStruct((T_S, D_MODEL), F32)],
            compiler_params=_cparams(1),
        )(ba, bb, bc, proj_b, proj_b, proj_b, x, w_branch, w_out)

    x_specs = _split_rows_specs(D_MODEL) if first else [row(D_MODEL)]
    kern = lambda *refs: _merge_next_kernel(first, *refs)
    return pl.pallas_call(
        kern,
        grid=(N_TILES,),
        in_specs=mix_specs + x_specs + w_specs + _norm_specs(l + 1),
        out_specs=[row(D_MODEL), row(D_MODEL), row(C_KEY)],
        out_shape=[jax.ShapeDtypeStruct((T_ALL, D_MODEL), F32), jax.ShapeDtypeStruct((T_ALL, D_MODEL), BF),
                   jax.ShapeDtypeStruct((T_ALL, C_KEY), F32)],
        compiler_params=_cparams(1),
    )(ba, bb, bc, proj_b, proj_b, proj_b, *(list(x) if first else [x]), w_branch, w_out,
      norm_g, w_lr, w_g, b_gate)


def _rope_tables(pos):
    half = A_HEAD_DIM // 2
    inv = jnp.power(ROPE_THETA, -jnp.arange(half, dtype=F32) * (2.0 / A_HEAD_DIM))
    ang = pos.astype(F32)[:, None] * inv[None, :]
    cos, sin = jnp.cos(ang), jnp.sin(ang)
    cos128 = jnp.concatenate([cos, cos, cos, cos], axis=1)
    sin128 = jnp.concatenate([-sin, sin, -sin, sin], axis=1)
    return cos128, sin128


def _block_tril(n, blk):
    r = np.arange(n)
    m = (r[:, None] >= r[None, :]) & ((r[:, None] // blk) == (r[None, :] // blk))
    return jnp.asarray(m.astype(np.float32), dtype=BF)


def kernel(x_prompt, x_sample, cache_a_k, cache_a_v, state_pool, state_gla, norm_g, w_in, q_norm_g, k_norm_g,
           sinks, w_pool, pool_scale, w_gate_lr, b_gate, gla_norm_g, w_branch, w_out):
    w_lr = jnp.pad(w_in[:, :, A_COLS:A_COLS + C_GATE_RANK],
                   ((0, 0), (0, 0), (0, LANES - C_GATE_RANK))).astype(BF)
    w_g = jnp.pad(w_gate_lr, ((0, 0), (0, LANES - C_GATE_RANK), (0, 0))).astype(BF)
    w_branch_bf = w_branch.astype(BF)
    w_out_bf = w_out.astype(BF)
    w_pool_bf = w_pool.astype(BF)
    qg = jnp.tile(q_norm_g, (1, 256 // A_HEAD_DIM))[:, None, :]
    kg = jnp.tile(k_norm_g, (1, 256 // A_HEAD_DIM))[:, None, :]
    norm_g = norm_g[:, None, :]
    b_gate = b_gate[:, None, :]
    pool_scale = pool_scale[:, None, :]
    gla_norm_g = gla_norm_g[:, None, :]
    cos_p, sin_p = _rope_tables(jnp.arange(SEQ))
    cos_s, sin_s = _rope_tables(PAST_LEN + jnp.arange(DEC_SEQ))
    cos_s = jnp.tile(cos_s, (SMP_BS, 1))
    sin_s = jnp.tile(sin_s, (SMP_BS, 1))
    lane = np.arange(256)
    bd = jnp.asarray((lane[:, None] // A_HEAD_DIM == lane[None, :] // A_HEAD_DIM).astype(np.float32), dtype=BF)
    bt_s = _block_tril(SMP_GB * DEC_SEQ, DEC_SEQ)
    ones_c = jnp.ones((GLA_C, 128), BF)
    cache_k = cache_a_k.reshape(DEPTH, DEC_BATCH, WINDOW, KV_W)
    cache_v = cache_a_v.reshape(DEPTH, DEC_BATCH, WINDOW, KV_W)
    pool_prev = jnp.pad(state_pool, ((0, 0), (0, 0), (1, 0), (0, 0)))

    w_t = jnp.swapaxes(w_in, 1, 2)
    x = (x_prompt.reshape(T_P, D_MODEL), x_sample.reshape(T_S, D_MODEL))
    h, loga = _norm(x[0], x[1], 0, norm_g, w_lr, w_g, b_gate)
    pk, pv, pp, sp = [], [], [], []
    nk_buf = nv_buf = sp_buf = ss_buf = None
    for l in range(DEPTH):
        proj_a = _inproj_a(h, w_t, l)
        proj_b = _inproj_b(h, w_t, l)

        oa, knew_p = _attn_prompt(proj_a, l, sinks, cos_p, sin_p, bd, qg, kg)
        oa, nk_buf, nv_buf = _attn_sample(oa, nk_buf, nv_buf, proj_a, l, cache_k, cache_v, sinks,
                                          cos_s, sin_s, bd, qg, kg)
        ob, tail_p = _pool_prompt(proj_a, l, w_pool_bf, pool_scale)
        ob, tail_s = _pool_sample(ob, proj_a, l, pool_prev, w_pool_bf, pool_scale)
        oc, sp_buf = _gla_prompt(sp_buf, proj_a, proj_b, loga, l, gla_norm_g)
        oc, ss_buf = _gla_sample(oc, ss_buf, proj_a, proj_b, loga, l, state_gla, bt_s, ones_c, gla_norm_g)

        outs = _merge_out(oa, ob, oc, proj_b, x, l, w_branch_bf, w_out_bf, norm_g, w_lr, w_g, b_gate)
        if l + 1 < DEPTH:
            x, h, loga = outs
        else:
            yp, ys = outs

        vp = proj_a[:T_P, A_AV:A_AV + KV_W].reshape(BATCH, SEQ, KV_W)[:, SEQ - WINDOW:].astype(F32)
        pk.append(knew_p)
        pv.append(vp)
        pp.append(tail_p[:, 1:])
        sp.append(tail_s[:, 1:])

    kv5 = lambda a, nb: a.reshape(DEPTH, nb, WINDOW, A_KV_HEADS, A_HEAD_DIM)
    yp = yp.reshape(BATCH, SEQ, D_MODEL)
    ys = ys.reshape(DEC_BATCH, DEC_SEQ, D_MODEL)
    return (yp, ys, kv5(jnp.stack(pk), BATCH), kv5(jnp.stack(pv), BATCH), jnp.stack(pp), sp_buf,
            kv5(nk_buf, DEC_BATCH), kv5(nv_buf, DEC_BATCH), jnp.stack(sp), ss_buf)
```

```python
import numpy as np
import jax
import jax.numpy as jnp
from jax import lax
from jax.experimental import pallas as pl
from jax.experimental.pallas import tpu as pltpu

F32 = jnp.float32
BF = jnp.bfloat16

D_MODEL = 2048
BATCH = 4
SEQ = 2048
DEPTH = 4
DEC_BATCH = 32
DEC_SEQ = 8
PAST_LEN = 16384
BRANCH = D_MODEL // 2
N_BRANCH = 3
A_HEADS = 16
A_KV_HEADS = 4
A_HEAD_DIM = 64
A_GROUP = A_HEADS // A_KV_HEADS
WINDOW = 128
ROPE_THETA = 10000.0
POOL_WINDOWS = (2, 4, 8, 16)
POOL_GROUP_DIM = 256
POOL_HIST = 15
C_HEADS = 4
C_KEY = 512
C_DK = 128
C_DV = 256
C_GATE_RANK = 16
C_GATE_TAU = 16.0
EPS = 1e-6
KV_W = A_KV_HEADS * A_HEAD_DIM
IN_COLS = 13840

T_P = BATCH * SEQ
T_S = DEC_BATCH * DEC_SEQ
T_ALL = T_P + T_S

A_AQ, A_AK, A_AV, A_AZ = 0, 1024, 1280, 1536
A_BU, A_BZ = 2560, 3584
A_CQ, A_CK, A_CV = 4608, 5120, 5632
A_COLS = 6656
LR_SHIFT = C_GATE_RANK
B_COLS = IN_COLS - A_COLS - LR_SHIFT
B_G, B_CZ = 0, N_BRANCH * D_MODEL

LANES = 128
TM = 1056
TN_A = 1664
TN_B = 1024
W_SUB = 512
CAST_ROWS = 128
TM_OUT = 256
N_P_TILES = T_P // TM_OUT
N_TILES = T_ALL // TM_OUT
ATT_BLK = WINDOW
ATT_NB = 1
POOL_TP = 512
POOL_HALO = 16
GLA_C = 64
GLA_NSUB = 8
GLA_SC = GLA_C * GLA_NSUB
GLA_HPS = 4
GLA_VW = 512
GLA_VBLKS = GLA_HPS * C_DV // GLA_VW
SMP_BS = 4
SMP_GB = 8
VMEM_LIMIT = 56 * 1024 * 1024
NEG_BIG = -1e30


def _cparams(n_axes):
    return pltpu.CompilerParams(
        dimension_semantics=("arbitrary",) * n_axes, vmem_limit_bytes=VMEM_LIMIT)


def _sigmoid(x):
    return 0.5 * jnp.tanh(0.5 * x) + 0.5


def _silu(x):
    return x * _sigmoid(x)


def _dot(a, b):
    return jnp.dot(a, b, preferred_element_type=F32)


def _dot_nt(a, b):
    return lax.dot_general(a, b, (((1,), (1,)), ((), ())), preferred_element_type=F32)


def _dot_tn(a, b):
    return lax.dot_general(a, b, (((0,), (0,)), ((), ())), preferred_element_type=F32)


def _split3(x):
    hi = x.astype(BF)
    r1 = x - hi.astype(F32)
    mid = r1.astype(BF)
    lo = (r1 - mid.astype(F32)).astype(BF)
    return hi, mid, lo


def _any_spec():
    return pl.BlockSpec(memory_space=pl.ANY)


def _row_spec(l, width):
    return pl.BlockSpec((None, 1, width), lambda *a: (l, 0, 0))


def _norm_gate(xf, g_ref, wlr_ref, wg_ref, bg_ref):
    ms = jnp.mean(xf * xf, axis=-1, keepdims=True)
    h = (xf * lax.rsqrt(ms + EPS) * g_ref[...]).astype(BF)
    lr = _dot(h, wlr_ref[...])
    z = _dot(lr.astype(BF), wg_ref[...]) + bg_ref[...]
    log_sig = jnp.minimum(z, 0.0) - jnp.log1p(jnp.exp(-jnp.abs(z)))
    return h, log_sig * (1.0 / C_GATE_TAU)


def _norm_specs(l):
    return [
        _row_spec(l, D_MODEL),
        pl.BlockSpec((None, D_MODEL, LANES), lambda *a: (l, 0, 0)),
        pl.BlockSpec((None, LANES, C_KEY), lambda *a: (l, 0, 0)),
        _row_spec(l, C_KEY),
    ]


def _split_rows_specs(width):
    return [pl.BlockSpec((TM_OUT, width), lambda i: (jnp.minimum(i, N_P_TILES - 1), 0)),
            pl.BlockSpec((TM_OUT, width), lambda i: (0, 0))]


def _pick_rows(xp_ref, xs_ref):
    return jnp.where(pl.program_id(0) < N_P_TILES, xp_ref[...], xs_ref[...])


def _norm_kernel(xp_ref, xs_ref, g_ref, wlr_ref, wg_ref, bg_ref, h_ref, loga_ref):
    h, loga = _norm_gate(_pick_rows(xp_ref, xs_ref), g_ref, wlr_ref, wg_ref, bg_ref)
    h_ref[...] = h
    loga_ref[...] = loga


def _norm(xp, xs, l, norm_g, w_lr, w_g, b_gate):
    return pl.pallas_call(
        _norm_kernel,
        grid=(N_TILES,),
        in_specs=_split_rows_specs(D_MODEL) + _norm_specs(l),
        out_specs=[pl.BlockSpec((TM_OUT, D_MODEL), lambda i: (i, 0)),
                   pl.BlockSpec((TM_OUT, C_KEY), lambda i: (i, 0))],
        out_shape=[jax.ShapeDtypeStruct((T_ALL, D_MODEL), BF),
                   jax.ShapeDtypeStruct((T_ALL, C_KEY), F32)],
        compiler_params=_cparams(1),
    )(xp, xs, norm_g, w_lr, w_g, b_gate)


def _inproj_a_kernel(h_ref, w_ref, o_ref, w_scr):
    @pl.when(pl.program_id(1) == 0)
    def _():
        for r in range(0, TN_A, CAST_ROWS):
            w_scr[r:r + CAST_ROWS, :] = w_ref[r:r + CAST_ROWS, :].astype(BF)

    o_ref[...] = _dot_nt(h_ref[...], w_scr[...]).astype(o_ref.dtype)


def _inproj_a(h, w_t, l):
    return pl.pallas_call(
        _inproj_a_kernel,
        grid=(A_COLS // TN_A, T_ALL // TM),
        in_specs=[
            pl.BlockSpec((TM, D_MODEL), lambda j, i: (i, 0)),
            pl.BlockSpec((None, TN_A, D_MODEL), lambda j, i: (l, j, 0)),
        ],
        out_specs=pl.BlockSpec((TM, TN_A), lambda j, i: (i, j)),
        out_shape=jax.ShapeDtypeStruct((T_ALL, A_COLS), BF),
        scratch_shapes=[pltpu.VMEM((TN_A, D_MODEL), BF)],
        compiler_params=_cparams(2),
    )(h, w_t)


def _copy_cast_rows(dst, dst0, src, src0, n):
    for r in range(0, n, CAST_ROWS):
        m = min(CAST_ROWS, n - r)
        dst[dst0 + r:dst0 + r + m, :] = src[src0 + r:src0 + r + m, :].astype(BF)


def _inproj_b_kernel(h_ref, w0_ref, w1_ref, w2_ref, o_ref, w_scr):
    @pl.when(pl.program_id(1) == 0)
    def _():
        _copy_cast_rows(w_scr, 0, w0_ref, LR_SHIFT, W_SUB - LR_SHIFT)
        _copy_cast_rows(w_scr, W_SUB - LR_SHIFT, w1_ref, 0, W_SUB)
        _copy_cast_rows(w_scr, 2 * W_SUB - LR_SHIFT, w2_ref, 0, LR_SHIFT)

    o_ref[...] = _dot_nt(h_ref[...], w_scr[...]).astype(o_ref.dtype)


def _inproj_b(h, w_t, l):
    assert TN_B == 2 * W_SUB
    nt = B_COLS // TN_B
    sub0 = A_COLS // W_SUB
    tail0 = (A_COLS + TN_B) // LR_SHIFT
    return pl.pallas_call(
        _inproj_b_kernel,
        grid=(nt, T_ALL // TM),
        in_specs=[
            pl.BlockSpec((TM, D_MODEL), lambda j, i: (i, 0)),
            pl.BlockSpec((None, W_SUB, D_MODEL), lambda j, i: (l, sub0 + 2 * j, 0)),
            pl.BlockSpec((None, W_SUB, D_MODEL), lambda j, i: (l, sub0 + 2 * j + 1, 0)),
            pl.BlockSpec((None, LR_SHIFT, D_MODEL), lambda j, i: (l, tail0 + (TN_B // LR_SHIFT) * j, 0)),
        ],
        out_specs=pl.BlockSpec((TM, TN_B), lambda j, i: (i, (j + nt - 1) % nt)),
        out_shape=jax.ShapeDtypeStruct((T_ALL, B_COLS), BF),
        scratch_shapes=[pltpu.VMEM((TN_B, D_MODEL), BF)],
        compiler_params=_cparams(2),
    )(h, w_t, w_t, w_t)


def _head_norm(x, bd, g_row):
    outs = []
    for c in range(x.shape[1] // 256):
        xc = x[:, c * 256:(c + 1) * 256]
        ss = _dot((xc * xc).astype(BF), bd)
        outs.append(xc * lax.rsqrt(ss * (1.0 / A_HEAD_DIM) + EPS) * g_row)
    return outs


def _rope128(xc, cos, sin, first_half):
    swapped = jnp.where(first_half, pltpu.roll(xc, 96, 1), pltpu.roll(xc, 32, 1))
    return xc * cos + swapped * sin


def _norm_rope(x, bd, g_row, cos, sin, first_half):
    chunks = []
    for blk in _head_norm(x, bd, g_row):
        for c in range(2):
            chunks.append(_rope128(blk[:, c * 128:(c + 1) * 128], cos, sin, first_half))
    return chunks


def _dup_half(chunk, rolled, lo_half, use_low):
    return jnp.where(lo_half, chunk, rolled) if use_low else jnp.where(lo_half, rolled, chunk)


def _attn_prompt_kernel(l, sinks_ref, q_ref, z0_ref, z1_ref, k_ref, v_ref, cos_ref, sin_ref, bd_ref, qg_ref,
                        kg_ref, o_ref, knew_ref, kprev, vprev_e, vprev_o):
    n = pl.program_id(1)
    R = ATT_BLK
    RR = ATT_NB * R
    lane_all = lax.broadcasted_iota(jnp.int32, (RR, 128), 1)
    first_half = (lane_all & 32) == 0
    lane = lax.broadcasted_iota(jnp.int32, (R, 128), 1)
    lo_half = lane < 64
    hi_half = jnp.logical_not(lo_half)
    cos = cos_ref[...]
    sin = sin_ref[...]
    bd = bd_ref[...]

    @pl.when(n == 0)
    def _():
        kprev[...] = jnp.zeros_like(kprev)
        vprev_e[...] = jnp.zeros_like(vprev_e)
        vprev_o[...] = jnp.zeros_like(vprev_o)

    kch_all = _norm_rope(k_ref[...].astype(F32), bd, kg_ref[...], cos, sin, first_half)
    knew_ref[0] = jnp.concatenate([kc[RR - R:] for kc in kch_all], axis=1)
    vraw = v_ref[...].astype(F32)
    vch_all = [vraw[:, :128], vraw[:, 128:]]
    keys = [[kprev[h] for h in range(A_KV_HEADS)]]
    vals_e = [[vprev_e[h] for h in range(A_KV_HEADS)]]
    vals_o = [[vprev_o[h] for h in range(A_KV_HEADS)]]
    for blk in range(ATT_NB):
        rows = slice(blk * R, (blk + 1) * R)
        kcur, vcur_e, vcur_o = [], [], []
        for h in range(A_KV_HEADS):
            c, low = h // 2, (h % 2 == 0)
            kc, vc = kch_all[c][rows], vch_all[c][rows]
            kcur.append(_dup_half(kc, pltpu.roll(kc, 64, 1), lo_half, low).astype(BF))
            v_lo = vc if low else pltpu.roll(vc, 64, 1)
            v_hi = pltpu.roll(vc, 64, 1) if low else vc
            vcur_e.append(jnp.where(lo_half, v_lo, 1.0).astype(BF))
            vcur_o.append(jnp.where(lo_half, 1.0, v_hi).astype(BF))
        keys.append(kcur)
        vals_e.append(vcur_e)
        vals_o.append(vcur_o)

    qch_all = _norm_rope(q_ref[...].astype(F32), bd, qg_ref[...], cos, sin, first_half)

    G = A_GROUP * R
    srow = lax.broadcasted_iota(jnp.int32, (G, 128), 0)
    scol = lax.broadcasted_iota(jnp.int32, (G, 128), 1)
    qi = srow & (R - 1)
    bias_prev_any = jnp.where(scol >= qi, 0.0, NEG_BIG)
    bias_prev_first = jnp.where(jnp.logical_and(scol >= qi, n > 0), 0.0, NEG_BIG)
    bias_cur = jnp.where(scol <= qi, 0.0, NEG_BIG)

    def scores(blk, h):
        rows = slice(blk * R, (blk + 1) * R)
        qa = qch_all[2 * h][rows]
        qb = qch_all[2 * h + 1][rows]
        q_stack = jnp.concatenate([jnp.where(lo_half, qa, 0.0), jnp.where(lo_half, qb, 0.0),
                                   jnp.where(hi_half, qa, 0.0), jnp.where(hi_half, qb, 0.0)],
                                  axis=0).astype(BF)
        bias_prev = bias_prev_first if blk == 0 else bias_prev_any
        return _dot_nt(q_stack, keys[blk][h]) + bias_prev, _dot_nt(q_stack, keys[blk + 1][h]) + bias_cur

    problems = [(blk, h) for blk in range(ATT_NB) for h in range(A_KV_HEADS)]
    ahead = scores(*problems[0])
    for idx, (blk, h) in enumerate(problems):
        s_p, s_c = ahead
        if idx + 1 < len(problems):
            ahead = scores(*problems[idx + 1])
        vprev_e_h, vcur_e_h = vals_e[blk][h], vals_e[blk + 1][h]
        vprev_o_h, vcur_o_h = vals_o[blk][h], vals_o[blk + 1][h]
        head_of_blk = (4 * h, 4 * h + 2, 4 * h + 1, 4 * h + 3)
        row_max = jnp.max(jnp.maximum(s_p, s_c), axis=-1, keepdims=True)
        m_blk = [jnp.maximum(row_max[bi * R:(bi + 1) * R], sinks_ref[l, hd]) for bi, hd in enumerate(head_of_blk)]
        e_sink = [jnp.exp(sinks_ref[l, hd] - m_blk[bi]) for bi, hd in enumerate(head_of_blk)]
        m = jnp.concatenate(m_blk, axis=0)
        p_p = jnp.exp(s_p - m).astype(BF)
        p_c = jnp.exp(s_c - m).astype(BF)
        half = G // 2
        pv_e = _dot(p_p[:half], vprev_e_h) + _dot(p_c[:half], vcur_e_h)
        pv_o = _dot(p_p[half:], vprev_o_h) + _dot(p_c[half:], vcur_o_h)
        out_rows = slice(blk * R, (blk + 1) * R)
        for cc in range(2):
            rows_e = slice(cc * R, (cc + 1) * R)
            ev, od = pv_e[rows_e], pv_o[rows_e]
            num = jnp.where(lo_half, ev, od)
            den = pltpu.roll(jnp.where(lo_half, od, ev), 64, 1)
            den = den + jnp.where(lo_half, e_sink[cc], e_sink[2 + cc])
            c = 2 * h + cc
            z_ref = z0_ref if c < 4 else z1_ref
            zc = z_ref[out_rows, (c % 4) * 128:(c % 4 + 1) * 128].astype(F32)
            o_ref[out_rows, c * 128:(c + 1) * 128] = (num / den * _silu(zc)).astype(o_ref.dtype)

    for h in range(A_KV_HEADS):
        kprev[h] = keys[ATT_NB][h]
        vprev_e[h] = vals_e[ATT_NB][h]
        vprev_o[h] = vals_o[ATT_NB][h]


def _attn_prompt(proj_a, l, sinks, cos, sin, bd, qg, kg):
    rows = ATT_NB * ATT_BLK
    nb = SEQ // rows
    rb = lambda b, n: b * nb + n
    kern = lambda *refs: _attn_prompt_kernel(l, *refs)
    return pl.pallas_call(
        kern,
        grid=(BATCH, nb),
        in_specs=[
            pl.BlockSpec(memory_space=pltpu.SMEM),
            pl.BlockSpec((rows, BRANCH), lambda b, n: (rb(b, n), A_AQ // BRANCH)),
            pl.BlockSpec((rows, 512), lambda b, n: (rb(b, n), A_AZ // 512)),
            pl.BlockSpec((rows, 512), lambda b, n: (rb(b, n), A_AZ // 512 + 1)),
            pl.BlockSpec((rows, KV_W), lambda b, n: (rb(b, n), A_AK // KV_W)),
            pl.BlockSpec((rows, KV_W), lambda b, n: (rb(b, n), A_AV // KV_W)),
            pl.BlockSpec((rows, 128), lambda b, n: (n, 0)),
            pl.BlockSpec((rows, 128), lambda b, n: (n, 0)),
            pl.BlockSpec((256, 256), lambda b, n: (0, 0)),
            _row_spec(l, 256),
            _row_spec(l, 256),
        ],
        out_specs=[
            pl.BlockSpec((rows, BRANCH), lambda b, n: (rb(b, n), 0)),
            pl.BlockSpec((1, ATT_BLK, KV_W), lambda b, n: (b, 0, 0)),
        ],
        out_shape=[
            jax.ShapeDtypeStruct((T_ALL, BRANCH), BF),
            jax.ShapeDtypeStruct((BATCH, WINDOW, KV_W), F32),
        ],
        scratch_shapes=[pltpu.VMEM((A_KV_HEADS, ATT_BLK, 128), BF),
                        pltpu.VMEM((A_KV_HEADS, ATT_BLK, 128), BF),
                        pltpu.VMEM((A_KV_HEADS, ATT_BLK, 128), BF)],
        compiler_params=_cparams(2),
    )(sinks, proj_a, proj_a, proj_a, proj_a, proj_a, cos, sin, bd, qg, kg)


def _attn_sample_kernel(l, n_alias, *refs):
    refs = refs[n_alias:]
    (sinks_ref, q_ref, z0_ref, z1_ref, k_ref, v_ref, ck_ref, cv_ref, cos_ref, sin_ref,
     bd_ref, qg_ref, kg_ref, o_ref, nk_ref, nv_ref, kall, vall) = refs
    L = DEC_SEQ
    R = SMP_BS * L
    lane = lax.broadcasted_iota(jnp.int32, (R, 128), 1)
    first_half = (lane & 32) == 0
    lane8 = lax.broadcasted_iota(jnp.int32, (L, 128), 1)
    lo8 = lane8 < 64
    lane_c = lax.broadcasted_iota(jnp.int32, (WINDOW, 128), 1)
    lo_c = lane_c < 64
    cos = cos_ref[...]
    sin = sin_ref[...]
    bd = bd_ref[...]

    @pl.when(pl.program_id(0) == 0)
    def _():
        kall[...] = jnp.zeros_like(kall)
        vall[...] = jnp.zeros_like(vall)

    kch = _norm_rope(k_ref[...].astype(F32), bd, kg_ref[...], cos, sin, first_half)
    vraw = v_ref[...].astype(F32)
    vch = [vraw[:, :128], vraw[:, 128:]]
    qch = _norm_rope(q_ref[...].astype(F32), bd, qg_ref[...], cos, sin, first_half)

    rows = A_GROUP * L
    srow = lax.broadcasted_iota(jnp.int32, (rows, 2 * WINDOW), 0)
    scol = lax.broadcasted_iota(jnp.int32, (rows, 2 * WINDOW), 1)
    t = srow & (L - 1)
    mask = jnp.logical_or(jnp.logical_and(scol < WINDOW, scol >= t),
                          jnp.logical_and(scol >= WINDOW, (scol - WINDOW) <= t))
    hrow = lax.broadcasted_iota(jnp.int32, (rows, 1), 0) >> (L.bit_length() - 1)
    z_all = jnp.concatenate([z0_ref[...], z1_ref[...]], axis=1).astype(F32)
    bias = jnp.where(mask, 0.0, NEG_BIG)
    pairs = [(bi, h) for bi in range(SMP_BS) for h in range(A_KV_HEADS)]
    slot = lambda bi, h: bi * A_KV_HEADS + h

    for bi in range(SMP_BS):
        r0 = bi * L
        knew = [kc[r0:r0 + L] for kc in kch]
        vnew = [vc[r0:r0 + L] for vc in vch]
        ck = ck_ref[bi]
        cv = cv_ref[bi]
        nk_ref[bi, 0:WINDOW - L] = ck[L:WINDOW]
        nv_ref[bi, 0:WINDOW - L] = cv[L:WINDOW]
        nk_ref[bi, WINDOW - L:WINDOW] = jnp.concatenate(knew, axis=1)
        nv_ref[bi, WINDOW - L:WINDOW] = jnp.concatenate(vnew, axis=1)
        for h in range(A_KV_HEADS):
            c, low = h // 2, (h % 2 == 0)
            ckc = ck[:, c * 128:(c + 1) * 128]
            cvc = cv[:, c * 128:(c + 1) * 128]
            kall[slot(bi, h), 0:WINDOW] = _dup_half(ckc, pltpu.roll(ckc, 64, 1), lo_c, low)
            vall[slot(bi, h), 0:WINDOW] = _dup_half(cvc, pltpu.roll(cvc, 64, 1), lo_c, low)
            kall[slot(bi, h), WINDOW:WINDOW + L] = _dup_half(knew[c], pltpu.roll(knew[c], 64, 1), lo8, low)
            vall[slot(bi, h), WINDOW:WINDOW + L] = _dup_half(vnew[c], pltpu.roll(vnew[c], 64, 1), lo8, low)

    scores = []
    for bi, h in pairs:
        r0 = bi * L
        qs = []
        for gi in range(A_GROUP):
            i = h * A_GROUP + gi
            sel = lo8 if i % 2 == 0 else jnp.logical_not(lo8)
            qs.append(jnp.where(sel, qch[i // 2][r0:r0 + L], 0.0))
        qm = jnp.concatenate(qs, axis=0).astype(BF)
        scores.append(_dot_nt(qm, kall[slot(bi, h)].astype(BF)) + bias)

    probs, dens = [], []
    for (bi, h), s in zip(pairs, scores):
        sink = jnp.zeros((rows, 1), F32)
        for gi in range(A_GROUP):
            sink = jnp.where(hrow == gi, sinks_ref[l, h * A_GROUP + gi], sink)
        m = jnp.maximum(jnp.max(s, axis=-1, keepdims=True), sink)
        p = jnp.exp(s - m)
        dens.append(jnp.sum(p, axis=-1, keepdims=True) + jnp.exp(sink - m))
        probs.append(p.astype(BF))

    pvs = [_dot(p, vall[slot(bi, h)].astype(BF)) / den for (bi, h), p, den in zip(pairs, probs, dens)]
    o_rows = []
    for bi in range(SMP_BS):
        o_chunks = []
        for h in range(A_KV_HEADS):
            pv = pvs[slot(bi, h)]
            for cc in range(2):
                o_chunks.append(jnp.where(lo8, pv[(2 * cc) * L:(2 * cc + 1) * L],
                                          pv[(2 * cc + 1) * L:(2 * cc + 2) * L]))
        o = jnp.concatenate(o_chunks, axis=1)
        o_rows.append(o * _silu(z_all[bi * L:(bi + 1) * L]))
    o_ref[...] = jnp.concatenate(o_rows, axis=0).astype(o_ref.dtype)


def _attn_sample(oa, nk_buf, nv_buf, proj_a, l, cache_k, cache_v, sinks, cos, sin, bd, qg, kg):
    R = SMP_BS * DEC_SEQ
    base = T_P // R
    alias_in = [oa] + ([nk_buf, nv_buf] if l > 0 else [])
    n_alias = len(alias_in)
    kern = lambda *refs: _attn_sample_kernel(l, n_alias, *refs)
    cache_spec = pl.BlockSpec((None, SMP_BS, WINDOW, KV_W), lambda b: (l, b, 0, 0))
    return pl.pallas_call(
        kern,
        grid=(DEC_BATCH // SMP_BS,),
        in_specs=[_any_spec()] * n_alias + [
            pl.BlockSpec(memory_space=pltpu.SMEM),
            pl.BlockSpec((R, BRANCH), lambda b: (base + b, A_AQ // BRANCH)),
            pl.BlockSpec((R, 512), lambda b: (base + b, A_AZ // 512)),
            pl.BlockSpec((R, 512), lambda b: (base + b, A_AZ // 512 + 1)),
            pl.BlockSpec((R, KV_W), lambda b: (base + b, A_AK // KV_W)),
            pl.BlockSpec((R, KV_W), lambda b: (base + b, A_AV // KV_W)),
            cache_spec, cache_spec,
            pl.BlockSpec((R, 128), lambda b: (0, 0)),
            pl.BlockSpec((R, 128), lambda b: (0, 0)),
            pl.BlockSpec((256, 256), lambda b: (0, 0)),
            _row_spec(l, 256),
            _row_spec(l, 256),
        ],
        out_specs=[pl.BlockSpec((R, BRANCH), lambda b: (base + b, 0)), cache_spec, cache_spec],
        out_shape=[
            jax.ShapeDtypeStruct((T_ALL, BRANCH), BF),
            jax.ShapeDtypeStruct((DEPTH, DEC_BATCH, WINDOW, KV_W), F32),
            jax.ShapeDtypeStruct((DEPTH, DEC_BATCH, WINDOW, KV_W), F32),
        ],
        scratch_shapes=[pltpu.VMEM((SMP_BS * A_KV_HEADS, 2 * WINDOW, 128), F32),
                        pltpu.VMEM((SMP_BS * A_KV_HEADS, 2 * WINDOW, 128), F32)],
        input_output_aliases={i: i for i in range(n_alias)},
        compiler_params=_cparams(1),
    )(*alias_in, sinks, proj_a, proj_a, proj_a, proj_a, proj_a, cache_k, cache_v, cos, sin, bd, qg, kg)


def _pool_compute(ext, u_rows, start_pos, wp_ref, scale_ref, z, n_rows):
    pos = start_pos + lax.broadcasted_iota(jnp.int32, (n_rows, 1), 0)
    outs = []
    for g, w in enumerate(POOL_WINDOWS):
        sl = slice(g * POOL_GROUP_DIM, (g + 1) * POOL_GROUP_DIM)
        acc = ext[:, sl]
        span = 1
        while span < w:
            acc = acc + pltpu.roll(acc, span, 0)
            span *= 2
        win_sum = acc[POOL_HALO:]
        count = jnp.minimum(w, pos + 1).astype(F32)
        d = (win_sum / count - u_rows[:, sl]).astype(BF)
        outs.append(_dot(d, wp_ref[g]))
    y = jnp.concatenate(outs, axis=1) * scale_ref[...]
    return y * _silu(z)


def _pool_prompt_kernel(u0_ref, u1_ref, z0_ref, z1_ref, wp_ref, scale_ref, o_ref, tail_ref, ext):
    t = pl.program_id(1)

    @pl.when(t == 0)
    def _():
        ext[0:POOL_HALO] = jnp.zeros((POOL_HALO, BRANCH), F32)

    u = jnp.concatenate([u0_ref[...], u1_ref[...]], axis=1).astype(F32)
    z = jnp.concatenate([z0_ref[...], z1_ref[...]], axis=1).astype(F32)
    ext[POOL_HALO:] = u
    out = _pool_compute(ext[...], u, t * POOL_TP, wp_ref, scale_ref, z, POOL_TP)
    o_ref[...] = out.astype(o_ref.dtype)
    tail = ext[POOL_TP:POOL_TP + POOL_HALO]
    tail_ref[0] = tail
    ext[0:POOL_HALO] = tail


def _pool_w_specs(l):
    ng = len(POOL_WINDOWS)
    return [pl.BlockSpec((None, ng, POOL_GROUP_DIM, POOL_GROUP_DIM), lambda *a: (l, 0, 0, 0)),
            _row_spec(l, BRANCH)]


def _pool_prompt(proj_a, l, w_pool, scale):
    nt = SEQ // POOL_TP
    half = lambda off, k: pl.BlockSpec((POOL_TP, 512), lambda b, t: (b * nt + t, off // 512 + k))
    return pl.pallas_call(
        _pool_prompt_kernel,
        grid=(BATCH, nt),
        in_specs=[half(A_BU, 0), half(A_BU, 1), half(A_BZ, 0), half(A_BZ, 1)] + _pool_w_specs(l),
        out_specs=[
            pl.BlockSpec((POOL_TP, BRANCH), lambda b, t: (b * nt + t, 0)),
            pl.BlockSpec((1, POOL_HALO, BRANCH), lambda b, t: (b, 0, 0)),
        ],
        out_shape=[
            jax.ShapeDtypeStruct((T_ALL, BRANCH), BF),
            jax.ShapeDtypeStruct((BATCH, POOL_HALO, BRANCH), F32),
        ],
        scratch_shapes=[pltpu.VMEM((POOL_HALO + POOL_TP, BRANCH), F32)],
        compiler_params=_cparams(2),
    )(proj_a, proj_a, proj_a, proj_a, w_pool, scale)


def _pool_sample_kernel(ob_in_ref, u0_ref, u1_ref, z0_ref, z1_ref, prev_ref, wp_ref, scale_ref, o_ref, tail_ref):
    del ob_in_ref
    L = DEC_SEQ
    u_all = jnp.concatenate([u0_ref[...], u1_ref[...]], axis=1).astype(F32)
    z_all = jnp.concatenate([z0_ref[...], z1_ref[...]], axis=1).astype(F32)
    outs = []
    for bi in range(SMP_BS):
        u = u_all[bi * L:(bi + 1) * L]
        ext = jnp.concatenate([prev_ref[bi], u], axis=0)
        outs.append(_pool_compute(ext, u, PAST_LEN, wp_ref, scale_ref, z_all[bi * L:(bi + 1) * L], L))
        tail_ref[bi] = ext[L:L + POOL_HALO]
    o_ref[...] = jnp.concatenate(outs, axis=0).astype(o_ref.dtype)


def _pool_sample(ob, proj_a, l, prev16, w_pool, scale):
    R = SMP_BS * DEC_SEQ
    base = T_P // R
    half = lambda off, k: pl.BlockSpec((R, 512), lambda b: (base + b, off // 512 + k))
    return pl.pallas_call(
        _pool_sample_kernel,
        grid=(DEC_BATCH // SMP_BS,),
        in_specs=[_any_spec(), half(A_BU, 0), half(A_BU, 1), half(A_BZ, 0), half(A_BZ, 1),
                  pl.BlockSpec((None, SMP_BS, POOL_HALO, BRANCH), lambda b: (l, b, 0, 0))] + _pool_w_specs(l),
        out_specs=[
            pl.BlockSpec((R, BRANCH), lambda b: (base + b, 0)),
            pl.BlockSpec((SMP_BS, POOL_HALO, BRANCH), lambda b: (b, 0, 0)),
        ],
        out_shape=[
            jax.ShapeDtypeStruct((T_ALL, BRANCH), BF),
            jax.ShapeDtypeStruct((DEC_BATCH, POOL_HALO, BRANCH), F32),
        ],
        input_output_aliases={0: 0},
        compiler_params=_cparams(1),
    )(ob, proj_a, proj_a, proj_a, proj_a, prev16, w_pool, scale)


def _chunk_cumsum(x, chunk):
    pos = lax.broadcasted_iota(jnp.int32, x.shape, 0) & (chunk - 1)
    span = 1
    while span < chunk:
        x = x + jnp.where(pos >= span, pltpu.roll(x, span, 0), 0.0)
        span *= 2
    return x


def _chunk_row(x, chunk, r):
    n = x.shape[0] // chunk
    return jnp.concatenate(
        [jnp.broadcast_to(x[c * chunk + r:c * chunk + r + 1], (chunk, x.shape[1])) for c in range(n)], axis=0)


def _gla_prompt_kernel(n_alias, *refs):
    refs = refs[n_alias:]
    q_ref, k_ref = refs[:2]
    v_refs = refs[2:2 + GLA_VBLKS]
    z_ref, la_ref, g_ref, o_ref, snew_ref, st_scr = refs[2 + GLA_VBLKS:]
    step = pl.program_id(2)

    @pl.when(step == 0)
    def _():
        st_scr[...] = jnp.zeros_like(st_scr)

    C = GLA_C
    chunks = [slice(c * C, (c + 1) * C) for c in range(GLA_NSUB)]
    row = lax.broadcasted_iota(jnp.int32, (GLA_SC, GLA_SC), 0)
    col = lax.broadcasted_iota(jnp.int32, (GLA_SC, GLA_SC), 1)
    visible = (row - col).astype(jnp.uint32) <= (row & (C - 1)).astype(jnp.uint32)
    g_row = g_ref[...]

    finals = []
    for hh in range(GLA_HPS):
        kcols = slice(hh * C_DK, (hh + 1) * C_DK)
        vcols = slice(hh * C_DV, (hh + 1) * C_DV)
        b = _chunk_cumsum(la_ref[:, kcols], C)
        b_last = _chunk_row(b, C, C - 1)
        b_mid = _chunk_row(b, C, C // 2 - 1)
        q = q_ref[:, kcols].astype(F32) * (C_DK ** -0.5)
        k = k_ref[:, kcols].astype(F32)
        v0 = (hh * C_DV) % GLA_VW
        v = v_refs[hh * C_DV // GLA_VW][:, v0:v0 + C_DV].astype(BF)
        kd = (k * jnp.exp(b_last - b)).astype(BF)
        qe = (q * jnp.exp(b)).astype(BF)
        q2 = (q * jnp.exp(b - b_mid)).astype(BF)
        k2 = (k * jnp.exp(b_mid - b)).astype(BF)

        att = jnp.where(visible, _dot_nt(q2, k2), 0.0).astype(BF)
        o = _dot(att, v)

        ut = [_dot_tn(v[sl], kd[sl]) for sl in chunks]
        st = st_scr[hh]
        st_before = []
        for c, sl in enumerate(chunks):
            st_before.append(st.astype(BF))
            st = st * jnp.exp(b[(c + 1) * C - 1:(c + 1) * C]) + ut[c]
        st_scr[hh] = st
        finals.append(st)

        o = o + jnp.concatenate([_dot_nt(qe[sl], st_before[c]) for c, sl in enumerate(chunks)], axis=0)
        ms = jnp.mean(o * o, axis=-1, keepdims=True)
        on = o * lax.rsqrt(ms + EPS) * g_row
        o_ref[:, vcols] = (on * _silu(z_ref[:, vcols].astype(F32))).astype(o_ref.dtype)

    @pl.when(step == pl.num_programs(2) - 1)
    def _():
        for hh in range(GLA_HPS):
            snew_ref[0, hh] = finals[hh].T


def _gla_prompt(s_buf, proj_a, proj_b, loga, l, g):
    ns = SEQ // GLA_SC
    rb = lambda b, h, s: b * ns + s
    alias_in = [s_buf] if l > 0 else []
    n_alias = len(alias_in)
    kern = lambda *refs: _gla_prompt_kernel(n_alias, *refs)
    kw, vw = GLA_HPS * C_DK, GLA_HPS * C_DV

    def cols(width, offset, stride=1, extra=0):
        assert offset % width == 0
        return pl.BlockSpec((GLA_SC, width), lambda b, h, s: (rb(b, h, s), offset // width + stride * h + extra))

    return pl.pallas_call(
        kern,
        grid=(BATCH, C_HEADS // GLA_HPS, ns),
        in_specs=[_any_spec()] * n_alias + [cols(kw, A_CQ), cols(kw, A_CK)]
        + [cols(GLA_VW, A_CV, GLA_VBLKS, j) for j in range(GLA_VBLKS)]
        + [cols(vw, B_CZ), cols(kw, 0), _row_spec(l, C_DV)],
        out_specs=[
            pl.BlockSpec((GLA_SC, vw), lambda b, h, s: (rb(b, h, s), h)),
            pl.BlockSpec((None, 1, GLA_HPS, C_DK, C_DV), lambda b, h, s: (l, b, h, 0, 0)),
        ],
        out_shape=[
            jax.ShapeDtypeStruct((T_ALL, BRANCH), BF),
            jax.ShapeDtypeStruct((DEPTH, BATCH, C_HEADS, C_DK, C_DV), F32),
        ],
        scratch_shapes=[pltpu.VMEM((GLA_HPS, C_DV, C_DK), F32)],
        input_output_aliases={0: 1} if l > 0 else {},
        compiler_params=_cparams(3),
    )(*alias_in, *([proj_a] * (2 + GLA_VBLKS)), proj_b, loga, g)


def _gla_sample_kernel(n_alias, *refs):
    refs = refs[n_alias:]
    q_ref, k_ref = refs[:2]
    v_refs = refs[2:2 + GLA_VBLKS]
    z_ref, la_ref, s0_ref, ones_ref, g_ref, o_ref, snew_ref = refs[2 + GLA_VBLKS:]
    C = SMP_GB * DEC_SEQ
    L = DEC_SEQ
    ones = ones_ref[...]
    g_row = g_ref[...]
    row = lax.broadcasted_iota(jnp.int32, (C, C), 0)
    col = lax.broadcasted_iota(jnp.int32, (C, C), 1)
    visible = (row - col).astype(jnp.uint32) <= (row & (L - 1)).astype(jnp.uint32)
    rowk = lax.broadcasted_iota(jnp.int32, (C, C_DK), 0) >> (L.bit_length() - 1)

    for hh in range(GLA_HPS):
        kcols = slice(hh * C_DK, (hh + 1) * C_DK)
        vcols = slice(hh * C_DV, (hh + 1) * C_DV)
        v0 = (hh * C_DV) % GLA_VW
        la = la_ref[:, kcols]
        b = _chunk_cumsum(la, L)
        b_last = _chunk_row(b, L, L - 1)
        b_mid = _chunk_row(b, L, L // 2 - 1)
        q = q_ref[:, kcols].astype(F32) * (C_DK ** -0.5)
        k = k_ref[:, kcols].astype(F32)
        v = v_refs[hh * C_DV // GLA_VW][:, v0:v0 + C_DV].astype(BF)
        kd = k * jnp.exp(b_last - b)
        qe = q * jnp.exp(b)

        o_inter = jnp.zeros((C, C_DV), F32)
        for gi in range(SMP_GB):
            own = rowk == gi
            s0 = s0_ref[gi, hh]
            o_inter = o_inter + _dot(jnp.where(own, qe, 0.0).astype(BF), s0.astype(BF))
            u = _dot_tn(jnp.where(own, kd, 0.0).astype(BF), v)
            own_parts = _split3(jnp.where(own, la, 0.0))
            dsum = _dot_tn(own_parts[0], ones) + _dot_tn(own_parts[1], ones) + _dot_tn(own_parts[2], ones)
            d = jnp.exp(dsum)
            snew_ref[gi, hh] = jnp.concatenate([d, d], axis=1) * s0 + u

        q2 = (q * jnp.exp(b - b_mid)).astype(BF)
        k2 = (k * jnp.exp(b_mid - b)).astype(BF)
        att = jnp.where(visible, _dot_nt(q2, k2), 0.0).astype(BF)
        o = o_inter + _dot(att, v)
        ms = jnp.mean(o * o, axis=-1, keepdims=True)
        on = o * lax.rsqrt(ms + EPS) * g_row
        o_ref[:, vcols] = (on * _silu(z_ref[:, vcols].astype(F32))).astype(o_ref.dtype)


def _gla_sample(oc, s_buf, proj_a, proj_b, loga, l, s0, ones, g):
    C = SMP_GB * DEC_SEQ
    base = T_P // C
    alias_in = [oc] + ([s_buf] if l > 0 else [])
    n_alias = len(alias_in)
    kern = lambda *refs: _gla_sample_kernel(n_alias, *refs)
    kw, vw = GLA_HPS * C_DK, GLA_HPS * C_DV
    state_spec = pl.BlockSpec((None, SMP_GB, GLA_HPS, C_DK, C_DV), lambda b, h: (l, b, h, 0, 0))

    def cols(width, offset, stride=1, extra=0):
        assert offset % width == 0
        return pl.BlockSpec((C, width), lambda b, h: (base + b, offset // width + stride * h + extra))

    return pl.pallas_call(
        kern,
        grid=(DEC_BATCH // SMP_GB, C_HEADS // GLA_HPS),
        in_specs=[_any_spec()] * n_alias + [cols(kw, A_CQ), cols(kw, A_CK)]
        + [cols(GLA_VW, A_CV, GLA_VBLKS, j) for j in range(GLA_VBLKS)]
        + [cols(vw, B_CZ), cols(kw, 0), state_spec,
           pl.BlockSpec((C, 128), lambda b, h: (0, 0)), _row_spec(l, C_DV)],
        out_specs=[cols(vw, 0), state_spec],
        out_shape=[
            jax.ShapeDtypeStruct((T_ALL, BRANCH), BF),
            jax.ShapeDtypeStruct((DEPTH, DEC_BATCH, C_HEADS, C_DK, C_DV), F32),
        ],
        input_output_aliases={i: i for i in range(n_alias)},
        compiler_params=_cparams(2),
    )(*alias_in, *([proj_a] * (2 + GLA_VBLKS)), proj_b, loga, s0, ones, g)


def _merged_residual(ba_ref, bb_ref, bc_ref, g0_ref, g1_ref, g2_ref, x, wb_ref, wo_ref):
    acc = _sigmoid(g0_ref[...].astype(F32)) * _dot(ba_ref[...], wb_ref[0])
    acc = acc + _sigmoid(g1_ref[...].astype(F32)) * _dot(bb_ref[...], wb_ref[1])
    acc = acc + _sigmoid(g2_ref[...].astype(F32)) * _dot(bc_ref[...], wb_ref[2])
    return x + _dot(acc.astype(BF), wo_ref[...])


def _merge_last_kernel(ba_ref, bb_ref, bc_ref, g0_ref, g1_ref, g2_ref, x_ref, wb_ref, wo_ref, yp_ref, ys_ref):
    y = _merged_residual(ba_ref, bb_ref, bc_ref, g0_ref, g1_ref, g2_ref, x_ref[...], wb_ref, wo_ref)
    i = pl.program_id(0)

    @pl.when(i < N_P_TILES)
    def _():
        yp_ref[...] = y

    @pl.when(i == N_P_TILES)
    def _():
        ys_ref[...] = y


def _merge_next_kernel(first, *refs):
    ba_ref, bb_ref, bc_ref, g0_ref, g1_ref, g2_ref = refs[:6]
    refs = refs[6:]
    if first:
        x = _pick_rows(refs[0], refs[1])
        refs = refs[2:]
    else:
        x = refs[0][...]
        refs = refs[1:]
    (wb_ref, wo_ref, g_ref, wlr_ref, wg_ref, bg_ref, wb_next_ref, wo_next_ref,
     y_ref, h_ref, loga_ref, wb_bf_ref, wo_bf_ref) = refs
    wb_bf_ref[...] = wb_next_ref[...].astype(BF)
    wo_bf_ref[...] = wo_next_ref[...].astype(BF)
    y = _merged_residual(ba_ref, bb_ref, bc_ref, g0_ref, g1_ref, g2_ref, x, wb_ref, wo_ref)
    y_ref[...] = y
    h, loga = _norm_gate(y, g_ref, wlr_ref, wg_ref, bg_ref)
    h_ref[...] = h
    loga_ref[...] = loga


def _merge_out(ba, bb, bc, proj_b, x, l, wb_bf, wo_bf, w_branch, w_out, norm_g, w_lr, w_g, b_gate):
    first, last = l == 0, l + 1 == DEPTH
    resident = pl.Buffered(1)
    w_specs = [
        pl.BlockSpec((N_BRANCH, BRANCH, D_MODEL), lambda i: (0, 0, 0), pipeline_mode=resident),
        pl.BlockSpec((D_MODEL, D_MODEL), lambda i: (0, 0), pipeline_mode=resident),
    ]
    row = lambda w: pl.BlockSpec((TM_OUT, w), lambda i: (i, 0))
    gate = lambda n: pl.BlockSpec((TM_OUT, D_MODEL), lambda i: (i, B_G // D_MODEL + n))
    mix_specs = [row(BRANCH)] * N_BRANCH + [gate(n) for n in range(N_BRANCH)]
    if last:
        return pl.pallas_call(
            _merge_last_kernel,
            grid=(N_TILES,),
            in_specs=mix_specs + [row(D_MODEL)] + w_specs,
            out_specs=_split_rows_specs(D_MODEL),
            out_shape=[jax.ShapeDtypeStruct((T_P, D_MODEL), F32), jax.ShapeDtypeStruct((T_S, D_MODEL), F32)],
            compiler_params=_cparams(1),
        )(ba, bb, bc, proj_b, proj_b, proj_b, x, wb_bf, wo_bf)

    wb_rows = N_BRANCH * BRANCH // N_P_TILES
    wo_rows = D_MODEL // N_P_TILES
    piece = lambda i: jnp.minimum(i, N_P_TILES - 1)
    x_specs = _split_rows_specs(D_MODEL) if first else [row(D_MODEL)]
    kern = lambda *refs: _merge_next_kernel(first, *refs)
    outs = pl.pallas_call(
        kern,
        grid=(N_TILES,),
        in_specs=mix_specs + x_specs + w_specs + _norm_specs(l + 1) + [
            pl.BlockSpec((None, wb_rows, D_MODEL), lambda i: (l + 1, piece(i), 0)),
            pl.BlockSpec((None, wo_rows, D_MODEL), lambda i: (l + 1, piece(i), 0)),
        ],
        out_specs=[row(D_MODEL), row(D_MODEL), row(C_KEY),
                   pl.BlockSpec((wb_rows, D_MODEL), lambda i: (piece(i), 0)),
                   pl.BlockSpec((wo_rows, D_MODEL), lambda i: (piece(i), 0))],
        out_shape=[jax.ShapeDtypeStruct((T_ALL, D_MODEL), F32), jax.ShapeDtypeStruct((T_ALL, D_MODEL), BF),
                   jax.ShapeDtypeStruct((T_ALL, C_KEY), F32),
                   jax.ShapeDtypeStruct((N_BRANCH * BRANCH, D_MODEL), BF),
                   jax.ShapeDtypeStruct((D_MODEL, D_MODEL), BF)],
        compiler_params=_cparams(1),
    )(ba, bb, bc, proj_b, proj_b, proj_b, *(list(x) if first else [x]), wb_bf, wo_bf,
      norm_g, w_lr, w_g, b_gate, w_branch.reshape(DEPTH, N_BRANCH * BRANCH, D_MODEL), w_out)
    y, h, loga, wb_next, wo_next = outs
    return y, h, loga, wb_next.reshape(N_BRANCH, BRANCH, D_MODEL), wo_next


def _rope_tables(pos):
    half = A_HEAD_DIM // 2
    inv = jnp.power(ROPE_THETA, -jnp.arange(half, dtype=F32) * (2.0 / A_HEAD_DIM))
    ang = pos.astype(F32)[:, None] * inv[None, :]
    cos, sin = jnp.cos(ang), jnp.sin(ang)
    cos128 = jnp.concatenate([cos, cos, cos, cos], axis=1)
    sin128 = jnp.concatenate([-sin, sin, -sin, sin], axis=1)
    return cos128, sin128


def kernel(x_prompt, x_sample, cache_a_k, cache_a_v, state_pool, state_gla, norm_g, w_in, q_norm_g, k_norm_g,
           sinks, w_pool, pool_scale, w_gate_lr, b_gate, gla_norm_g, w_branch, w_out):
    w_lr = jnp.pad(w_in[:, :, A_COLS:A_COLS + C_GATE_RANK],
                   ((0, 0), (0, 0), (0, LANES - C_GATE_RANK))).astype(BF)
    w_g = jnp.pad(w_gate_lr, ((0, 0), (0, LANES - C_GATE_RANK), (0, 0))).astype(BF)
    wb_bf = w_branch[0].astype(BF)
    wo_bf = w_out[0].astype(BF)
    w_pool_bf = w_pool.astype(BF)
    qg = (jnp.tile(q_norm_g, (1, 256 // A_HEAD_DIM)) * (A_HEAD_DIM ** -0.5))[:, None, :]
    kg = jnp.tile(k_norm_g, (1, 256 // A_HEAD_DIM))[:, None, :]
    norm_g = norm_g[:, None, :]
    b_gate = b_gate[:, None, :]
    pool_scale = pool_scale[:, None, :]
    gla_norm_g = gla_norm_g[:, None, :]
    cos_p, sin_p = _rope_tables(jnp.arange(SEQ))
    cos_s, sin_s = _rope_tables(PAST_LEN + jnp.arange(DEC_SEQ))
    cos_s = jnp.tile(cos_s, (SMP_BS, 1))
    sin_s = jnp.tile(sin_s, (SMP_BS, 1))
    lane = np.arange(256)
    bd = jnp.asarray((lane[:, None] // A_HEAD_DIM == lane[None, :] // A_HEAD_DIM).astype(np.float32), dtype=BF)
    ones_c = jnp.ones((GLA_C, 128), BF)
    cache_k = cache_a_k.reshape(DEPTH, DEC_BATCH, WINDOW, KV_W)
    cache_v = cache_a_v.reshape(DEPTH, DEC_BATCH, WINDOW, KV_W)
    pool_prev = jnp.pad(state_pool, ((0, 0), (0, 0), (1, 0), (0, 0)))

    w_t = jnp.swapaxes(w_in, 1, 2)
    x = (x_prompt.reshape(T_P, D_MODEL), x_sample.reshape(T_S, D_MODEL))
    h, loga = _norm(x[0], x[1], 0, norm_g, w_lr, w_g, b_gate)
    pk, pv, pp, sp = [], [], [], []
    nk_buf = nv_buf = sp_buf = ss_buf = None
    for l in range(DEPTH):
        proj_a = _inproj_a(h, w_t, l)
        proj_b = _inproj_b(h, w_t, l)

        oa, knew_p = _attn_prompt(proj_a, l, sinks, cos_p, sin_p, bd, qg, kg)
        oa, nk_buf, nv_buf = _attn_sample(oa, nk_buf, nv_buf, proj_a, l, cache_k, cache_v, sinks,
                                          cos_s, sin_s, bd, qg, kg)
        ob, tail_p = _pool_prompt(proj_a, l, w_pool_bf, pool_scale)
        ob, tail_s = _pool_sample(ob, proj_a, l, pool_prev, w_pool_bf, pool_scale)
        oc, sp_buf = _gla_prompt(sp_buf, proj_a, proj_b, loga, l, gla_norm_g)
        oc, ss_buf = _gla_sample(oc, ss_buf, proj_a, proj_b, loga, l, state_gla, ones_c, gla_norm_g)

        outs = _merge_out(oa, ob, oc, proj_b, x, l, wb_bf, wo_bf, w_branch, w_out, norm_g, w_lr, w_g, b_gate)
        if l + 1 < DEPTH:
            x, h, loga, wb_bf, wo_bf = outs
        else:
            yp, ys = outs

        vp = proj_a[:T_P, A_AV:A_AV + KV_W].reshape(BATCH, SEQ, KV_W)[:, SEQ - WINDOW:].astype(F32)
        pk.append(knew_p)
        pv.append(vp)
        pp.append(tail_p[:, 1:])
        sp.append(tail_s[:, 1:])

    kv5 = lambda a, nb: a.reshape(DEPTH, nb, WINDOW, A_KV_HEADS, A_HEAD_DIM)
    yp = yp.reshape(BATCH, SEQ, D_MODEL)
    ys = ys.reshape(DEC_BATCH, DEC_SEQ, D_MODEL)
    return (yp, ys, kv5(jnp.stack(pk), BATCH), kv5(jnp.stack(pv), BATCH), jnp.stack(pp), sp_buf,
            kv5(nk_buf, DEC_BATCH), kv5(nv_buf, DEC_BATCH), jnp.stack(sp), ss_buf)
```

```python
import numpy as np
import jax
import jax.numpy as jnp
from jax import lax
from jax.experimental import pallas as pl
from jax.experimental.pallas import tpu as pltpu

F32 = jnp.float32
BF = jnp.bfloat16

D_MODEL = 2048
BATCH = 4
SEQ = 2048
DEPTH = 4
DEC_BATCH = 32
DEC_SEQ = 8
PAST_LEN = 16384
BRANCH = D_MODEL // 2
N_BRANCH = 3
A_HEADS = 16
A_KV_HEADS = 4
A_HEAD_DIM = 64
A_GROUP = A_HEADS // A_KV_HEADS
WINDOW = 128
ROPE_THETA = 10000.0
POOL_WINDOWS = (2, 4, 8, 16)
POOL_GROUP_DIM = 256
POOL_HIST = 15
C_HEADS = 4
C_KEY = 512
C_DK = 128
C_DV = 256
C_GATE_RANK = 16
C_GATE_TAU = 16.0
EPS = 1e-6
KV_W = A_KV_HEADS * A_HEAD_DIM
IN_COLS = 13840

T_P = BATCH * SEQ
T_S = DEC_BATCH * DEC_SEQ
T_ALL = T_P + T_S

A_AQ, A_AK, A_AV, A_AZ = 0, 1024, 1280, 1536
A_BU, A_BZ = 2560, 3584
A_CQ, A_CK, A_CV = 4608, 5120, 5632
A_COLS = 6656
LR_SHIFT = C_GATE_RANK
B_COLS = IN_COLS - A_COLS - LR_SHIFT
B_G, B_CZ = 0, N_BRANCH * D_MODEL

LANES = 128
TM = 1056
TN_A = 1664
TN_B = 1024
W_SUB = 512
CAST_ROWS = 128
TM_OUT = 256
N_P_TILES = T_P // TM_OUT
N_TILES = T_ALL // TM_OUT
ATT_BLK = WINDOW
ATT_NB = 1
POOL_TP = 1024
POOL_HALO = 16
GLA_C = 64
GLA_NSUB = 8
GLA_SC = GLA_C * GLA_NSUB
GLA_HPS = 4
GLA_VW = 512
GLA_VBLKS = GLA_HPS * C_DV // GLA_VW
SMP_BS = 4
SMP_GB = 8
VMEM_LIMIT = 56 * 1024 * 1024
NEG_BIG = -1e30


def _cparams(n_axes):
    return pltpu.CompilerParams(
        dimension_semantics=("arbitrary",) * n_axes, vmem_limit_bytes=VMEM_LIMIT)


def _sigmoid(x):
    return 0.5 * jnp.tanh(0.5 * x) + 0.5


def _silu(x):
    return x * _sigmoid(x)


def _dot(a, b):
    return jnp.dot(a, b, preferred_element_type=F32)


def _dot_nt(a, b):
    return lax.dot_general(a, b, (((1,), (1,)), ((), ())), preferred_element_type=F32)


def _dot_tn(a, b):
    return lax.dot_general(a, b, (((0,), (0,)), ((), ())), preferred_element_type=F32)


def _split3(x):
    hi = x.astype(BF)
    r1 = x - hi.astype(F32)
    mid = r1.astype(BF)
    lo = (r1 - mid.astype(F32)).astype(BF)
    return hi, mid, lo


def _any_spec():
    return pl.BlockSpec(memory_space=pl.ANY)


def _row_spec(l, width):
    return pl.BlockSpec((None, 1, width), lambda *a: (l, 0, 0))


def _norm_gate(xf, g_ref, wlr_ref, wg_ref, bg_ref):
    ms = jnp.mean(xf * xf, axis=-1, keepdims=True)
    h = (xf * lax.rsqrt(ms + EPS) * g_ref[...]).astype(BF)
    lr = _dot(h, wlr_ref[...])
    z = _dot(lr.astype(BF), wg_ref[...]) + bg_ref[...]
    log_sig = jnp.minimum(z, 0.0) - jnp.log1p(jnp.exp(-jnp.abs(z)))
    return h, log_sig * (1.0 / C_GATE_TAU)


def _norm_specs(l):
    return [
        _row_spec(l, D_MODEL),
        pl.BlockSpec((None, D_MODEL, LANES), lambda *a: (l, 0, 0)),
        pl.BlockSpec((None, LANES, C_KEY), lambda *a: (l, 0, 0)),
        _row_spec(l, C_KEY),
    ]


def _split_rows_specs(width):
    return [pl.BlockSpec((TM_OUT, width), lambda i: (jnp.minimum(i, N_P_TILES - 1), 0)),
            pl.BlockSpec((TM_OUT, width), lambda i: (0, 0))]


def _pick_rows(xp_ref, xs_ref):
    return jnp.where(pl.program_id(0) < N_P_TILES, xp_ref[...], xs_ref[...])


def _norm_kernel(xp_ref, xs_ref, g_ref, wlr_ref, wg_ref, bg_ref, h_ref, loga_ref):
    h, loga = _norm_gate(_pick_rows(xp_ref, xs_ref), g_ref, wlr_ref, wg_ref, bg_ref)
    h_ref[...] = h
    loga_ref[...] = loga


def _norm(xp, xs, l, norm_g, w_lr, w_g, b_gate):
    return pl.pallas_call(
        _norm_kernel,
        grid=(N_TILES,),
        in_specs=_split_rows_specs(D_MODEL) + _norm_specs(l),
        out_specs=[pl.BlockSpec((TM_OUT, D_MODEL), lambda i: (i, 0)),
                   pl.BlockSpec((TM_OUT, C_KEY), lambda i: (i, 0))],
        out_shape=[jax.ShapeDtypeStruct((T_ALL, D_MODEL), BF),
                   jax.ShapeDtypeStruct((T_ALL, C_KEY), F32)],
        compiler_params=_cparams(1),
    )(xp, xs, norm_g, w_lr, w_g, b_gate)


def _inproj_a_kernel(h_ref, w_ref, o_ref, w_scr):
    @pl.when(pl.program_id(1) == 0)
    def _():
        for r in range(0, TN_A, CAST_ROWS):
            w_scr[r:r + CAST_ROWS, :] = w_ref[r:r + CAST_ROWS, :].astype(BF)

    o_ref[...] = _dot_nt(h_ref[...], w_scr[...]).astype(o_ref.dtype)


def _inproj_a(h, w_t, l):
    return pl.pallas_call(
        _inproj_a_kernel,
        grid=(A_COLS // TN_A, T_ALL // TM),
        in_specs=[
            pl.BlockSpec((TM, D_MODEL), lambda j, i: (i, 0)),
            pl.BlockSpec((None, TN_A, D_MODEL), lambda j, i: (l, j, 0)),
        ],
        out_specs=pl.BlockSpec((TM, TN_A), lambda j, i: (i, j)),
        out_shape=jax.ShapeDtypeStruct((T_ALL, A_COLS), BF),
        scratch_shapes=[pltpu.VMEM((TN_A, D_MODEL), BF)],
        compiler_params=_cparams(2),
    )(h, w_t)


def _copy_cast_rows(dst, dst0, src, src0, n):
    for r in range(0, n, CAST_ROWS):
        m = min(CAST_ROWS, n - r)
        dst[dst0 + r:dst0 + r + m, :] = src[src0 + r:src0 + r + m, :].astype(BF)


def _inproj_b_kernel(h_ref, w0_ref, w1_ref, w2_ref, o_ref, w_scr):
    @pl.when(pl.program_id(1) == 0)
    def _():
        _copy_cast_rows(w_scr, 0, w0_ref, LR_SHIFT, W_SUB - LR_SHIFT)
        _copy_cast_rows(w_scr, W_SUB - LR_SHIFT, w1_ref, 0, W_SUB)
        _copy_cast_rows(w_scr, 2 * W_SUB - LR_SHIFT, w2_ref, 0, LR_SHIFT)

    o_ref[...] = _dot_nt(h_ref[...], w_scr[...]).astype(o_ref.dtype)


def _inproj_b(h, w_t, l):
    assert TN_B == 2 * W_SUB
    nt = B_COLS // TN_B
    sub0 = A_COLS // W_SUB
    tail0 = (A_COLS + TN_B) // LR_SHIFT
    return pl.pallas_call(
        _inproj_b_kernel,
        grid=(nt, T_ALL // TM),
        in_specs=[
            pl.BlockSpec((TM, D_MODEL), lambda j, i: (i, 0)),
            pl.BlockSpec((None, W_SUB, D_MODEL), lambda j, i: (l, sub0 + 2 * j, 0)),
            pl.BlockSpec((None, W_SUB, D_MODEL), lambda j, i: (l, sub0 + 2 * j + 1, 0)),
            pl.BlockSpec((None, LR_SHIFT, D_MODEL), lambda j, i: (l, tail0 + (TN_B // LR_SHIFT) * j, 0)),
        ],
        out_specs=pl.BlockSpec((TM, TN_B), lambda j, i: (i, (j + nt - 1) % nt)),
        out_shape=jax.ShapeDtypeStruct((T_ALL, B_COLS), BF),
        scratch_shapes=[pltpu.VMEM((TN_B, D_MODEL), BF)],
        compiler_params=_cparams(2),
    )(h, w_t, w_t, w_t)


def _head_norm(x, bd, g_row):
    outs = []
    for c in range(x.shape[1] // 256):
        xc = x[:, c * 256:(c + 1) * 256]
        ss = _dot((xc * xc).astype(BF), bd)
        outs.append(xc * lax.rsqrt(ss * (1.0 / A_HEAD_DIM) + EPS) * g_row)
    return outs


def _rope128(xc, cos, sin, first_half):
    swapped = jnp.where(first_half, pltpu.roll(xc, 96, 1), pltpu.roll(xc, 32, 1))
    return xc * cos + swapped * sin


def _norm_rope(x, bd, g_row, cos, sin, first_half):
    chunks = []
    for blk in _head_norm(x, bd, g_row):
        for c in range(2):
            chunks.append(_rope128(blk[:, c * 128:(c + 1) * 128], cos, sin, first_half))
    return chunks


def _dup_half(chunk, rolled, lo_half, use_low):
    return jnp.where(lo_half, chunk, rolled) if use_low else jnp.where(lo_half, rolled, chunk)


def _attn_prompt_kernel(l, sinks_ref, q_ref, z0_ref, z1_ref, k_ref, v_ref, cos_ref, sin_ref, bd_ref, qg_ref,
                        kg_ref, o_ref, knew_ref, kprev, vprev_e, vprev_o):
    n = pl.program_id(1)
    R = ATT_BLK
    RR = ATT_NB * R
    lane_all = lax.broadcasted_iota(jnp.int32, (RR, 128), 1)
    first_half = (lane_all & 32) == 0
    lane = lax.broadcasted_iota(jnp.int32, (R, 128), 1)
    lo_half = lane < 64
    hi_half = jnp.logical_not(lo_half)
    cos = cos_ref[...]
    sin = sin_ref[...]
    bd = bd_ref[...]

    @pl.when(n == 0)
    def _():
        kprev[...] = jnp.zeros_like(kprev)
        vprev_e[...] = jnp.zeros_like(vprev_e)
        vprev_o[...] = jnp.zeros_like(vprev_o)

    kch_all = _norm_rope(k_ref[...].astype(F32), bd, kg_ref[...], cos, sin, first_half)
    knew_ref[0] = jnp.concatenate([kc[RR - R:] for kc in kch_all], axis=1)
    vraw = v_ref[...].astype(F32)
    vch_all = [vraw[:, :128], vraw[:, 128:]]
    keys = [[kprev[h] for h in range(A_KV_HEADS)]]
    vals_e = [[vprev_e[h] for h in range(A_KV_HEADS)]]
    vals_o = [[vprev_o[h] for h in range(A_KV_HEADS)]]
    for blk in range(ATT_NB):
        rows = slice(blk * R, (blk + 1) * R)
        kcur, vcur_e, vcur_o = [], [], []
        for h in range(A_KV_HEADS):
            c, low = h // 2, (h % 2 == 0)
            kc, vc = kch_all[c][rows], vch_all[c][rows]
            kcur.append(_dup_half(kc, pltpu.roll(kc, 64, 1), lo_half, low).astype(BF))
            v_lo = vc if low else pltpu.roll(vc, 64, 1)
            v_hi = pltpu.roll(vc, 64, 1) if low else vc
            vcur_e.append(jnp.where(lo_half, v_lo, 1.0).astype(BF))
            vcur_o.append(jnp.where(lo_half, 1.0, v_hi).astype(BF))
        keys.append(kcur)
        vals_e.append(vcur_e)
        vals_o.append(vcur_o)

    qch_all = _norm_rope(q_ref[...].astype(F32), bd, qg_ref[...], cos, sin, first_half)

    G = A_GROUP * R
    srow = lax.broadcasted_iota(jnp.int32, (G, 128), 0)
    scol = lax.broadcasted_iota(jnp.int32, (G, 128), 1)
    qi = srow & (R - 1)
    bias_prev_any = jnp.where(scol >= qi, 0.0, NEG_BIG)
    bias_prev_first = jnp.where(jnp.logical_and(scol >= qi, n > 0), 0.0, NEG_BIG)
    bias_cur = jnp.where(scol <= qi, 0.0, NEG_BIG)

    def scores(blk, h):
        rows = slice(blk * R, (blk + 1) * R)
        qa = qch_all[2 * h][rows]
        qb = qch_all[2 * h + 1][rows]
        q_stack = jnp.concatenate([jnp.where(lo_half, qa, 0.0), jnp.where(lo_half, qb, 0.0),
                                   jnp.where(hi_half, qa, 0.0), jnp.where(hi_half, qb, 0.0)],
                                  axis=0).astype(BF)
        bias_prev = bias_prev_first if blk == 0 else bias_prev_any
        return _dot_nt(q_stack, keys[blk][h]) + bias_prev, _dot_nt(q_stack, keys[blk + 1][h]) + bias_cur

    problems = [(blk, h) for blk in range(ATT_NB) for h in range(A_KV_HEADS)]
    ahead = scores(*problems[0])
    for idx, (blk, h) in enumerate(problems):
        s_p, s_c = ahead
        if idx + 1 < len(problems):
            ahead = scores(*problems[idx + 1])
        vprev_e_h, vcur_e_h = vals_e[blk][h], vals_e[blk + 1][h]
        vprev_o_h, vcur_o_h = vals_o[blk][h], vals_o[blk + 1][h]
        head_of_blk = (4 * h, 4 * h + 2, 4 * h + 1, 4 * h + 3)
        row_max = jnp.max(jnp.maximum(s_p, s_c), axis=-1, keepdims=True)
        m_blk = [jnp.maximum(row_max[bi * R:(bi + 1) * R], sinks_ref[l, hd]) for bi, hd in enumerate(head_of_blk)]
        e_sink = [jnp.exp(sinks_ref[l, hd] - m_blk[bi]) for bi, hd in enumerate(head_of_blk)]
        m = jnp.concatenate(m_blk, axis=0)
        p_p = jnp.exp(s_p - m).astype(BF)
        p_c = jnp.exp(s_c - m).astype(BF)
        half = G // 2
        pv_e = _dot(p_p[:half], vprev_e_h) + _dot(p_c[:half], vcur_e_h)
        pv_o = _dot(p_p[half:], vprev_o_h) + _dot(p_c[half:], vcur_o_h)
        out_rows = slice(blk * R, (blk + 1) * R)
        for cc in range(2):
            rows_e = slice(cc * R, (cc + 1) * R)
            ev, od = pv_e[rows_e], pv_o[rows_e]
            num = jnp.where(lo_half, ev, od)
            den = pltpu.roll(jnp.where(lo_half, od, ev), 64, 1)
            den = den + jnp.where(lo_half, e_sink[cc], e_sink[2 + cc])
            c = 2 * h + cc
            z_ref = z0_ref if c < 4 else z1_ref
            zc = z_ref[out_rows, (c % 4) * 128:(c % 4 + 1) * 128].astype(F32)
            o_ref[out_rows, c * 128:(c + 1) * 128] = (num / den * _silu(zc)).astype(o_ref.dtype)

    for h in range(A_KV_HEADS):
        kprev[h] = keys[ATT_NB][h]
        vprev_e[h] = vals_e[ATT_NB][h]
        vprev_o[h] = vals_o[ATT_NB][h]


def _attn_prompt(proj_a, l, sinks, cos, sin, bd, qg, kg):
    rows = ATT_NB * ATT_BLK
    nb = SEQ // rows
    rb = lambda b, n: b * nb + n
    kern = lambda *refs: _attn_prompt_kernel(l, *refs)
    return pl.pallas_call(
        kern,
        grid=(BATCH, nb),
        in_specs=[
            pl.BlockSpec(memory_space=pltpu.SMEM),
            pl.BlockSpec((rows, BRANCH), lambda b, n: (rb(b, n), A_AQ // BRANCH)),
            pl.BlockSpec((rows, 512), lambda b, n: (rb(b, n), A_AZ // 512)),
            pl.BlockSpec((rows, 512), lambda b, n: (rb(b, n), A_AZ // 512 + 1)),
            pl.BlockSpec((rows, KV_W), lambda b, n: (rb(b, n), A_AK // KV_W)),
            pl.BlockSpec((rows, KV_W), lambda b, n: (rb(b, n), A_AV // KV_W)),
            pl.BlockSpec((rows, 128), lambda b, n: (n, 0)),
            pl.BlockSpec((rows, 128), lambda b, n: (n, 0)),
            pl.BlockSpec((256, 256), lambda b, n: (0, 0)),
            _row_spec(l, 256),
            _row_spec(l, 256),
        ],
        out_specs=[
            pl.BlockSpec((rows, BRANCH), lambda b, n: (rb(b, n), 0)),
            pl.BlockSpec((1, ATT_BLK, KV_W), lambda b, n: (b, 0, 0)),
        ],
        out_shape=[
            jax.ShapeDtypeStruct((T_ALL, BRANCH), BF),
            jax.ShapeDtypeStruct((BATCH, WINDOW, KV_W), F32),
        ],
        scratch_shapes=[pltpu.VMEM((A_KV_HEADS, ATT_BLK, 128), BF),
                        pltpu.VMEM((A_KV_HEADS, ATT_BLK, 128), BF),
                        pltpu.VMEM((A_KV_HEADS, ATT_BLK, 128), BF)],
        compiler_params=_cparams(2),
    )(sinks, proj_a, proj_a, proj_a, proj_a, proj_a, cos, sin, bd, qg, kg)


def _attn_sample_kernel(l, n_alias, *refs):
    refs = refs[n_alias:]
    (sinks_ref, q_ref, z0_ref, z1_ref, k_ref, v_ref, ck_ref, cv_ref, cos_ref, sin_ref,
     bd_ref, qg_ref, kg_ref, o_ref, nk_ref, nv_ref, kall, vall) = refs
    L = DEC_SEQ
    R = SMP_BS * L
    lane = lax.broadcasted_iota(jnp.int32, (R, 128), 1)
    first_half = (lane & 32) == 0
    lane8 = lax.broadcasted_iota(jnp.int32, (L, 128), 1)
    lo8 = lane8 < 64
    lane_c = lax.broadcasted_iota(jnp.int32, (WINDOW, 128), 1)
    lo_c = lane_c < 64
    cos = cos_ref[...]
    sin = sin_ref[...]
    bd = bd_ref[...]

    @pl.when(pl.program_id(0) == 0)
    def _():
        kall[...] = jnp.zeros_like(kall)
        vall[...] = jnp.zeros_like(vall)

    kch = _norm_rope(k_ref[...].astype(F32), bd, kg_ref[...], cos, sin, first_half)
    vraw = v_ref[...].astype(F32)
    vch = [vraw[:, :128], vraw[:, 128:]]
    qch = _norm_rope(q_ref[...].astype(F32), bd, qg_ref[...], cos, sin, first_half)

    rows = A_GROUP * L
    srow = lax.broadcasted_iota(jnp.int32, (rows, 2 * WINDOW), 0)
    scol = lax.broadcasted_iota(jnp.int32, (rows, 2 * WINDOW), 1)
    t = srow & (L - 1)
    mask = jnp.logical_or(jnp.logical_and(scol < WINDOW, scol >= t),
                          jnp.logical_and(scol >= WINDOW, (scol - WINDOW) <= t))
    hrow = lax.broadcasted_iota(jnp.int32, (rows, 1), 0) >> (L.bit_length() - 1)
    z_all = jnp.concatenate([z0_ref[...], z1_ref[...]], axis=1).astype(F32)
    bias = jnp.where(mask, 0.0, NEG_BIG)
    pairs = [(bi, h) for bi in range(SMP_BS) for h in range(A_KV_HEADS)]
    slot = lambda bi, h: bi * A_KV_HEADS + h

    for bi in range(SMP_BS):
        r0 = bi * L
        knew = [kc[r0:r0 + L] for kc in kch]
        vnew = [vc[r0:r0 + L] for vc in vch]
        ck = ck_ref[bi].T
        cv = cv_ref[bi].T
        nk_ref[bi] = jnp.concatenate([ck[L:WINDOW], jnp.concatenate(knew, axis=1)], axis=0).T
        nv_ref[bi] = jnp.concatenate([cv[L:WINDOW], jnp.concatenate(vnew, axis=1)], axis=0).T
        for h in range(A_KV_HEADS):
            c, low = h // 2, (h % 2 == 0)
            ckc = ck[:, c * 128:(c + 1) * 128]
            cvc = cv[:, c * 128:(c + 1) * 128]
            kall[slot(bi, h), 0:WINDOW] = _dup_half(ckc, pltpu.roll(ckc, 64, 1), lo_c, low)
            vall[slot(bi, h), 0:WINDOW] = _dup_half(cvc, pltpu.roll(cvc, 64, 1), lo_c, low)
            kall[slot(bi, h), WINDOW:WINDOW + L] = _dup_half(knew[c], pltpu.roll(knew[c], 64, 1), lo8, low)
            vall[slot(bi, h), WINDOW:WINDOW + L] = _dup_half(vnew[c], pltpu.roll(vnew[c], 64, 1), lo8, low)

    scores = []
    for bi, h in pairs:
        r0 = bi * L
        qs = []
        for gi in range(A_GROUP):
            i = h * A_GROUP + gi
            sel = lo8 if i % 2 == 0 else jnp.logical_not(lo8)
            qs.append(jnp.where(sel, qch[i // 2][r0:r0 + L], 0.0))
        qm = jnp.concatenate(qs, axis=0).astype(BF)
        scores.append(_dot_nt(qm, kall[slot(bi, h)].astype(BF)) + bias)

    probs, dens = [], []
    for (bi, h), s in zip(pairs, scores):
        sink = jnp.zeros((rows, 1), F32)
        for gi in range(A_GROUP):
            sink = jnp.where(hrow == gi, sinks_ref[l, h * A_GROUP + gi], sink)
        m = jnp.maximum(jnp.max(s, axis=-1, keepdims=True), sink)
        p = jnp.exp(s - m)
        dens.append(jnp.sum(p, axis=-1, keepdims=True) + jnp.exp(sink - m))
        probs.append(p.astype(BF))

    pvs = [_dot(p, vall[slot(bi, h)].astype(BF)) / den for (bi, h), p, den in zip(pairs, probs, dens)]
    o_rows = []
    for bi in range(SMP_BS):
        o_chunks = []
        for h in range(A_KV_HEADS):
            pv = pvs[slot(bi, h)]
            for cc in range(2):
                o_chunks.append(jnp.where(lo8, pv[(2 * cc) * L:(2 * cc + 1) * L],
                                          pv[(2 * cc + 1) * L:(2 * cc + 2) * L]))
        o = jnp.concatenate(o_chunks, axis=1)
        o_rows.append(o * _silu(z_all[bi * L:(bi + 1) * L]))
    o_ref[...] = jnp.concatenate(o_rows, axis=0).astype(o_ref.dtype)


def _attn_sample(oa, nk_buf, nv_buf, proj_a, l, cache_k, cache_v, sinks, cos, sin, bd, qg, kg):
    R = SMP_BS * DEC_SEQ
    base = T_P // R
    alias_in = [oa] + ([nk_buf, nv_buf] if l > 0 else [])
    n_alias = len(alias_in)
    kern = lambda *refs: _attn_sample_kernel(l, n_alias, *refs)
    cache_spec = pl.BlockSpec((None, SMP_BS, KV_W, WINDOW), lambda b: (l, b, 0, 0))
    return pl.pallas_call(
        kern,
        grid=(DEC_BATCH // SMP_BS,),
        in_specs=[_any_spec()] * n_alias + [
            pl.BlockSpec(memory_space=pltpu.SMEM),
            pl.BlockSpec((R, BRANCH), lambda b: (base + b, A_AQ // BRANCH)),
            pl.BlockSpec((R, 512), lambda b: (base + b, A_AZ // 512)),
            pl.BlockSpec((R, 512), lambda b: (base + b, A_AZ // 512 + 1)),
            pl.BlockSpec((R, KV_W), lambda b: (base + b, A_AK // KV_W)),
            pl.BlockSpec((R, KV_W), lambda b: (base + b, A_AV // KV_W)),
            cache_spec, cache_spec,
            pl.BlockSpec((R, 128), lambda b: (0, 0)),
            pl.BlockSpec((R, 128), lambda b: (0, 0)),
            pl.BlockSpec((256, 256), lambda b: (0, 0)),
            _row_spec(l, 256),
            _row_spec(l, 256),
        ],
        out_specs=[pl.BlockSpec((R, BRANCH), lambda b: (base + b, 0)), cache_spec, cache_spec],
        out_shape=[
            jax.ShapeDtypeStruct((T_ALL, BRANCH), BF),
            jax.ShapeDtypeStruct((DEPTH, DEC_BATCH, KV_W, WINDOW), F32),
            jax.ShapeDtypeStruct((DEPTH, DEC_BATCH, KV_W, WINDOW), F32),
        ],
        scratch_shapes=[pltpu.VMEM((SMP_BS * A_KV_HEADS, 2 * WINDOW, 128), F32),
                        pltpu.VMEM((SMP_BS * A_KV_HEADS, 2 * WINDOW, 128), F32)],
        input_output_aliases={i: i for i in range(n_alias)},
        compiler_params=_cparams(1),
    )(*alias_in, sinks, proj_a, proj_a, proj_a, proj_a, proj_a, cache_k, cache_v, cos, sin, bd, qg, kg)


def _pool_compute(ext, u_rows, start_pos, wp_ref, scale_ref, z, n_rows):
    pos = start_pos + lax.broadcasted_iota(jnp.int32, (n_rows, 1), 0)
    outs = []
    for g, w in enumerate(POOL_WINDOWS):
        sl = slice(g * POOL_GROUP_DIM, (g + 1) * POOL_GROUP_DIM)
        acc = ext[:, sl]
        span = 1
        while span < w:
            acc = acc + pltpu.roll(acc, span, 0)
            span *= 2
        win_sum = acc[POOL_HALO:]
        count = jnp.minimum(w, pos + 1).astype(F32)
        d = (win_sum / count - u_rows[:, sl]).astype(BF)
        outs.append(_dot(d, wp_ref[g]))
    y = jnp.concatenate(outs, axis=1) * scale_ref[...]
    return y * _silu(z)


def _pool_prompt_kernel(u0_ref, u1_ref, z0_ref, z1_ref, wp_ref, scale_ref, o_ref, tail_ref, ext):
    t = pl.program_id(1)

    @pl.when(t == 0)
    def _():
        ext[0:POOL_HALO] = jnp.zeros((POOL_HALO, BRANCH), F32)

    u = jnp.concatenate([u0_ref[...], u1_ref[...]], axis=1).astype(F32)
    z = jnp.concatenate([z0_ref[...], z1_ref[...]], axis=1).astype(F32)
    ext[POOL_HALO:] = u
    out = _pool_compute(ext[...], u, t * POOL_TP, wp_ref, scale_ref, z, POOL_TP)
    o_ref[...] = out.astype(o_ref.dtype)
    tail = ext[POOL_TP:POOL_TP + POOL_HALO]
    tail_ref[0] = tail
    ext[0:POOL_HALO] = tail


def _pool_w_specs(l):
    ng = len(POOL_WINDOWS)
    return [pl.BlockSpec((None, ng, POOL_GROUP_DIM, POOL_GROUP_DIM), lambda *a: (l, 0, 0, 0)),
            _row_spec(l, BRANCH)]


def _pool_prompt(proj_a, l, w_pool, scale):
    nt = SEQ // POOL_TP
    half = lambda off, k: pl.BlockSpec((POOL_TP, 512), lambda b, t: (b * nt + t, off // 512 + k))
    return pl.pallas_call(
        _pool_prompt_kernel,
        grid=(BATCH, nt),
        in_specs=[half(A_BU, 0), half(A_BU, 1), half(A_BZ, 0), half(A_BZ, 1)] + _pool_w_specs(l),
        out_specs=[
            pl.BlockSpec((POOL_TP, BRANCH), lambda b, t: (b * nt + t, 0)),
            pl.BlockSpec((1, POOL_HALO, BRANCH), lambda b, t: (b, 0, 0)),
        ],
        out_shape=[
            jax.ShapeDtypeStruct((T_ALL, BRANCH), BF),
            jax.ShapeDtypeStruct((BATCH, POOL_HALO, BRANCH), F32),
        ],
        scratch_shapes=[pltpu.VMEM((POOL_HALO + POOL_TP, BRANCH), F32)],
        compiler_params=_cparams(2),
    )(proj_a, proj_a, proj_a, proj_a, w_pool, scale)


def _pool_sample_kernel(ob_in_ref, u0_ref, u1_ref, z0_ref, z1_ref, prev_ref, wp_ref, scale_ref, o_ref, tail_ref):
    del ob_in_ref
    L = DEC_SEQ
    u_all = jnp.concatenate([u0_ref[...], u1_ref[...]], axis=1).astype(F32)
    z_all = jnp.concatenate([z0_ref[...], z1_ref[...]], axis=1).astype(F32)
    outs = []
    for bi in range(SMP_BS):
        u = u_all[bi * L:(bi + 1) * L]
        ext = jnp.concatenate([prev_ref[bi], u], axis=0)
        outs.append(_pool_compute(ext, u, PAST_LEN, wp_ref, scale_ref, z_all[bi * L:(bi + 1) * L], L))
        tail_ref[bi] = ext[L:L + POOL_HALO]
    o_ref[...] = jnp.concatenate(outs, axis=0).astype(o_ref.dtype)


def _pool_sample(ob, proj_a, l, prev16, w_pool, scale):
    R = SMP_BS * DEC_SEQ
    base = T_P // R
    half = lambda off, k: pl.BlockSpec((R, 512), lambda b: (base + b, off // 512 + k))
    return pl.pallas_call(
        _pool_sample_kernel,
        grid=(DEC_BATCH // SMP_BS,),
        in_specs=[_any_spec(), half(A_BU, 0), half(A_BU, 1), half(A_BZ, 0), half(A_BZ, 1),
                  pl.BlockSpec((None, SMP_BS, POOL_HALO, BRANCH), lambda b: (l, b, 0, 0))] + _pool_w_specs(l),
        out_specs=[
            pl.BlockSpec((R, BRANCH), lambda b: (base + b, 0)),
            pl.BlockSpec((SMP_BS, POOL_HALO, BRANCH), lambda b: (b, 0, 0)),
        ],
        out_shape=[
            jax.ShapeDtypeStruct((T_ALL, BRANCH), BF),
            jax.ShapeDtypeStruct((DEC_BATCH, POOL_HALO, BRANCH), F32),
        ],
        input_output_aliases={0: 0},
        compiler_params=_cparams(1),
    )(ob, proj_a, proj_a, proj_a, proj_a, prev16, w_pool, scale)


def _chunk_cumsum(x, chunk):
    pos = lax.broadcasted_iota(jnp.int32, x.shape, 0) & (chunk - 1)
    span = 1
    while span < chunk:
        x = x + jnp.where(pos >= span, pltpu.roll(x, span, 0), 0.0)
        span *= 2
    return x


def _chunk_row(x, chunk, r):
    n = x.shape[0] // chunk
    return jnp.concatenate(
        [jnp.broadcast_to(x[c * chunk + r:c * chunk + r + 1], (chunk, x.shape[1])) for c in range(n)], axis=0)


def _gla_prompt_kernel(n_alias, *refs):
    refs = refs[n_alias:]
    q_ref, k_ref = refs[:2]
    v_refs = refs[2:2 + GLA_VBLKS]
    z_ref, la_ref, g_ref, o_ref, snew_ref, st_scr = refs[2 + GLA_VBLKS:]
    step = pl.program_id(2)

    @pl.when(step == 0)
    def _():
        st_scr[...] = jnp.zeros_like(st_scr)

    C = GLA_C
    chunks = [slice(c * C, (c + 1) * C) for c in range(GLA_NSUB)]
    row = lax.broadcasted_iota(jnp.int32, (GLA_SC, GLA_SC), 0)
    col = lax.broadcasted_iota(jnp.int32, (GLA_SC, GLA_SC), 1)
    visible = (row - col).astype(jnp.uint32) <= (row & (C - 1)).astype(jnp.uint32)
    g_row = g_ref[...]

    finals = []
    for hh in range(GLA_HPS):
        kcols = slice(hh * C_DK, (hh + 1) * C_DK)
        vcols = slice(hh * C_DV, (hh + 1) * C_DV)
        b = _chunk_cumsum(la_ref[:, kcols], C)
        b_last = _chunk_row(b, C, C - 1)
        b_mid = _chunk_row(b, C, C // 2 - 1)
        q = q_ref[:, kcols].astype(F32) * (C_DK ** -0.5)
        k = k_ref[:, kcols].astype(F32)
        v0 = (hh * C_DV) % GLA_VW
        v = v_refs[hh * C_DV // GLA_VW][:, v0:v0 + C_DV].astype(BF)
        kd = (k * jnp.exp(b_last - b)).astype(BF)
        qe = (q * jnp.exp(b)).astype(BF)
        q2 = (q * jnp.exp(b - b_mid)).astype(BF)
        k2 = (k * jnp.exp(b_mid - b)).astype(BF)

        att = jnp.where(visible, _dot_nt(q2, k2), 0.0).astype(BF)
        o = _dot(att, v)

        ut = [_dot_tn(v[sl], kd[sl]) for sl in chunks]
        st = st_scr[hh]
        st_before = []
        for c, sl in enumerate(chunks):
            st_before.append(st.astype(BF))
            st = st * jnp.exp(b[(c + 1) * C - 1:(c + 1) * C]) + ut[c]
        st_scr[hh] = st
        finals.append(st)

        o = o + jnp.concatenate([_dot_nt(qe[sl], st_before[c]) for c, sl in enumerate(chunks)], axis=0)
        ms = jnp.mean(o * o, axis=-1, keepdims=True)
        on = o * lax.rsqrt(ms + EPS) * g_row
        o_ref[:, vcols] = (on * _silu(z_ref[:, vcols].astype(F32))).astype(o_ref.dtype)

    @pl.when(step == pl.num_programs(2) - 1)
    def _():
        for hh in range(GLA_HPS):
            snew_ref[0, hh] = finals[hh].T


def _gla_prompt(s_buf, proj_a, proj_b, loga, l, g):
    ns = SEQ // GLA_SC
    rb = lambda b, h, s: b * ns + s
    alias_in = [s_buf] if l > 0 else []
    n_alias = len(alias_in)
    kern = lambda *refs: _gla_prompt_kernel(n_alias, *refs)
    kw, vw = GLA_HPS * C_DK, GLA_HPS * C_DV

    def cols(width, offset, stride=1, extra=0):
        assert offset % width == 0
        return pl.BlockSpec((GLA_SC, width), lambda b, h, s: (rb(b, h, s), offset // width + stride * h + extra))

    return pl.pallas_call(
        kern,
        grid=(BATCH, C_HEADS // GLA_HPS, ns),
        in_specs=[_any_spec()] * n_alias + [cols(kw, A_CQ), cols(kw, A_CK)]
        + [cols(GLA_VW, A_CV, GLA_VBLKS, j) for j in range(GLA_VBLKS)]
        + [cols(vw, B_CZ), cols(kw, 0), _row_spec(l, C_DV)],
        out_specs=[
            pl.BlockSpec((GLA_SC, vw), lambda b, h, s: (rb(b, h, s), h)),
            pl.BlockSpec((None, 1, GLA_HPS, C_DK, C_DV), lambda b, h, s: (l, b, h, 0, 0)),
        ],
        out_shape=[
            jax.ShapeDtypeStruct((T_ALL, BRANCH), BF),
            jax.ShapeDtypeStruct((DEPTH, BATCH, C_HEADS, C_DK, C_DV), F32),
        ],
        scratch_shapes=[pltpu.VMEM((GLA_HPS, C_DV, C_DK), F32)],
        input_output_aliases={0: 1} if l > 0 else {},
        compiler_params=_cparams(3),
    )(*alias_in, *([proj_a] * (2 + GLA_VBLKS)), proj_b, loga, g)


def _gla_sample_kernel(n_alias, *refs):
    refs = refs[n_alias:]
    q_ref, k_ref = refs[:2]
    v_refs = refs[2:2 + GLA_VBLKS]
    z_ref, la_ref, s0_ref, ones_ref, g_ref, o_ref, snew_ref = refs[2 + GLA_VBLKS:]
    C = SMP_GB * DEC_SEQ
    L = DEC_SEQ
    ones = ones_ref[...]
    g_row = g_ref[...]
    row = lax.broadcasted_iota(jnp.int32, (C, C), 0)
    col = lax.broadcasted_iota(jnp.int32, (C, C), 1)
    visible = (row - col).astype(jnp.uint32) <= (row & (L - 1)).astype(jnp.uint32)
    rowk = lax.broadcasted_iota(jnp.int32, (C, C_DK), 0) >> (L.bit_length() - 1)

    for hh in range(GLA_HPS):
        kcols = slice(hh * C_DK, (hh + 1) * C_DK)
        vcols = slice(hh * C_DV, (hh + 1) * C_DV)
        v0 = (hh * C_DV) % GLA_VW
        la = la_ref[:, kcols]
        b = _chunk_cumsum(la, L)
        b_last = _chunk_row(b, L, L - 1)
        b_mid = _chunk_row(b, L, L // 2 - 1)
        q = q_ref[:, kcols].astype(F32) * (C_DK ** -0.5)
        k = k_ref[:, kcols].astype(F32)
        v = v_refs[hh * C_DV // GLA_VW][:, v0:v0 + C_DV].astype(BF)
        kd = k * jnp.exp(b_last - b)
        qe = q * jnp.exp(b)

        o_inter = jnp.zeros((C, C_DV), F32)
        for gi in range(SMP_GB):
            own = rowk == gi
            s0 = s0_ref[gi, hh]
            o_inter = o_inter + _dot(jnp.where(own, qe, 0.0).astype(BF), s0.astype(BF))
            u = _dot_tn(jnp.where(own, kd, 0.0).astype(BF), v)
            own_parts = _split3(jnp.where(own, la, 0.0))
            dsum = _dot_tn(own_parts[0], ones) + _dot_tn(own_parts[1], ones) + _dot_tn(own_parts[2], ones)
            d = jnp.exp(dsum)
            snew_ref[gi, hh] = jnp.concatenate([d, d], axis=1) * s0 + u

        q2 = (q * jnp.exp(b - b_mid)).astype(BF)
        k2 = (k * jnp.exp(b_mid - b)).astype(BF)
        att = jnp.where(visible, _dot_nt(q2, k2), 0.0).astype(BF)
        o = o_inter + _dot(att, v)
        ms = jnp.mean(o * o, axis=-1, keepdims=True)
        on = o * lax.rsqrt(ms + EPS) * g_row
        o_ref[:, vcols] = (on * _silu(z_ref[:, vcols].astype(F32))).astype(o_ref.dtype)


def _gla_sample(oc, s_buf, proj_a, proj_b, loga, l, s0, ones, g):
    C = SMP_GB * DEC_SEQ
    base = T_P // C
    alias_in = [oc] + ([s_buf] if l > 0 else [])
    n_alias = len(alias_in)
    kern = lambda *refs: _gla_sample_kernel(n_alias, *refs)
    kw, vw = GLA_HPS * C_DK, GLA_HPS * C_DV
    state_spec = pl.BlockSpec((None, SMP_GB, GLA_HPS, C_DK, C_DV), lambda b, h: (l, b, h, 0, 0))

    def cols(width, offset, stride=1, extra=0):
        assert offset % width == 0
        return pl.BlockSpec((C, width), lambda b, h: (base + b, offset // width + stride * h + extra))

    return pl.pallas_call(
        kern,
        grid=(DEC_BATCH // SMP_GB, C_HEADS // GLA_HPS),
        in_specs=[_any_spec()] * n_alias + [cols(kw, A_CQ), cols(kw, A_CK)]
        + [cols(GLA_VW, A_CV, GLA_VBLKS, j) for j in range(GLA_VBLKS)]
        + [cols(vw, B_CZ), cols(kw, 0), state_spec,
           pl.BlockSpec((C, 128), lambda b, h: (0, 0)), _row_spec(l, C_DV)],
        out_specs=[cols(vw, 0), state_spec],
        out_shape=[
            jax.ShapeDtypeStruct((T_ALL, BRANCH), BF),
            jax.ShapeDtypeStruct((DEPTH, DEC_BATCH, C_HEADS, C_DK, C_DV), F32),
        ],
        input_output_aliases={i: i for i in range(n_alias)},
        compiler_params=_cparams(2),
    )(*alias_in, *([proj_a] * (2 + GLA_VBLKS)), proj_b, loga, s0, ones, g)


def _merged_residual(ba_ref, bb_ref, bc_ref, g0_ref, g1_ref, g2_ref, x, wb_ref, wo_ref):
    acc = _sigmoid(g0_ref[...].astype(F32)) * _dot(ba_ref[...], wb_ref[0])
    acc = acc + _sigmoid(g1_ref[...].astype(F32)) * _dot(bb_ref[...], wb_ref[1])
    acc = acc + _sigmoid(g2_ref[...].astype(F32)) * _dot(bc_ref[...], wb_ref[2])
    return x + _dot(acc.astype(BF), wo_ref[...])


def _merge_last_kernel(ba_ref, bb_ref, bc_ref, g0_ref, g1_ref, g2_ref, x_ref, wb_ref, wo_ref, yp_ref, ys_ref):
    y = _merged_residual(ba_ref, bb_ref, bc_ref, g0_ref, g1_ref, g2_ref, x_ref[...], wb_ref, wo_ref)
    i = pl.program_id(0)

    @pl.when(i < N_P_TILES)
    def _():
        yp_ref[...] = y

    @pl.when(i == N_P_TILES)
    def _():
        ys_ref[...] = y


def _merge_next_kernel(first, *refs):
    ba_ref, bb_ref, bc_ref, g0_ref, g1_ref, g2_ref = refs[:6]
    refs = refs[6:]
    if first:
        x = _pick_rows(refs[0], refs[1])
        refs = refs[2:]
    else:
        x = refs[0][...]
        refs = refs[1:]
    (wb_ref, wo_ref, g_ref, wlr_ref, wg_ref, bg_ref, wb_next_ref, wo_next_ref,
     y_ref, h_ref, loga_ref, wb_bf_ref, wo_bf_ref) = refs
    wb_bf_ref[...] = wb_next_ref[...].astype(BF)
    wo_bf_ref[...] = wo_next_ref[...].astype(BF)
    y = _merged_residual(ba_ref, bb_ref, bc_ref, g0_ref, g1_ref, g2_ref, x, wb_ref, wo_ref)
    y_ref[...] = y
    h, loga = _norm_gate(y, g_ref, wlr_ref, wg_ref, bg_ref)
    h_ref[...] = h
    loga_ref[...] = loga


def _merge_out(ba, bb, bc, proj_b, x, l, wb_bf, wo_bf, w_branch, w_out, norm_g, w_lr, w_g, b_gate):
    first, last = l == 0, l + 1 == DEPTH
    resident = pl.Buffered(1)
    w_specs = [
        pl.BlockSpec((N_BRANCH, BRANCH, D_MODEL), lambda i: (0, 0, 0), pipeline_mode=resident),
        pl.BlockSpec((D_MODEL, D_MODEL), lambda i: (0, 0), pipeline_mode=resident),
    ]
    row = lambda w: pl.BlockSpec((TM_OUT, w), lambda i: (i, 0))
    gate = lambda n: pl.BlockSpec((TM_OUT, D_MODEL), lambda i: (i, B_G // D_MODEL + n))
    mix_specs = [row(BRANCH)] * N_BRANCH + [gate(n) for n in range(N_BRANCH)]
    if last:
        return pl.pallas_call(
            _merge_last_kernel,
            grid=(N_TILES,),
            in_specs=mix_specs + [row(D_MODEL)] + w_specs,
            out_specs=_split_rows_specs(D_MODEL),
            out_shape=[jax.ShapeDtypeStruct((T_P, D_MODEL), F32), jax.ShapeDtypeStruct((T_S, D_MODEL), F32)],
            compiler_params=_cparams(1),
        )(ba, bb, bc, proj_b, proj_b, proj_b, x, wb_bf, wo_bf)

    wb_rows = N_BRANCH * BRANCH // N_P_TILES
    wo_rows = D_MODEL // N_P_TILES
    piece = lambda i: jnp.minimum(i, N_P_TILES - 1)
    x_specs = _split_rows_specs(D_MODEL) if first else [row(D_MODEL)]
    kern = lambda *refs: _merge_next_kernel(first, *refs)
    outs = pl.pallas_call(
        kern,
        grid=(N_TILES,),
        in_specs=mix_specs + x_specs + w_specs + _norm_specs(l + 1) + [
            pl.BlockSpec((None, wb_rows, D_MODEL), lambda i: (l + 1, piece(i), 0)),
            pl.BlockSpec((None, wo_rows, D_MODEL), lambda i: (l + 1, piece(i), 0)),
        ],
        out_specs=[row(D_MODEL), row(D_MODEL), row(C_KEY),
                   pl.BlockSpec((wb_rows, D_MODEL), lambda i: (piece(i), 0)),
                   pl.BlockSpec((wo_rows, D_MODEL), lambda i: (piece(i), 0))],
        out_shape=[jax.ShapeDtypeStruct((T_ALL, D_MODEL), F32), jax.ShapeDtypeStruct((T_ALL, D_MODEL), BF),
                   jax.ShapeDtypeStruct((T_ALL, C_KEY), F32),
                   jax.ShapeDtypeStruct((N_BRANCH * BRANCH, D_MODEL), BF),
                   jax.ShapeDtypeStruct((D_MODEL, D_MODEL), BF)],
        compiler_params=_cparams(1),
    )(ba, bb, bc, proj_b, proj_b, proj_b, *(list(x) if first else [x]), wb_bf, wo_bf,
      norm_g, w_lr, w_g, b_gate, w_branch.reshape(DEPTH, N_BRANCH * BRANCH, D_MODEL), w_out)
    y, h, loga, wb_next, wo_next = outs
    return y, h, loga, wb_next.reshape(N_BRANCH, BRANCH, D_MODEL), wo_next


def _rope_tables(pos):
    half = A_HEAD_DIM // 2
    inv = jnp.power(ROPE_THETA, -jnp.arange(half, dtype=F32) * (2.0 / A_HEAD_DIM))
    ang = pos.astype(F32)[:, None] * inv[None, :]
    cos, sin = jnp.cos(ang), jnp.sin(ang)
    cos128 = jnp.concatenate([cos, cos, cos, cos], axis=1)
    sin128 = jnp.concatenate([-sin, sin, -sin, sin], axis=1)
    return cos128, sin128


def kernel(x_prompt, x_sample, cache_a_k, cache_a_v, state_pool, state_gla, norm_g, w_in, q_norm_g, k_norm_g,
           sinks, w_pool, pool_scale, w_gate_lr, b_gate, gla_norm_g, w_branch, w_out):
    w_lr = jnp.pad(w_in[:, :, A_COLS:A_COLS + C_GATE_RANK],
                   ((0, 0), (0, 0), (0, LANES - C_GATE_RANK))).astype(BF)
    w_g = jnp.pad(w_gate_lr, ((0, 0), (0, LANES - C_GATE_RANK), (0, 0))).astype(BF)
    wb_bf = w_branch[0].astype(BF)
    wo_bf = w_out[0].astype(BF)
    w_pool_bf = w_pool.astype(BF)
    qg = (jnp.tile(q_norm_g, (1, 256 // A_HEAD_DIM)) * (A_HEAD_DIM ** -0.5))[:, None, :]
    kg = jnp.tile(k_norm_g, (1, 256 // A_HEAD_DIM))[:, None, :]
    norm_g = norm_g[:, None, :]
    b_gate = b_gate[:, None, :]
    pool_scale = pool_scale[:, None, :]
    gla_norm_g = gla_norm_g[:, None, :]
    cos_p, sin_p = _rope_tables(jnp.arange(SEQ))
    cos_s, sin_s = _rope_tables(PAST_LEN + jnp.arange(DEC_SEQ))
    cos_s = jnp.tile(cos_s, (SMP_BS, 1))
    sin_s = jnp.tile(sin_s, (SMP_BS, 1))
    lane = np.arange(256)
    bd = jnp.asarray((lane[:, None] // A_HEAD_DIM == lane[None, :] // A_HEAD_DIM).astype(np.float32), dtype=BF)
    ones_c = jnp.ones((GLA_C, 128), BF)
    to_hd_pos = lambda c: jnp.transpose(c, (0, 1, 3, 4, 2)).reshape(DEPTH, DEC_BATCH, KV_W, WINDOW)
    from_hd_pos = lambda c: jnp.transpose(
        c.reshape(DEPTH, DEC_BATCH, A_KV_HEADS, A_HEAD_DIM, WINDOW), (0, 1, 4, 2, 3))
    cache_k = to_hd_pos(cache_a_k)
    cache_v = to_hd_pos(cache_a_v)
    pool_prev = jnp.pad(state_pool, ((0, 0), (0, 0), (1, 0), (0, 0)))

    w_t = jnp.swapaxes(w_in, 1, 2)
    x = (x_prompt.reshape(T_P, D_MODEL), x_sample.reshape(T_S, D_MODEL))
    h, loga = _norm(x[0], x[1], 0, norm_g, w_lr, w_g, b_gate)
    pk, pv, pp, sp = [], [], [], []
    nk_buf = nv_buf = sp_buf = ss_buf = None
    for l in range(DEPTH):
        proj_a = _inproj_a(h, w_t, l)
        proj_b = _inproj_b(h, w_t, l)

        oa, knew_p = _attn_prompt(proj_a, l, sinks, cos_p, sin_p, bd, qg, kg)
        oa, nk_buf, nv_buf = _attn_sample(oa, nk_buf, nv_buf, proj_a, l, cache_k, cache_v, sinks,
                                          cos_s, sin_s, bd, qg, kg)
        ob, tail_p = _pool_prompt(proj_a, l, w_pool_bf, pool_scale)
        ob, tail_s = _pool_sample(ob, proj_a, l, pool_prev, w_pool_bf, pool_scale)
        oc, sp_buf = _gla_prompt(sp_buf, proj_a, proj_b, loga, l, gla_norm_g)
        oc, ss_buf = _gla_sample(oc, ss_buf, proj_a, proj_b, loga, l, state_gla, ones_c, gla_norm_g)

        outs = _merge_out(oa, ob, oc, proj_b, x, l, wb_bf, wo_bf, w_branch, w_out, norm_g, w_lr, w_g, b_gate)
        if l + 1 < DEPTH:
            x, h, loga, wb_bf, wo_bf = outs
        else:
            yp, ys = outs

        vp = proj_a[:T_P, A_AV:A_AV + KV_W].reshape(BATCH, SEQ, KV_W)[:, SEQ - WINDOW:].astype(F32)
        pk.append(knew_p)
        pv.append(vp)
        pp.append(tail_p[:, 1:])
        sp.append(tail_s[:, 1:])

    kv5 = lambda a, nb: a.reshape(DEPTH, nb, WINDOW, A_KV_HEADS, A_HEAD_DIM)
    yp = yp.reshape(BATCH, SEQ, D_MODEL)
    ys = ys.reshape(DEC_BATCH, DEC_SEQ, D_MODEL)
    return (yp, ys, kv5(jnp.stack(pk), BATCH), kv5(jnp.stack(pv), BATCH), jnp.stack(pp), sp_buf,
            from_hd_pos(nk_buf), from_hd_pos(nv_buf), jnp.stack(sp), ss_buf)
```

```python
import numpy as np
import jax
import jax.numpy as jnp
from jax import lax
from jax.experimental import pallas as pl
from jax.experimental.pallas import tpu as pltpu

F32 = jnp.float32
BF = jnp.bfloat16

D_MODEL = 2048
BATCH = 4
SEQ = 2048
DEPTH = 4
DEC_BATCH = 32
DEC_SEQ = 8
PAST_LEN = 16384
BRANCH = D_MODEL // 2
N_BRANCH = 3
A_HEADS = 16
A_KV_HEADS = 4
A_HEAD_DIM = 64
A_GROUP = A_HEADS // A_KV_HEADS
WINDOW = 128
ROPE_THETA = 10000.0
POOL_WINDOWS = (2, 4, 8, 16)
POOL_GROUP_DIM = 256
POOL_HIST = 15
C_HEADS = 4
C_KEY = 512
C_DK = 128
C_DV = 256
C_GATE_RANK = 16
C_GATE_TAU = 16.0
EPS = 1e-6
KV_W = A_KV_HEADS * A_HEAD_DIM
IN_COLS = 13840

T_P = BATCH * SEQ
T_S = DEC_BATCH * DEC_SEQ
T_ALL = T_P + T_S

A_AQ, A_AK, A_AV, A_AZ = 0, 1024, 1280, 1536
A_BU, A_BZ = 2560, 3584
A_CQ, A_CK, A_CV = 4608, 5120, 5632
A_COLS = 6656
LR_SHIFT = C_GATE_RANK
B_COLS = IN_COLS - A_COLS - LR_SHIFT
B_G, B_CZ = 0, N_BRANCH * D_MODEL

LANES = 128
TM = 1056
TN_A = 1664
TN_B = 1024
W_SUB = 512
CAST_ROWS = 128
TM_OUT = 256
N_P_TILES = T_P // TM_OUT
N_TILES = T_ALL // TM_OUT
ATT_BLK = WINDOW
ATT_NB = 1
POOL_TP = 1024
POOL_HALO = 16
GLA_C = 64
GLA_NSUB = 8
GLA_SC = GLA_C * GLA_NSUB
GLA_ATT = 256
GLA_HPS = 4
GLA_VW = 512
GLA_VBLKS = GLA_HPS * C_DV // GLA_VW
SMP_BS = 4
SMP_GB = 8
VMEM_LIMIT = 56 * 1024 * 1024
NEG_BIG = -1e30


def _cparams(n_axes):
    return pltpu.CompilerParams(
        dimension_semantics=("arbitrary",) * n_axes, vmem_limit_bytes=VMEM_LIMIT)


def _sigmoid(x):
    return 0.5 * jnp.tanh(0.5 * x) + 0.5


def _silu(x):
    return x * _sigmoid(x)


def _dot(a, b):
    return jnp.dot(a, b, preferred_element_type=F32)


def _dot_nt(a, b):
    return lax.dot_general(a, b, (((1,), (1,)), ((), ())), preferred_element_type=F32)


def _dot_tn(a, b):
    return lax.dot_general(a, b, (((0,), (0,)), ((), ())), preferred_element_type=F32)


def _split3(x):
    hi = x.astype(BF)
    r1 = x - hi.astype(F32)
    mid = r1.astype(BF)
    lo = (r1 - mid.astype(F32)).astype(BF)
    return hi, mid, lo


def _any_spec():
    return pl.BlockSpec(memory_space=pl.ANY)


def _row_spec(l, width):
    return pl.BlockSpec((None, 1, width), lambda *a: (l, 0, 0))


def _norm_gate(xf, g_ref, wlr_ref, wg_ref, bg_ref):
    ms = jnp.mean(xf * xf, axis=-1, keepdims=True)
    h = (xf * lax.rsqrt(ms + EPS) * g_ref[...]).astype(BF)
    lr = _dot(h, wlr_ref[...])
    z = _dot(lr.astype(BF), wg_ref[...]) + bg_ref[...]
    log_sig = jnp.minimum(z, 0.0) - jnp.log1p(jnp.exp(-jnp.abs(z)))
    return h, log_sig * (1.0 / C_GATE_TAU)


def _norm_specs(l):
    return [
        _row_spec(l, D_MODEL),
        pl.BlockSpec((None, D_MODEL, LANES), lambda *a: (l, 0, 0)),
        pl.BlockSpec((None, LANES, C_KEY), lambda *a: (l, 0, 0)),
        _row_spec(l, C_KEY),
    ]


def _split_rows_specs(width):
    return [pl.BlockSpec((TM_OUT, width), lambda i: (jnp.minimum(i, N_P_TILES - 1), 0)),
            pl.BlockSpec((TM_OUT, width), lambda i: (0, 0))]


def _pick_rows(xp_ref, xs_ref):
    return jnp.where(pl.program_id(0) < N_P_TILES, xp_ref[...], xs_ref[...])


def _norm_kernel(xp_ref, xs_ref, g_ref, wlr_ref, wg_ref, bg_ref, h_ref, loga_ref):
    h, loga = _norm_gate(_pick_rows(xp_ref, xs_ref), g_ref, wlr_ref, wg_ref, bg_ref)
    h_ref[...] = h
    loga_ref[...] = loga


def _norm(xp, xs, l, norm_g, w_lr, w_g, b_gate):
    return pl.pallas_call(
        _norm_kernel,
        grid=(N_TILES,),
        in_specs=_split_rows_specs(D_MODEL) + _norm_specs(l),
        out_specs=[pl.BlockSpec((TM_OUT, D_MODEL), lambda i: (i, 0)),
                   pl.BlockSpec((TM_OUT, C_KEY), lambda i: (i, 0))],
        out_shape=[jax.ShapeDtypeStruct((T_ALL, D_MODEL), BF),
                   jax.ShapeDtypeStruct((T_ALL, C_KEY), F32)],
        compiler_params=_cparams(1),
    )(xp, xs, norm_g, w_lr, w_g, b_gate)


def _inproj_a_kernel(h_ref, w_ref, o_ref, w_scr):
    @pl.when(pl.program_id(1) == 0)
    def _():
        for r in range(0, TN_A, CAST_ROWS):
            w_scr[r:r + CAST_ROWS, :] = w_ref[r:r + CAST_ROWS, :].astype(BF)

    o_ref[...] = _dot_nt(h_ref[...], w_scr[...]).astype(o_ref.dtype)


def _inproj_a(h, w_t, l):
    return pl.pallas_call(
        _inproj_a_kernel,
        grid=(A_COLS // TN_A, T_ALL // TM),
        in_specs=[
            pl.BlockSpec((TM, D_MODEL), lambda j, i: (i, 0)),
            pl.BlockSpec((None, TN_A, D_MODEL), lambda j, i: (l, j, 0)),
        ],
        out_specs=pl.BlockSpec((TM, TN_A), lambda j, i: (i, j)),
        out_shape=jax.ShapeDtypeStruct((T_ALL, A_COLS), BF),
        scratch_shapes=[pltpu.VMEM((TN_A, D_MODEL), BF)],
        compiler_params=_cparams(2),
    )(h, w_t)


def _copy_cast_rows(dst, dst0, src, src0, n):
    for r in range(0, n, CAST_ROWS):
        m = min(CAST_ROWS, n - r)
        dst[dst0 + r:dst0 + r + m, :] = src[src0 + r:src0 + r + m, :].astype(BF)


def _inproj_b_kernel(h_ref, w0_ref, w1_ref, w2_ref, o_ref, w_scr):
    @pl.when(pl.program_id(1) == 0)
    def _():
        _copy_cast_rows(w_scr, 0, w0_ref, LR_SHIFT, W_SUB - LR_SHIFT)
        _copy_cast_rows(w_scr, W_SUB - LR_SHIFT, w1_ref, 0, W_SUB)
        _copy_cast_rows(w_scr, 2 * W_SUB - LR_SHIFT, w2_ref, 0, LR_SHIFT)

    o_ref[...] = _dot_nt(h_ref[...], w_scr[...]).astype(o_ref.dtype)


def _inproj_b(h, w_t, l):
    assert TN_B == 2 * W_SUB
    nt = B_COLS // TN_B
    sub0 = A_COLS // W_SUB
    tail0 = (A_COLS + TN_B) // LR_SHIFT
    return pl.pallas_call(
        _inproj_b_kernel,
        grid=(nt, T_ALL // TM),
        in_specs=[
            pl.BlockSpec((TM, D_MODEL), lambda j, i: (i, 0)),
            pl.BlockSpec((None, W_SUB, D_MODEL), lambda j, i: (l, sub0 + 2 * j, 0)),
            pl.BlockSpec((None, W_SUB, D_MODEL), lambda j, i: (l, sub0 + 2 * j + 1, 0)),
            pl.BlockSpec((None, LR_SHIFT, D_MODEL), lambda j, i: (l, tail0 + (TN_B // LR_SHIFT) * j, 0)),
        ],
        out_specs=pl.BlockSpec((TM, TN_B), lambda j, i: (i, (j + nt - 1) % nt)),
        out_shape=jax.ShapeDtypeStruct((T_ALL, B_COLS), BF),
        scratch_shapes=[pltpu.VMEM((TN_B, D_MODEL), BF)],
        compiler_params=_cparams(2),
    )(h, w_t, w_t, w_t)


def _head_norm(x, bd, g_row):
    outs = []
    for c in range(x.shape[1] // 256):
        xc = x[:, c * 256:(c + 1) * 256]
        ss = _dot((xc * xc).astype(BF), bd)
        outs.append(xc * lax.rsqrt(ss * (1.0 / A_HEAD_DIM) + EPS) * g_row)
    return outs


def _rope128(xc, cos, sin, first_half):
    swapped = jnp.where(first_half, pltpu.roll(xc, 96, 1), pltpu.roll(xc, 32, 1))
    return xc * cos + swapped * sin


def _norm_rope(x, bd, g_row, cos, sin, first_half):
    chunks = []
    for blk in _head_norm(x, bd, g_row):
        for c in range(2):
            chunks.append(_rope128(blk[:, c * 128:(c + 1) * 128], cos, sin, first_half))
    return chunks


def _dup_half(chunk, rolled, lo_half, use_low):
    return jnp.where(lo_half, chunk, rolled) if use_low else jnp.where(lo_half, rolled, chunk)


def _attn_prompt_kernel(l, sinks_ref, q_ref, z0_ref, z1_ref, k_ref, v_ref, cos_ref, sin_ref, bd_ref, qg_ref,
                        kg_ref, o_ref, knew_ref, vnew_ref, kprev, vprev_e, vprev_o):
    n = pl.program_id(1)
    R = ATT_BLK
    RR = ATT_NB * R
    lane_all = lax.broadcasted_iota(jnp.int32, (RR, 128), 1)
    first_half = (lane_all & 32) == 0
    lane = lax.broadcasted_iota(jnp.int32, (R, 128), 1)
    lo_half = lane < 64
    hi_half = jnp.logical_not(lo_half)
    cos = cos_ref[...]
    sin = sin_ref[...]
    bd = bd_ref[...]

    @pl.when(n == 0)
    def _():
        kprev[...] = jnp.zeros_like(kprev)
        vprev_e[...] = jnp.zeros_like(vprev_e)
        vprev_o[...] = jnp.zeros_like(vprev_o)

    kch_all = _norm_rope(k_ref[...].astype(F32), bd, kg_ref[...], cos, sin, first_half)
    knew_ref[0] = jnp.concatenate([kc[RR - R:] for kc in kch_all], axis=1)
    vraw = v_ref[...].astype(F32)
    vnew_ref[0] = vraw[RR - R:]
    vch_all = [vraw[:, :128], vraw[:, 128:]]
    keys = [[kprev[h] for h in range(A_KV_HEADS)]]
    vals_e = [[vprev_e[h] for h in range(A_KV_HEADS)]]
    vals_o = [[vprev_o[h] for h in range(A_KV_HEADS)]]
    for blk in range(ATT_NB):
        rows = slice(blk * R, (blk + 1) * R)
        kcur, vcur_e, vcur_o = [], [], []
        for h in range(A_KV_HEADS):
            c, low = h // 2, (h % 2 == 0)
            kc, vc = kch_all[c][rows], vch_all[c][rows]
            kcur.append(_dup_half(kc, pltpu.roll(kc, 64, 1), lo_half, low).astype(BF))
            v_lo = vc if low else pltpu.roll(vc, 64, 1)
            v_hi = pltpu.roll(vc, 64, 1) if low else vc
            vcur_e.append(jnp.where(lo_half, v_lo, 1.0).astype(BF))
            vcur_o.append(jnp.where(lo_half, 1.0, v_hi).astype(BF))
        keys.append(kcur)
        vals_e.append(vcur_e)
        vals_o.append(vcur_o)

    qch_all = _norm_rope(q_ref[...].astype(F32), bd, qg_ref[...], cos, sin, first_half)

    G = A_GROUP * R
    srow = lax.broadcasted_iota(jnp.int32, (G, 128), 0)
    scol = lax.broadcasted_iota(jnp.int32, (G, 128), 1)
    qi = srow & (R - 1)
    bias_prev_any = jnp.where(scol >= qi, 0.0, NEG_BIG)
    bias_prev_first = jnp.where(jnp.logical_and(scol >= qi, n > 0), 0.0, NEG_BIG)
    bias_cur = jnp.where(scol <= qi, 0.0, NEG_BIG)

    def scores(blk, h):
        rows = slice(blk * R, (blk + 1) * R)
        qa = qch_all[2 * h][rows]
        qb = qch_all[2 * h + 1][rows]
        q_stack = jnp.concatenate([jnp.where(lo_half, qa, 0.0), jnp.where(lo_half, qb, 0.0),
                                   jnp.where(hi_half, qa, 0.0), jnp.where(hi_half, qb, 0.0)],
                                  axis=0).astype(BF)
        bias_prev = bias_prev_first if blk == 0 else bias_prev_any
        return _dot_nt(q_stack, keys[blk][h]) + bias_prev, _dot_nt(q_stack, keys[blk + 1][h]) + bias_cur

    problems = [(blk, h) for blk in range(ATT_NB) for h in range(A_KV_HEADS)]
    ahead = scores(*problems[0])
    for idx, (blk, h) in enumerate(problems):
        s_p, s_c = ahead
        if idx + 1 < len(problems):
            ahead = scores(*problems[idx + 1])
        vprev_e_h, vcur_e_h = vals_e[blk][h], vals_e[blk + 1][h]
        vprev_o_h, vcur_o_h = vals_o[blk][h], vals_o[blk + 1][h]
        head_of_blk = (4 * h, 4 * h + 2, 4 * h + 1, 4 * h + 3)
        row_max = jnp.max(jnp.maximum(s_p, s_c), axis=-1, keepdims=True)
        m_blk = [jnp.maximum(row_max[bi * R:(bi + 1) * R], sinks_ref[l, hd]) for bi, hd in enumerate(head_of_blk)]
        e_sink = [jnp.exp(sinks_ref[l, hd] - m_blk[bi]) for bi, hd in enumerate(head_of_blk)]
        m = jnp.concatenate(m_blk, axis=0)
        p_p = jnp.exp(s_p - m).astype(BF)
        p_c = jnp.exp(s_c - m).astype(BF)
        half = G // 2
        pv_e = _dot(p_p[:half], vprev_e_h) + _dot(p_c[:half], vcur_e_h)
        pv_o = _dot(p_p[half:], vprev_o_h) + _dot(p_c[half:], vcur_o_h)
        out_rows = slice(blk * R, (blk + 1) * R)
        for cc in range(2):
            rows_e = slice(cc * R, (cc + 1) * R)
            ev, od = pv_e[rows_e], pv_o[rows_e]
            num = jnp.where(lo_half, ev, od)
            den = pltpu.roll(jnp.where(lo_half, od, ev), 64, 1)
            den = den + jnp.where(lo_half, e_sink[cc], e_sink[2 + cc])
            c = 2 * h + cc
            z_ref = z0_ref if c < 4 else z1_ref
            zc = z_ref[out_rows, (c % 4) * 128:(c % 4 + 1) * 128].astype(F32)
            o_ref[out_rows, c * 128:(c + 1) * 128] = (num / den * _silu(zc)).astype(o_ref.dtype)

    for h in range(A_KV_HEADS):
        kprev[h] = keys[ATT_NB][h]
        vprev_e[h] = vals_e[ATT_NB][h]
        vprev_o[h] = vals_o[ATT_NB][h]


def _attn_prompt(proj_a, l, sinks, cos, sin, bd, qg, kg):
    rows = ATT_NB * ATT_BLK
    nb = SEQ // rows
    rb = lambda b, n: b * nb + n
    kern = lambda *refs: _attn_prompt_kernel(l, *refs)
    return pl.pallas_call(
        kern,
        grid=(BATCH, nb),
        in_specs=[
            pl.BlockSpec(memory_space=pltpu.SMEM),
            pl.BlockSpec((rows, BRANCH), lambda b, n: (rb(b, n), A_AQ // BRANCH)),
            pl.BlockSpec((rows, 512), lambda b, n: (rb(b, n), A_AZ // 512)),
            pl.BlockSpec((rows, 512), lambda b, n: (rb(b, n), A_AZ // 512 + 1)),
            pl.BlockSpec((rows, KV_W), lambda b, n: (rb(b, n), A_AK // KV_W)),
            pl.BlockSpec((rows, KV_W), lambda b, n: (rb(b, n), A_AV // KV_W)),
            pl.BlockSpec((rows, 128), lambda b, n: (n, 0)),
            pl.BlockSpec((rows, 128), lambda b, n: (n, 0)),
            pl.BlockSpec((256, 256), lambda b, n: (0, 0)),
            _row_spec(l, 256),
            _row_spec(l, 256),
        ],
        out_specs=[
            pl.BlockSpec((rows, BRANCH), lambda b, n: (rb(b, n), 0)),
            pl.BlockSpec((1, ATT_BLK, KV_W), lambda b, n: (b, 0, 0)),
            pl.BlockSpec((1, ATT_BLK, KV_W), lambda b, n: (b, 0, 0)),
        ],
        out_shape=[
            jax.ShapeDtypeStruct((T_ALL, BRANCH), BF),
            jax.ShapeDtypeStruct((BATCH, WINDOW, KV_W), F32),
            jax.ShapeDtypeStruct((BATCH, WINDOW, KV_W), F32),
        ],
        scratch_shapes=[pltpu.VMEM((A_KV_HEADS, ATT_BLK, 128), BF),
                        pltpu.VMEM((A_KV_HEADS, ATT_BLK, 128), BF),
                        pltpu.VMEM((A_KV_HEADS, ATT_BLK, 128), BF)],
        compiler_params=_cparams(2),
    )(sinks, proj_a, proj_a, proj_a, proj_a, proj_a, cos, sin, bd, qg, kg)


def _attn_sample_kernel(l, n_alias, *refs):
    refs = refs[n_alias:]
    (sinks_ref, q_ref, z0_ref, z1_ref, k_ref, v_ref, ck_ref, cv_ref, cos_ref, sin_ref,
     bd_ref, qg_ref, kg_ref, o_ref, nk_ref, nv_ref, kall, vall) = refs
    L = DEC_SEQ
    R = SMP_BS * L
    lane = lax.broadcasted_iota(jnp.int32, (R, 128), 1)
    first_half = (lane & 32) == 0
    lane8 = lax.broadcasted_iota(jnp.int32, (L, 128), 1)
    lo8 = lane8 < 64
    lane_c = lax.broadcasted_iota(jnp.int32, (WINDOW, 128), 1)
    lo_c = lane_c < 64
    cos = cos_ref[...]
    sin = sin_ref[...]
    bd = bd_ref[...]

    @pl.when(pl.program_id(0) == 0)
    def _():
        kall[...] = jnp.zeros_like(kall)
        vall[...] = jnp.zeros_like(vall)

    kch = _norm_rope(k_ref[...].astype(F32), bd, kg_ref[...], cos, sin, first_half)
    vraw = v_ref[...].astype(F32)
    vch = [vraw[:, :128], vraw[:, 128:]]
    qch = _norm_rope(q_ref[...].astype(F32), bd, qg_ref[...], cos, sin, first_half)

    rows = A_GROUP * L
    srow = lax.broadcasted_iota(jnp.int32, (rows, 2 * WINDOW), 0)
    scol = lax.broadcasted_iota(jnp.int32, (rows, 2 * WINDOW), 1)
    t = srow & (L - 1)
    mask = jnp.logical_or(jnp.logical_and(scol < WINDOW, scol >= t),
                          jnp.logical_and(scol >= WINDOW, (scol - WINDOW) <= t))
    hrow = lax.broadcasted_iota(jnp.int32, (rows, 1), 0) >> (L.bit_length() - 1)
    z_all = jnp.concatenate([z0_ref[...], z1_ref[...]], axis=1).astype(F32)
    bias = jnp.where(mask, 0.0, NEG_BIG)
    pairs = [(bi, h) for bi in range(SMP_BS) for h in range(A_KV_HEADS)]
    slot = lambda bi, h: bi * A_KV_HEADS + h

    for bi in range(SMP_BS):
        r0 = bi * L
        knew = [kc[r0:r0 + L] for kc in kch]
        vnew = [vc[r0:r0 + L] for vc in vch]
        ck = ck_ref[bi].T
        cv = cv_ref[bi].T
        nk_ref[bi] = jnp.concatenate([ck[L:WINDOW], jnp.concatenate(knew, axis=1)], axis=0).T
        nv_ref[bi] = jnp.concatenate([cv[L:WINDOW], jnp.concatenate(vnew, axis=1)], axis=0).T
        for h in range(A_KV_HEADS):
            c, low = h // 2, (h % 2 == 0)
            ckc = ck[:, c * 128:(c + 1) * 128]
            cvc = cv[:, c * 128:(c + 1) * 128]
            kall[slot(bi, h), 0:WINDOW] = _dup_half(ckc, pltpu.roll(ckc, 64, 1), lo_c, low)
            vall[slot(bi, h), 0:WINDOW] = _dup_half(cvc, pltpu.roll(cvc, 64, 1), lo_c, low)
            kall[slot(bi, h), WINDOW:WINDOW + L] = _dup_half(knew[c], pltpu.roll(knew[c], 64, 1), lo8, low)
            vall[slot(bi, h), WINDOW:WINDOW + L] = _dup_half(vnew[c], pltpu.roll(vnew[c], 64, 1), lo8, low)

    scores = []
    for bi, h in pairs:
        r0 = bi * L
        qs = []
        for gi in range(A_GROUP):
            i = h * A_GROUP + gi
            sel = lo8 if i % 2 == 0 else jnp.logical_not(lo8)
            qs.append(jnp.where(sel, qch[i // 2][r0:r0 + L], 0.0))
        qm = jnp.concatenate(qs, axis=0).astype(BF)
        scores.append(_dot_nt(qm, kall[slot(bi, h)].astype(BF)) + bias)

    probs, dens = [], []
    for (bi, h), s in zip(pairs, scores):
        sink = jnp.zeros((rows, 1), F32)
        for gi in range(A_GROUP):
            sink = jnp.where(hrow == gi, sinks_ref[l, h * A_GROUP + gi], sink)
        m = jnp.maximum(jnp.max(s, axis=-1, keepdims=True), sink)
        p = jnp.exp(s - m)
        dens.append(jnp.sum(p, axis=-1, keepdims=True) + jnp.exp(sink - m))
        probs.append(p.astype(BF))

    pvs = [_dot(p, vall[slot(bi, h)].astype(BF)) / den for (bi, h), p, den in zip(pairs, probs, dens)]
    o_rows = []
    for bi in range(SMP_BS):
        o_chunks = []
        for h in range(A_KV_HEADS):
            pv = pvs[slot(bi, h)]
            for cc in range(2):
                o_chunks.append(jnp.where(lo8, pv[(2 * cc) * L:(2 * cc + 1) * L],
                                          pv[(2 * cc + 1) * L:(2 * cc + 2) * L]))
        o = jnp.concatenate(o_chunks, axis=1)
        o_rows.append(o * _silu(z_all[bi * L:(bi + 1) * L]))
    o_ref[...] = jnp.concatenate(o_rows, axis=0).astype(o_ref.dtype)


def _attn_sample(oa, nk_buf, nv_buf, proj_a, l, cache_k, cache_v, sinks, cos, sin, bd, qg, kg):
    R = SMP_BS * DEC_SEQ
    base = T_P // R
    alias_in = [oa] + ([nk_buf, nv_buf] if l > 0 else [])
    n_alias = len(alias_in)
    kern = lambda *refs: _attn_sample_kernel(l, n_alias, *refs)
    cache_spec = pl.BlockSpec((None, SMP_BS, KV_W, WINDOW), lambda b: (l, b, 0, 0))
    return pl.pallas_call(
        kern,
        grid=(DEC_BATCH // SMP_BS,),
        in_specs=[_any_spec()] * n_alias + [
            pl.BlockSpec(memory_space=pltpu.SMEM),
            pl.BlockSpec((R, BRANCH), lambda b: (base + b, A_AQ // BRANCH)),
            pl.BlockSpec((R, 512), lambda b: (base + b, A_AZ // 512)),
            pl.BlockSpec((R, 512), lambda b: (base + b, A_AZ // 512 + 1)),
            pl.BlockSpec((R, KV_W), lambda b: (base + b, A_AK // KV_W)),
            pl.BlockSpec((R, KV_W), lambda b: (base + b, A_AV // KV_W)),
            cache_spec, cache_spec,
            pl.BlockSpec((R, 128), lambda b: (0, 0)),
            pl.BlockSpec((R, 128), lambda b: (0, 0)),
            pl.BlockSpec((256, 256), lambda b: (0, 0)),
            _row_spec(l, 256),
            _row_spec(l, 256),
        ],
        out_specs=[pl.BlockSpec((R, BRANCH), lambda b: (base + b, 0)), cache_spec, cache_spec],
        out_shape=[
            jax.ShapeDtypeStruct((T_ALL, BRANCH), BF),
            jax.ShapeDtypeStruct((DEPTH, DEC_BATCH, KV_W, WINDOW), F32),
            jax.ShapeDtypeStruct((DEPTH, DEC_BATCH, KV_W, WINDOW), F32),
        ],
        scratch_shapes=[pltpu.VMEM((SMP_BS * A_KV_HEADS, 2 * WINDOW, 128), F32),
                        pltpu.VMEM((SMP_BS * A_KV_HEADS, 2 * WINDOW, 128), F32)],
        input_output_aliases={i: i for i in range(n_alias)},
        compiler_params=_cparams(1),
    )(*alias_in, sinks, proj_a, proj_a, proj_a, proj_a, proj_a, cache_k, cache_v, cos, sin, bd, qg, kg)


def _pool_compute(ext, u_rows, start_pos, wp_ref, scale_ref, z, n_rows):
    pos = start_pos + lax.broadcasted_iota(jnp.int32, (n_rows, 1), 0)
    outs = []
    for g, w in enumerate(POOL_WINDOWS):
        sl = slice(g * POOL_GROUP_DIM, (g + 1) * POOL_GROUP_DIM)
        acc = ext[:, sl]
        span = 1
        while span < w:
            acc = acc + pltpu.roll(acc, span, 0)
            span *= 2
        win_sum = acc[POOL_HALO:]
        count = jnp.minimum(w, pos + 1).astype(F32)
        d = (win_sum / count - u_rows[:, sl]).astype(BF)
        outs.append(_dot(d, wp_ref[g]))
    y = jnp.concatenate(outs, axis=1) * scale_ref[...]
    return y * _silu(z)


def _pool_prompt_kernel(u0_ref, u1_ref, z0_ref, z1_ref, wp_ref, scale_ref, o_ref, tail_ref, ext):
    t = pl.program_id(1)

    @pl.when(t == 0)
    def _():
        ext[0:POOL_HALO] = jnp.zeros((POOL_HALO, BRANCH), F32)

    u = jnp.concatenate([u0_ref[...], u1_ref[...]], axis=1).astype(F32)
    z = jnp.concatenate([z0_ref[...], z1_ref[...]], axis=1).astype(F32)
    ext[POOL_HALO:] = u
    out = _pool_compute(ext[...], u, t * POOL_TP, wp_ref, scale_ref, z, POOL_TP)
    o_ref[...] = out.astype(o_ref.dtype)
    tail = ext[POOL_TP:POOL_TP + POOL_HALO]
    tail_ref[0] = tail
    ext[0:POOL_HALO] = tail


def _pool_w_specs(l):
    ng = len(POOL_WINDOWS)
    return [pl.BlockSpec((None, ng, POOL_GROUP_DIM, POOL_GROUP_DIM), lambda *a: (l, 0, 0, 0)),
            _row_spec(l, BRANCH)]


def _pool_prompt(proj_a, l, w_pool, scale):
    nt = SEQ // POOL_TP
    half = lambda off, k: pl.BlockSpec((POOL_TP, 512), lambda b, t: (b * nt + t, off // 512 + k))
    return pl.pallas_call(
        _pool_prompt_kernel,
        grid=(BATCH, nt),
        in_specs=[half(A_BU, 0), half(A_BU, 1), half(A_BZ, 0), half(A_BZ, 1)] + _pool_w_specs(l),
        out_specs=[
            pl.BlockSpec((POOL_TP, BRANCH), lambda b, t: (b * nt + t, 0)),
            pl.BlockSpec((1, POOL_HALO, BRANCH), lambda b, t: (b, 0, 0)),
        ],
        out_shape=[
            jax.ShapeDtypeStruct((T_ALL, BRANCH), BF),
            jax.ShapeDtypeStruct((BATCH, POOL_HALO, BRANCH), F32),
        ],
        scratch_shapes=[pltpu.VMEM((POOL_HALO + POOL_TP, BRANCH), F32)],
        compiler_params=_cparams(2),
    )(proj_a, proj_a, proj_a, proj_a, w_pool, scale)


def _pool_sample_kernel(ob_in_ref, u0_ref, u1_ref, z0_ref, z1_ref, prev_ref, wp_ref, scale_ref, o_ref, tail_ref):
    del ob_in_ref
    L = DEC_SEQ
    u_all = jnp.concatenate([u0_ref[...], u1_ref[...]], axis=1).astype(F32)
    z_all = jnp.concatenate([z0_ref[...], z1_ref[...]], axis=1).astype(F32)
    outs = []
    for bi in range(SMP_BS):
        u = u_all[bi * L:(bi + 1) * L]
        ext = jnp.concatenate([prev_ref[bi], u], axis=0)
        outs.append(_pool_compute(ext, u, PAST_LEN, wp_ref, scale_ref, z_all[bi * L:(bi + 1) * L], L))
        tail_ref[bi] = ext[L:L + POOL_HALO]
    o_ref[...] = jnp.concatenate(outs, axis=0).astype(o_ref.dtype)


def _pool_sample(ob, proj_a, l, prev16, w_pool, scale):
    R = SMP_BS * DEC_SEQ
    base = T_P // R
    half = lambda off, k: pl.BlockSpec((R, 512), lambda b: (base + b, off // 512 + k))
    return pl.pallas_call(
        _pool_sample_kernel,
        grid=(DEC_BATCH // SMP_BS,),
        in_specs=[_any_spec(), half(A_BU, 0), half(A_BU, 1), half(A_BZ, 0), half(A_BZ, 1),
                  pl.BlockSpec((None, SMP_BS, POOL_HALO, BRANCH), lambda b: (l, b, 0, 0))] + _pool_w_specs(l),
        out_specs=[
            pl.BlockSpec((R, BRANCH), lambda b: (base + b, 0)),
            pl.BlockSpec((SMP_BS, POOL_HALO, BRANCH), lambda b: (b, 0, 0)),
        ],
        out_shape=[
            jax.ShapeDtypeStruct((T_ALL, BRANCH), BF),
            jax.ShapeDtypeStruct((DEC_BATCH, POOL_HALO, BRANCH), F32),
        ],
        input_output_aliases={0: 0},
        compiler_params=_cparams(1),
    )(ob, proj_a, proj_a, proj_a, proj_a, prev16, w_pool, scale)


def _chunk_cumsum(x, chunk):
    pos = lax.broadcasted_iota(jnp.int32, x.shape, 0) & (chunk - 1)
    span = 1
    while span < chunk:
        x = x + jnp.where(pos >= span, pltpu.roll(x, span, 0), 0.0)
        span *= 2
    return x


def _chunk_row(x, chunk, r):
    n = x.shape[0] // chunk
    return jnp.concatenate(
        [jnp.broadcast_to(x[c * chunk + r:c * chunk + r + 1], (chunk, x.shape[1])) for c in range(n)], axis=0)


def _gla_prompt_kernel(n_alias, *refs):
    refs = refs[n_alias:]
    q_ref, k_ref = refs[:2]
    v_refs = refs[2:2 + GLA_VBLKS]
    z_ref, la_ref, g_ref, o_ref, snew_ref, st_scr = refs[2 + GLA_VBLKS:]
    step = pl.program_id(2)

    @pl.when(step == 0)
    def _():
        st_scr[...] = jnp.zeros_like(st_scr)

    C = GLA_C
    chunks = [slice(c * C, (c + 1) * C) for c in range(GLA_NSUB)]
    row = lax.broadcasted_iota(jnp.int32, (GLA_ATT, GLA_ATT), 0)
    col = lax.broadcasted_iota(jnp.int32, (GLA_ATT, GLA_ATT), 1)
    visible = (row - col).astype(jnp.uint32) <= (row & (C - 1)).astype(jnp.uint32)
    g_row = g_ref[...]

    finals = []
    for hh in range(GLA_HPS):
        kcols = slice(hh * C_DK, (hh + 1) * C_DK)
        vcols = slice(hh * C_DV, (hh + 1) * C_DV)
        b = _chunk_cumsum(la_ref[:, kcols], C)
        b_last = _chunk_row(b, C, C - 1)
        b_mid = _chunk_row(b, C, C // 2 - 1)
        q = q_ref[:, kcols].astype(F32) * (C_DK ** -0.5)
        k = k_ref[:, kcols].astype(F32)
        v0 = (hh * C_DV) % GLA_VW
        v = v_refs[hh * C_DV // GLA_VW][:, v0:v0 + C_DV].astype(BF)
        kd = (k * jnp.exp(b_last - b)).astype(BF)
        qe = (q * jnp.exp(b)).astype(BF)
        q2 = (q * jnp.exp(b - b_mid)).astype(BF)
        k2 = (k * jnp.exp(b_mid - b)).astype(BF)

        o_parts = []
        for r0 in range(0, GLA_SC, GLA_ATT):
            rs = slice(r0, r0 + GLA_ATT)
            att = jnp.where(visible, _dot_nt(q2[rs], k2[rs]), 0.0).astype(BF)
            o_parts.append(_dot(att, v[rs]))
        o = jnp.concatenate(o_parts, axis=0)

        ut = [_dot_tn(v[sl], kd[sl]) for sl in chunks]
        st = st_scr[hh]
        st_before = []
        for c, sl in enumerate(chunks):
            st_before.append(st.astype(BF))
            st = st * jnp.exp(b[(c + 1) * C - 1:(c + 1) * C]) + ut[c]
        st_scr[hh] = st
        finals.append(st)

        o = o + jnp.concatenate([_dot_nt(qe[sl], st_before[c]) for c, sl in enumerate(chunks)], axis=0)
        ms = jnp.mean(o * o, axis=-1, keepdims=True)
        on = o * lax.rsqrt(ms + EPS) * g_row
        o_ref[:, vcols] = (on * _silu(z_ref[:, vcols].astype(F32))).astype(o_ref.dtype)

    @pl.when(step == pl.num_programs(2) - 1)
    def _():
        for hh in range(GLA_HPS):
            snew_ref[0, hh] = finals[hh].T


def _gla_prompt(s_buf, proj_a, proj_b, loga, l, g):
    ns = SEQ // GLA_SC
    rb = lambda b, h, s: b * ns + s
    alias_in = [s_buf] if l > 0 else []
    n_alias = len(alias_in)
    kern = lambda *refs: _gla_prompt_kernel(n_alias, *refs)
    kw, vw = GLA_HPS * C_DK, GLA_HPS * C_DV

    def cols(width, offset, stride=1, extra=0):
        assert offset % width == 0
        return pl.BlockSpec((GLA_SC, width), lambda b, h, s: (rb(b, h, s), offset // width + stride * h + extra))

    return pl.pallas_call(
        kern,
        grid=(BATCH, C_HEADS // GLA_HPS, ns),
        in_specs=[_any_spec()] * n_alias + [cols(kw, A_CQ), cols(kw, A_CK)]
        + [cols(GLA_VW, A_CV, GLA_VBLKS, j) for j in range(GLA_VBLKS)]
        + [cols(vw, B_CZ), cols(kw, 0), _row_spec(l, C_DV)],
        out_specs=[
            pl.BlockSpec((GLA_SC, vw), lambda b, h, s: (rb(b, h, s), h)),
            pl.BlockSpec((None, 1, GLA_HPS, C_DK, C_DV), lambda b, h, s: (l, b, h, 0, 0)),
        ],
        out_shape=[
            jax.ShapeDtypeStruct((T_ALL, BRANCH), BF),
            jax.ShapeDtypeStruct((DEPTH, BATCH, C_HEADS, C_DK, C_DV), F32),
        ],
        scratch_shapes=[pltpu.VMEM((GLA_HPS, C_DV, C_DK), F32)],
        input_output_aliases={0: 1} if l > 0 else {},
        compiler_params=_cparams(3),
    )(*alias_in, *([proj_a] * (2 + GLA_VBLKS)), proj_b, loga, g)


def _gla_sample_kernel(n_alias, *refs):
    refs = refs[n_alias:]
    q_ref, k_ref = refs[:2]
    v_refs = refs[2:2 + GLA_VBLKS]
    z_ref, la_ref, s0_ref, ones_ref, g_ref, o_ref, snew_ref = refs[2 + GLA_VBLKS:]
    C = SMP_GB * DEC_SEQ
    L = DEC_SEQ
    ones = ones_ref[...]
    g_row = g_ref[...]
    row = lax.broadcasted_iota(jnp.int32, (C, C), 0)
    col = lax.broadcasted_iota(jnp.int32, (C, C), 1)
    visible = (row - col).astype(jnp.uint32) <= (row & (L - 1)).astype(jnp.uint32)
    rowk = lax.broadcasted_iota(jnp.int32, (C, C_DK), 0) >> (L.bit_length() - 1)

    for hh in range(GLA_HPS):
        kcols = slice(hh * C_DK, (hh + 1) * C_DK)
        vcols = slice(hh * C_DV, (hh + 1) * C_DV)
        v0 = (hh * C_DV) % GLA_VW
        la = la_ref[:, kcols]
        b = _chunk_cumsum(la, L)
        b_last = _chunk_row(b, L, L - 1)
        b_mid = _chunk_row(b, L, L // 2 - 1)
        q = q_ref[:, kcols].astype(F32) * (C_DK ** -0.5)
        k = k_ref[:, kcols].astype(F32)
        v = v_refs[hh * C_DV // GLA_VW][:, v0:v0 + C_DV].astype(BF)
        kd = k * jnp.exp(b_last - b)
        qe = q * jnp.exp(b)

        o_inter = jnp.zeros((C, C_DV), F32)
        for gi in range(SMP_GB):
            own = rowk == gi
            s0 = s0_ref[gi, hh]
            o_inter = o_inter + _dot(jnp.where(own, qe, 0.0).astype(BF), s0.astype(BF))
            u = _dot_tn(jnp.where(own, kd, 0.0).astype(BF), v)
            own_parts = _split3(jnp.where(own, la, 0.0))
            dsum = _dot_tn(own_parts[0], ones) + _dot_tn(own_parts[1], ones) + _dot_tn(own_parts[2], ones)
            d = jnp.exp(dsum)
            snew_ref[gi, hh] = jnp.concatenate([d, d], axis=1) * s0 + u

        q2 = (q * jnp.exp(b - b_mid)).astype(BF)
        k2 = (k * jnp.exp(b_mid - b)).astype(BF)
        att = jnp.where(visible, _dot_nt(q2, k2), 0.0).astype(BF)
        o = o_inter + _dot(att, v)
        ms = jnp.mean(o * o, axis=-1, keepdims=True)
        on = o * lax.rsqrt(ms + EPS) * g_row
        o_ref[:, vcols] = (on * _silu(z_ref[:, vcols].astype(F32))).astype(o_ref.dtype)


def _gla_sample(oc, s_buf, proj_a, proj_b, loga, l, s0, ones, g):
    C = SMP_GB * DEC_SEQ
    base = T_P // C
    alias_in = [oc] + ([s_buf] if l > 0 else [])
    n_alias = len(alias_in)
    kern = lambda *refs: _gla_sample_kernel(n_alias, *refs)
    kw, vw = GLA_HPS * C_DK, GLA_HPS * C_DV
    state_spec = pl.BlockSpec((None, SMP_GB, GLA_HPS, C_DK, C_DV), lambda b, h: (l, b, h, 0, 0))

    def cols(width, offset, stride=1, extra=0):
        assert offset % width == 0
        return pl.BlockSpec((C, width), lambda b, h: (base + b, offset // width + stride * h + extra))

    return pl.pallas_call(
        kern,
        grid=(DEC_BATCH // SMP_GB, C_HEADS // GLA_HPS),
        in_specs=[_any_spec()] * n_alias + [cols(kw, A_CQ), cols(kw, A_CK)]
        + [cols(GLA_VW, A_CV, GLA_VBLKS, j) for j in range(GLA_VBLKS)]
        + [cols(vw, B_CZ), cols(kw, 0), state_spec,
           pl.BlockSpec((C, 128), lambda b, h: (0, 0)), _row_spec(l, C_DV)],
        out_specs=[cols(vw, 0), state_spec],
        out_shape=[
            jax.ShapeDtypeStruct((T_ALL, BRANCH), BF),
            jax.ShapeDtypeStruct((DEPTH, DEC_BATCH, C_HEADS, C_DK, C_DV), F32),
        ],
        input_output_aliases={i: i for i in range(n_alias)},
        compiler_params=_cparams(2),
    )(*alias_in, *([proj_a] * (2 + GLA_VBLKS)), proj_b, loga, s0, ones, g)


def _merged_residual(ba_ref, bb_ref, bc_ref, g0_ref, g1_ref, g2_ref, x, wb_ref, wo_ref):
    acc = _sigmoid(g0_ref[...].astype(F32)) * _dot(ba_ref[...], wb_ref[0])
    acc = acc + _sigmoid(g1_ref[...].astype(F32)) * _dot(bb_ref[...], wb_ref[1])
    acc = acc + _sigmoid(g2_ref[...].astype(F32)) * _dot(bc_ref[...], wb_ref[2])
    return x + _dot(acc.astype(BF), wo_ref[...])


def _merge_last_kernel(ba_ref, bb_ref, bc_ref, g0_ref, g1_ref, g2_ref, x_ref, wb_ref, wo_ref, yp_ref, ys_ref):
    y = _merged_residual(ba_ref, bb_ref, bc_ref, g0_ref, g1_ref, g2_ref, x_ref[...], wb_ref, wo_ref)
    i = pl.program_id(0)

    @pl.when(i < N_P_TILES)
    def _():
        yp_ref[...] = y

    @pl.when(i == N_P_TILES)
    def _():
        ys_ref[...] = y


def _merge_next_kernel(first, *refs):
    ba_ref, bb_ref, bc_ref, g0_ref, g1_ref, g2_ref = refs[:6]
    refs = refs[6:]
    if first:
        x = _pick_rows(refs[0], refs[1])
        refs = refs[2:]
    else:
        x = refs[0][...]
        refs = refs[1:]
    (wb_ref, wo_ref, g_ref, wlr_ref, wg_ref, bg_ref, wb_next_ref, wo_next_ref,
     y_ref, h_ref, loga_ref, wb_bf_ref, wo_bf_ref) = refs
    wb_bf_ref[...] = wb_next_ref[...].astype(BF)
    wo_bf_ref[...] = wo_next_ref[...].astype(BF)
    y = _merged_residual(ba_ref, bb_ref, bc_ref, g0_ref, g1_ref, g2_ref, x, wb_ref, wo_ref)
    y_ref[...] = y
    h, loga = _norm_gate(y, g_ref, wlr_ref, wg_ref, bg_ref)
    h_ref[...] = h
    loga_ref[...] = loga


def _merge_out(ba, bb, bc, proj_b, x, l, wb_bf, wo_bf, w_branch, w_out, norm_g, w_lr, w_g, b_gate):
    first, last = l == 0, l + 1 == DEPTH
    resident = pl.Buffered(1)
    w_specs = [
        pl.BlockSpec((N_BRANCH, BRANCH, D_MODEL), lambda i: (0, 0, 0), pipeline_mode=resident),
        pl.BlockSpec((D_MODEL, D_MODEL), lambda i: (0, 0), pipeline_mode=resident),
    ]
    row = lambda w: pl.BlockSpec((TM_OUT, w), lambda i: (i, 0))
    gate = lambda n: pl.BlockSpec((TM_OUT, D_MODEL), lambda i: (i, B_G // D_MODEL + n))
    mix_specs = [row(BRANCH)] * N_BRANCH + [gate(n) for n in range(N_BRANCH)]
    if last:
        return pl.pallas_call(
            _merge_last_kernel,
            grid=(N_TILES,),
            in_specs=mix_specs + [row(D_MODEL)] + w_specs,
            out_specs=_split_rows_specs(D_MODEL),
            out_shape=[jax.ShapeDtypeStruct((T_P, D_MODEL), F32), jax.ShapeDtypeStruct((T_S, D_MODEL), F32)],
            compiler_params=_cparams(1),
        )(ba, bb, bc, proj_b, proj_b, proj_b, x, wb_bf, wo_bf)

    wb_rows = N_BRANCH * BRANCH // N_P_TILES
    wo_rows = D_MODEL // N_P_TILES
    piece = lambda i: jnp.minimum(i, N_P_TILES - 1)
    x_specs = _split_rows_specs(D_MODEL) if first else [row(D_MODEL)]
    kern = lambda *refs: _merge_next_kernel(first, *refs)
    outs = pl.pallas_call(
        kern,
        grid=(N_TILES,),
        in_specs=mix_specs + x_specs + w_specs + _norm_specs(l + 1) + [
            pl.BlockSpec((None, wb_rows, D_MODEL), lambda i: (l + 1, piece(i), 0)),
            pl.BlockSpec((None, wo_rows, D_MODEL), lambda i: (l + 1, piece(i), 0)),
        ],
        out_specs=[row(D_MODEL), row(D_MODEL), row(C_KEY),
                   pl.BlockSpec((wb_rows, D_MODEL), lambda i: (piece(i), 0)),
                   pl.BlockSpec((wo_rows, D_MODEL), lambda i: (piece(i), 0))],
        out_shape=[jax.ShapeDtypeStruct((T_ALL, D_MODEL), F32), jax.ShapeDtypeStruct((T_ALL, D_MODEL), BF),
                   jax.ShapeDtypeStruct((T_ALL, C_KEY), F32),
                   jax.ShapeDtypeStruct((N_BRANCH * BRANCH, D_MODEL), BF),
                   jax.ShapeDtypeStruct((D_MODEL, D_MODEL), BF)],
        compiler_params=_cparams(1),
    )(ba, bb, bc, proj_b, proj_b, proj_b, *(list(x) if first else [x]), wb_bf, wo_bf,
      norm_g, w_lr, w_g, b_gate, w_branch.reshape(DEPTH, N_BRANCH * BRANCH, D_MODEL), w_out)
    y, h, loga, wb_next, wo_next = outs
    return y, h, loga, wb_next.reshape(N_BRANCH, BRANCH, D_MODEL), wo_next


def _rope_tables(pos):
    half = A_HEAD_DIM // 2
    inv = jnp.power(ROPE_THETA, -jnp.arange(half, dtype=F32) * (2.0 / A_HEAD_DIM))
    ang = pos.astype(F32)[:, None] * inv[None, :]
    cos, sin = jnp.cos(ang), jnp.sin(ang)
    cos128 = jnp.concatenate([cos, cos, cos, cos], axis=1)
    sin128 = jnp.concatenate([-sin, sin, -sin, sin], axis=1)
    return cos128, sin128


def kernel(x_prompt, x_sample, cache_a_k, cache_a_v, state_pool, state_gla, norm_g, w_in, q_norm_g, k_norm_g,
           sinks, w_pool, pool_scale, w_gate_lr, b_gate, gla_norm_g, w_branch, w_out):
    w_lr = jnp.pad(w_in[:, :, A_COLS:A_COLS + C_GATE_RANK],
                   ((0, 0), (0, 0), (0, LANES - C_GATE_RANK))).astype(BF)
    w_g = jnp.pad(w_gate_lr, ((0, 0), (0, LANES - C_GATE_RANK), (0, 0))).astype(BF)
    wb_bf = w_branch[0].astype(BF)
    wo_bf = w_out[0].astype(BF)
    w_pool_bf = w_pool.astype(BF)
    qg = (jnp.tile(q_norm_g, (1, 256 // A_HEAD_DIM)) * (A_HEAD_DIM ** -0.5))[:, None, :]
    kg = jnp.tile(k_norm_g, (1, 256 // A_HEAD_DIM))[:, None, :]
    norm_g = norm_g[:, None, :]
    b_gate = b_gate[:, None, :]
    pool_scale = pool_scale[:, None, :]
    gla_norm_g = gla_norm_g[:, None, :]
    cos_p, sin_p = _rope_tables(jnp.arange(SEQ))
    cos_s, sin_s = _rope_tables(PAST_LEN + jnp.arange(DEC_SEQ))
    cos_s = jnp.tile(cos_s, (SMP_BS, 1))
    sin_s = jnp.tile(sin_s, (SMP_BS, 1))
    lane = np.arange(256)
    bd = jnp.asarray((lane[:, None] // A_HEAD_DIM == lane[None, :] // A_HEAD_DIM).astype(np.float32), dtype=BF)
    ones_c = jnp.ones((GLA_C, 128), BF)
    to_hd_pos = lambda c: jnp.transpose(c, (0, 1, 3, 4, 2)).reshape(DEPTH, DEC_BATCH, KV_W, WINDOW)
    from_hd_pos = lambda c: jnp.transpose(
        c.reshape(DEPTH, DEC_BATCH, A_KV_HEADS, A_HEAD_DIM, WINDOW), (0, 1, 4, 2, 3))
    cache_k = to_hd_pos(cache_a_k)
    cache_v = to_hd_pos(cache_a_v)
    pool_prev = jnp.pad(state_pool, ((0, 0), (0, 0), (1, 0), (0, 0)))

    w_t = jnp.swapaxes(w_in, 1, 2)
    x = (x_prompt.reshape(T_P, D_MODEL), x_sample.reshape(T_S, D_MODEL))
    h, loga = _norm(x[0], x[1], 0, norm_g, w_lr, w_g, b_gate)
    pk, pv, pp, sp = [], [], [], []
    nk_buf = nv_buf = sp_buf = ss_buf = None
    for l in range(DEPTH):
        proj_a = _inproj_a(h, w_t, l)
        proj_b = _inproj_b(h, w_t, l)

        oa, knew_p, vnew_p = _attn_prompt(proj_a, l, sinks, cos_p, sin_p, bd, qg, kg)
        oa, nk_buf, nv_buf = _attn_sample(oa, nk_buf, nv_buf, proj_a, l, cache_k, cache_v, sinks,
                                          cos_s, sin_s, bd, qg, kg)
        ob, tail_p = _pool_prompt(proj_a, l, w_pool_bf, pool_scale)
        ob, tail_s = _pool_sample(ob, proj_a, l, pool_prev, w_pool_bf, pool_scale)
        oc, sp_buf = _gla_prompt(sp_buf, proj_a, proj_b, loga, l, gla_norm_g)
        oc, ss_buf = _gla_sample(oc, ss_buf, proj_a, proj_b, loga, l, state_gla, ones_c, gla_norm_g)

        outs = _merge_out(oa, ob, oc, proj_b, x, l, wb_bf, wo_bf, w_branch, w_out, norm_g, w_lr, w_g, b_gate)
        if l + 1 < DEPTH:
            x, h, loga, wb_bf, wo_bf = outs
        else:
            yp, ys = outs

        pk.append(knew_p)
        pv.append(vnew_p)
        pp.append(tail_p[:, 1:])
        sp.append(tail_s[:, 1:])

    kv5 = lambda a, nb: a.reshape(DEPTH, nb, WINDOW, A_KV_HEADS, A_HEAD_DIM)
    yp = yp.reshape(BATCH, SEQ, D_MODEL)
    ys = ys.reshape(DEC_BATCH, DEC_SEQ, D_MODEL)
    return (yp, ys, kv5(jnp.stack(pk), BATCH), kv5(jnp.stack(pv), BATCH), jnp.stack(pp), sp_buf,
            from_hd_pos(nk_buf), from_hd_pos(nv_buf), jnp.stack(sp), ss_buf)
```

```python
import numpy as np
import jax
import jax.numpy as jnp
from jax import lax
from jax.experimental import pallas as pl
from jax.experimental.pallas import tpu as pltpu

F32 = jnp.float32
BF = jnp.bfloat16

D_MODEL = 2048
BATCH = 4
SEQ = 2048
DEPTH = 4
DEC_BATCH = 32
DEC_SEQ = 8
PAST_LEN = 16384
BRANCH = D_MODEL // 2
N_BRANCH = 3
A_HEADS = 16
A_KV_HEADS = 4
A_HEAD_DIM = 64
A_GROUP = A_HEADS // A_KV_HEADS
WINDOW = 128
ROPE_THETA = 10000.0
POOL_WINDOWS = (2, 4, 8, 16)
POOL_GROUP_DIM = 256
POOL_HIST = 15
C_HEADS = 4
C_KEY = 512
C_DK = 128
C_DV = 256
C_GATE_RANK = 16
C_GATE_TAU = 16.0
EPS = 1e-6
KV_W = A_KV_HEADS * A_HEAD_DIM
IN_COLS = 13840

T_P = BATCH * SEQ
T_S = DEC_BATCH * DEC_SEQ
T_ALL = T_P + T_S

A_AQ, A_AK, A_AV, A_AZ = 0, 1024, 1280, 1536
A_BU, A_BZ = 2560, 3584
A_CQ, A_CK, A_CV = 4608, 5120, 5632
A_COLS = 6656
LR_SHIFT = C_GATE_RANK
B_COLS = IN_COLS - A_COLS - LR_SHIFT
B_G, B_CZ = 0, N_BRANCH * D_MODEL

LANES = 128
TM = 1056
TM_B = 1408
TN_A = 1664
TN_B = 1024
W_SUB = 512
CAST_ROWS = 128
TM_OUT = 256
N_P_TILES = T_P // TM_OUT
N_TILES = T_ALL // TM_OUT
ATT_BLK = WINDOW
ATT_NB = 1
POOL_TP = 1024
POOL_HALO = 16
GLA_C = 64
GLA_NSUB = 16
GLA_SC = GLA_C * GLA_NSUB
GLA_ATT = 256
GLA_HPS = 4
GLA_VW = 512
GLA_VBLKS = GLA_HPS * C_DV // GLA_VW
SMP_BS = 8
SMP_GB = 8
VMEM_LIMIT = 56 * 1024 * 1024
NEG_BIG = -1e30


def _cparams(n_axes):
    return pltpu.CompilerParams(
        dimension_semantics=("arbitrary",) * n_axes, vmem_limit_bytes=VMEM_LIMIT)


def _sigmoid(x):
    return 0.5 * jnp.tanh(0.5 * x) + 0.5


def _silu(x):
    return x * _sigmoid(x)


def _dot(a, b):
    return jnp.dot(a, b, preferred_element_type=F32)


def _dot_nt(a, b):
    return lax.dot_general(a, b, (((1,), (1,)), ((), ())), preferred_element_type=F32)


def _dot_tn(a, b):
    return lax.dot_general(a, b, (((0,), (0,)), ((), ())), preferred_element_type=F32)


def _split3(x):
    hi = x.astype(BF)
    r1 = x - hi.astype(F32)
    mid = r1.astype(BF)
    lo = (r1 - mid.astype(F32)).astype(BF)
    return hi, mid, lo


def _any_spec():
    return pl.BlockSpec(memory_space=pl.ANY)


def _row_spec(l, width):
    return pl.BlockSpec((None, 1, width), lambda *a: (l, 0, 0))


def _norm_gate(xf, g_ref, wlr_ref, wg_ref, bg_ref):
    ms = jnp.mean(xf * xf, axis=-1, keepdims=True)
    h = (xf * lax.rsqrt(ms + EPS) * g_ref[...]).astype(BF)
    lr = _dot(h, wlr_ref[...])
    z = _dot(lr.astype(BF), wg_ref[...]) + bg_ref[...]
    log_sig = jnp.minimum(z, 0.0) - jnp.log1p(jnp.exp(-jnp.abs(z)))
    return h, log_sig * (1.0 / C_GATE_TAU)


def _norm_specs(l):
    return [
        _row_spec(l, D_MODEL),
        pl.BlockSpec((None, D_MODEL, LANES), lambda *a: (l, 0, 0)),
        pl.BlockSpec((None, LANES, C_KEY), lambda *a: (l, 0, 0)),
        _row_spec(l, C_KEY),
    ]


def _split_rows_specs(width):
    return [pl.BlockSpec((TM_OUT, width), lambda i: (jnp.minimum(i, N_P_TILES - 1), 0)),
            pl.BlockSpec((TM_OUT, width), lambda i: (0, 0))]


def _pick_rows(xp_ref, xs_ref):
    return jnp.where(pl.program_id(0) < N_P_TILES, xp_ref[...], xs_ref[...])


def _norm_kernel(xp_ref, xs_ref, g_ref, wlr_ref, wg_ref, bg_ref, h_ref, loga_ref):
    h, loga = _norm_gate(_pick_rows(xp_ref, xs_ref), g_ref, wlr_ref, wg_ref, bg_ref)
    h_ref[...] = h
    loga_ref[...] = loga


def _norm(xp, xs, l, norm_g, w_lr, w_g, b_gate):
    return pl.pallas_call(
        _norm_kernel,
        grid=(N_TILES,),
        in_specs=_split_rows_specs(D_MODEL) + _norm_specs(l),
        out_specs=[pl.BlockSpec((TM_OUT, D_MODEL), lambda i: (i, 0)),
                   pl.BlockSpec((TM_OUT, C_KEY), lambda i: (i, 0))],
        out_shape=[jax.ShapeDtypeStruct((T_ALL, D_MODEL), BF),
                   jax.ShapeDtypeStruct((T_ALL, C_KEY), F32)],
        compiler_params=_cparams(1),
    )(xp, xs, norm_g, w_lr, w_g, b_gate)


def _inproj_a_kernel(h_ref, w_ref, o_ref, w_scr):
    @pl.when(pl.program_id(1) == 0)
    def _():
        for r in range(0, TN_A, CAST_ROWS):
            w_scr[r:r + CAST_ROWS, :] = w_ref[r:r + CAST_ROWS, :].astype(BF)

    o_ref[...] = _dot_nt(h_ref[...], w_scr[...]).astype(o_ref.dtype)


def _inproj_a(h, w_t, l):
    return pl.pallas_call(
        _inproj_a_kernel,
        grid=(A_COLS // TN_A, T_ALL // TM),
        in_specs=[
            pl.BlockSpec((TM, D_MODEL), lambda j, i: (i, 0)),
            pl.BlockSpec((None, TN_A, D_MODEL), lambda j, i: (l, j, 0)),
        ],
        out_specs=pl.BlockSpec((TM, TN_A), lambda j, i: (i, j)),
        out_shape=jax.ShapeDtypeStruct((T_ALL, A_COLS), BF),
        scratch_shapes=[pltpu.VMEM((TN_A, D_MODEL), BF)],
        compiler_params=_cparams(2),
    )(h, w_t)


def _copy_cast_rows(dst, dst0, src, src0, n):
    for r in range(0, n, CAST_ROWS):
        m = min(CAST_ROWS, n - r)
        dst[dst0 + r:dst0 + r + m, :] = src[src0 + r:src0 + r + m, :].astype(BF)


def _inproj_b_kernel(h_ref, w0_ref, w1_ref, w2_ref, o_ref, w_scr):
    @pl.when(pl.program_id(1) == 0)
    def _():
        _copy_cast_rows(w_scr, 0, w0_ref, LR_SHIFT, W_SUB - LR_SHIFT)
        _copy_cast_rows(w_scr, W_SUB - LR_SHIFT, w1_ref, 0, W_SUB)
        _copy_cast_rows(w_scr, 2 * W_SUB - LR_SHIFT, w2_ref, 0, LR_SHIFT)

    o_ref[...] = _dot_nt(h_ref[...], w_scr[...]).astype(o_ref.dtype)


def _inproj_b(h, w_t, l):
    assert TN_B == 2 * W_SUB
    nt = B_COLS // TN_B
    sub0 = A_COLS // W_SUB
    tail0 = (A_COLS + TN_B) // LR_SHIFT
    return pl.pallas_call(
        _inproj_b_kernel,
        grid=(nt, T_ALL // TM_B),
        in_specs=[
            pl.BlockSpec((TM_B, D_MODEL), lambda j, i: (i, 0)),
            pl.BlockSpec((None, W_SUB, D_MODEL), lambda j, i: (l, sub0 + 2 * j, 0)),
            pl.BlockSpec((None, W_SUB, D_MODEL), lambda j, i: (l, sub0 + 2 * j + 1, 0)),
            pl.BlockSpec((None, LR_SHIFT, D_MODEL), lambda j, i: (l, tail0 + (TN_B // LR_SHIFT) * j, 0)),
        ],
        out_specs=pl.BlockSpec((TM_B, TN_B), lambda j, i: (i, (j + nt - 1) % nt)),
        out_shape=jax.ShapeDtypeStruct((T_ALL, B_COLS), BF),
        scratch_shapes=[pltpu.VMEM((TN_B, D_MODEL), BF)],
        compiler_params=_cparams(2),
    )(h, w_t, w_t, w_t)


def _head_norm(x, bd, g_row):
    outs = []
    for c in range(x.shape[1] // 256):
        xc = x[:, c * 256:(c + 1) * 256]
        ss = _dot((xc * xc).astype(BF), bd)
        outs.append(xc * lax.rsqrt(ss * (1.0 / A_HEAD_DIM) + EPS) * g_row)
    return outs


def _rope128(xc, cos, sin, first_half):
    swapped = jnp.where(first_half, pltpu.roll(xc, 96, 1), pltpu.roll(xc, 32, 1))
    return xc * cos + swapped * sin


def _norm_rope(x, bd, g_row, cos, sin, first_half):
    chunks = []
    for blk in _head_norm(x, bd, g_row):
        for c in range(2):
            chunks.append(_rope128(blk[:, c * 128:(c + 1) * 128], cos, sin, first_half))
    return chunks


def _dup_half(chunk, rolled, lo_half, use_low):
    return jnp.where(lo_half, chunk, rolled) if use_low else jnp.where(lo_half, rolled, chunk)


def _attn_prompt_kernel(l, sinks_ref, q_ref, z0_ref, z1_ref, k_ref, v_ref, cos_ref, sin_ref, bd_ref, qg_ref,
                        kg_ref, o_ref, knew_ref, vnew_ref, kprev, vprev_e, vprev_o):
    n = pl.program_id(1)
    R = ATT_BLK
    RR = ATT_NB * R
    lane_all = lax.broadcasted_iota(jnp.int32, (RR, 128), 1)
    first_half = (lane_all & 32) == 0
    lane = lax.broadcasted_iota(jnp.int32, (R, 128), 1)
    lo_half = lane < 64
    hi_half = jnp.logical_not(lo_half)
    cos = cos_ref[...]
    sin = sin_ref[...]
    bd = bd_ref[...]

    @pl.when(n == 0)
    def _():
        kprev[...] = jnp.zeros_like(kprev)
        vprev_e[...] = jnp.zeros_like(vprev_e)
        vprev_o[...] = jnp.zeros_like(vprev_o)

    kch_all = _norm_rope(k_ref[...].astype(F32), bd, kg_ref[...], cos, sin, first_half)
    knew_ref[0] = jnp.concatenate([kc[RR - R:] for kc in kch_all], axis=1)
    vraw = v_ref[...].astype(F32)
    vnew_ref[0] = vraw[RR - R:]
    vch_all = [vraw[:, :128], vraw[:, 128:]]
    keys = [[kprev[h] for h in range(A_KV_HEADS)]]
    vals_e = [[vprev_e[h] for h in range(A_KV_HEADS)]]
    vals_o = [[vprev_o[h] for h in range(A_KV_HEADS)]]
    for blk in range(ATT_NB):
        rows = slice(blk * R, (blk + 1) * R)
        kcur, vcur_e, vcur_o = [], [], []
        for h in range(A_KV_HEADS):
            c, low = h // 2, (h % 2 == 0)
            kc, vc = kch_all[c][rows], vch_all[c][rows]
            kcur.append(_dup_half(kc, pltpu.roll(kc, 64, 1), lo_half, low).astype(BF))
            v_lo = vc if low else pltpu.roll(vc, 64, 1)
            v_hi = pltpu.roll(vc, 64, 1) if low else vc
            vcur_e.append(jnp.where(lo_half, v_lo, 1.0).astype(BF))
            vcur_o.append(jnp.where(lo_half, 1.0, v_hi).astype(BF))
        keys.append(kcur)
        vals_e.append(vcur_e)
        vals_o.append(vcur_o)

    qch_all = _norm_rope(q_ref[...].astype(F32), bd, qg_ref[...], cos, sin, first_half)

    G = A_GROUP * R
    srow = lax.broadcasted_iota(jnp.int32, (G, 128), 0)
    scol = lax.broadcasted_iota(jnp.int32, (G, 128), 1)
    qi = srow & (R - 1)
    bias_prev_any = jnp.where(scol >= qi, 0.0, NEG_BIG)
    bias_prev_first = jnp.where(jnp.logical_and(scol >= qi, n > 0), 0.0, NEG_BIG)
    bias_cur = jnp.where(scol <= qi, 0.0, NEG_BIG)

    def scores(blk, h):
        rows = slice(blk * R, (blk + 1) * R)
        qa = qch_all[2 * h][rows]
        qb = qch_all[2 * h + 1][rows]
        q_stack = jnp.concatenate([jnp.where(lo_half, qa, 0.0), jnp.where(lo_half, qb, 0.0),
                                   jnp.where(hi_half, qa, 0.0), jnp.where(hi_half, qb, 0.0)],
                                  axis=0).astype(BF)
        bias_prev = bias_prev_first if blk == 0 else bias_prev_any
        return _dot_nt(q_stack, keys[blk][h]) + bias_prev, _dot_nt(q_stack, keys[blk + 1][h]) + bias_cur

    problems = [(blk, h) for blk in range(ATT_NB) for h in range(A_KV_HEADS)]
    ahead = scores(*problems[0])
    for idx, (blk, h) in enumerate(problems):
        s_p, s_c = ahead
        if idx + 1 < len(problems):
            ahead = scores(*problems[idx + 1])
        vprev_e_h, vcur_e_h = vals_e[blk][h], vals_e[blk + 1][h]
        vprev_o_h, vcur_o_h = vals_o[blk][h], vals_o[blk + 1][h]
        head_of_blk = (4 * h, 4 * h + 2, 4 * h + 1, 4 * h + 3)
        row_max = jnp.max(jnp.maximum(s_p, s_c), axis=-1, keepdims=True)
        m_blk = [jnp.maximum(row_max[bi * R:(bi + 1) * R], sinks_ref[l, hd]) for bi, hd in enumerate(head_of_blk)]
        e_sink = [jnp.exp(sinks_ref[l, hd] - m_blk[bi]) for bi, hd in enumerate(head_of_blk)]
        m = jnp.concatenate(m_blk, axis=0)
        p_p = jnp.exp(s_p - m).astype(BF)
        p_c = jnp.exp(s_c - m).astype(BF)
        half = G // 2
        pv_e = _dot(p_p[:half], vprev_e_h) + _dot(p_c[:half], vcur_e_h)
        pv_o = _dot(p_p[half:], vprev_o_h) + _dot(p_c[half:], vcur_o_h)
        out_rows = slice(blk * R, (blk + 1) * R)
        for cc in range(2):
            rows_e = slice(cc * R, (cc + 1) * R)
            ev, od = pv_e[rows_e], pv_o[rows_e]
            num = jnp.where(lo_half, ev, od)
            den = pltpu.roll(jnp.where(lo_half, od, ev), 64, 1)
            den = den + jnp.where(lo_half, e_sink[cc], e_sink[2 + cc])
            c = 2 * h + cc
            z_ref = z0_ref if c < 4 else z1_ref
            zc = z_ref[out_rows, (c % 4) * 128:(c % 4 + 1) * 128].astype(F32)
            o_ref[out_rows, c * 128:(c + 1) * 128] = (num / den * _silu(zc)).astype(o_ref.dtype)

    for h in range(A_KV_HEADS):
        kprev[h] = keys[ATT_NB][h]
        vprev_e[h] = vals_e[ATT_NB][h]
        vprev_o[h] = vals_o[ATT_NB][h]


def _attn_prompt(proj_a, l, sinks, cos, sin, bd, qg, kg):
    rows = ATT_NB * ATT_BLK
    nb = SEQ // rows
    rb = lambda b, n: b * nb + n
    kern = lambda *refs: _attn_prompt_kernel(l, *refs)
    return pl.pallas_call(
        kern,
        grid=(BATCH, nb),
        in_specs=[
            pl.BlockSpec(memory_space=pltpu.SMEM),
            pl.BlockSpec((rows, BRANCH), lambda b, n: (rb(b, n), A_AQ // BRANCH)),
            pl.BlockSpec((rows, 512), lambda b, n: (rb(b, n), A_AZ // 512)),
            pl.BlockSpec((rows, 512), lambda b, n: (rb(b, n), A_AZ // 512 + 1)),
            pl.BlockSpec((rows, KV_W), lambda b, n: (rb(b, n), A_AK // KV_W)),
            pl.BlockSpec((rows, KV_W), lambda b, n: (rb(b, n), A_AV // KV_W)),
            pl.BlockSpec((rows, 128), lambda b, n: (n, 0)),
            pl.BlockSpec((rows, 128), lambda b, n: (n, 0)),
            pl.BlockSpec((256, 256), lambda b, n: (0, 0)),
            _row_spec(l, 256),
            _row_spec(l, 256),
        ],
        out_specs=[
            pl.BlockSpec((rows, BRANCH), lambda b, n: (rb(b, n), 0)),
            pl.BlockSpec((1, ATT_BLK, KV_W), lambda b, n: (b, 0, 0)),
            pl.BlockSpec((1, ATT_BLK, KV_W), lambda b, n: (b, 0, 0)),
        ],
        out_shape=[
            jax.ShapeDtypeStruct((T_ALL, BRANCH), BF),
            jax.ShapeDtypeStruct((BATCH, WINDOW, KV_W), F32),
            jax.ShapeDtypeStruct((BATCH, WINDOW, KV_W), F32),
        ],
        scratch_shapes=[pltpu.VMEM((A_KV_HEADS, ATT_BLK, 128), BF),
                        pltpu.VMEM((A_KV_HEADS, ATT_BLK, 128), BF),
                        pltpu.VMEM((A_KV_HEADS, ATT_BLK, 128), BF)],
        compiler_params=_cparams(2),
    )(sinks, proj_a, proj_a, proj_a, proj_a, proj_a, cos, sin, bd, qg, kg)


def _attn_sample_kernel(l, n_alias, *refs):
    refs = refs[n_alias:]
    (sinks_ref, q_ref, z0_ref, z1_ref, k_ref, v_ref, ck_ref, cv_ref, cos_ref, sin_ref,
     bd_ref, qg_ref, kg_ref, o_ref, nk_ref, nv_ref, kall, vall) = refs
    L = DEC_SEQ
    R = SMP_BS * L
    lane = lax.broadcasted_iota(jnp.int32, (R, 128), 1)
    first_half = (lane & 32) == 0
    lane8 = lax.broadcasted_iota(jnp.int32, (L, 128), 1)
    lo8 = lane8 < 64
    lane_c = lax.broadcasted_iota(jnp.int32, (WINDOW, 128), 1)
    lo_c = lane_c < 64
    cos = cos_ref[...]
    sin = sin_ref[...]
    bd = bd_ref[...]

    @pl.when(pl.program_id(0) == 0)
    def _():
        kall[...] = jnp.zeros_like(kall)
        vall[...] = jnp.zeros_like(vall)

    kch = _norm_rope(k_ref[...].astype(F32), bd, kg_ref[...], cos, sin, first_half)
    vraw = v_ref[...].astype(F32)
    vch = [vraw[:, :128], vraw[:, 128:]]
    qch = _norm_rope(q_ref[...].astype(F32), bd, qg_ref[...], cos, sin, first_half)

    rows = A_GROUP * L
    srow = lax.broadcasted_iota(jnp.int32, (rows, 2 * WINDOW), 0)
    scol = lax.broadcasted_iota(jnp.int32, (rows, 2 * WINDOW), 1)
    t = srow & (L - 1)
    mask = jnp.logical_or(jnp.logical_and(scol < WINDOW, scol >= t),
                          jnp.logical_and(scol >= WINDOW, (scol - WINDOW) <= t))
    hrow = lax.broadcasted_iota(jnp.int32, (rows, 1), 0) >> (L.bit_length() - 1)
    z_all = jnp.concatenate([z0_ref[...], z1_ref[...]], axis=1).astype(F32)
    bias = jnp.where(mask, 0.0, NEG_BIG)
    pairs = [(bi, h) for bi in range(SMP_BS) for h in range(A_KV_HEADS)]
    slot = lambda bi, h: bi * A_KV_HEADS + h

    for bi in range(SMP_BS):
        r0 = bi * L
        knew = [kc[r0:r0 + L] for kc in kch]
        vnew = [vc[r0:r0 + L] for vc in vch]
        ck = ck_ref[bi].T
        cv = cv_ref[bi].T
        nk_ref[bi] = jnp.concatenate([ck[L:WINDOW], jnp.concatenate(knew, axis=1)], axis=0).T
        nv_ref[bi] = jnp.concatenate([cv[L:WINDOW], jnp.concatenate(vnew, axis=1)], axis=0).T
        for h in range(A_KV_HEADS):
            c, low = h // 2, (h % 2 == 0)
            ckc = ck[:, c * 128:(c + 1) * 128]
            cvc = cv[:, c * 128:(c + 1) * 128]
            kall[slot(bi, h), 0:WINDOW] = _dup_half(ckc, pltpu.roll(ckc, 64, 1), lo_c, low)
            vall[slot(bi, h), 0:WINDOW] = _dup_half(cvc, pltpu.roll(cvc, 64, 1), lo_c, low)
            kall[slot(bi, h), WINDOW:WINDOW + L] = _dup_half(knew[c], pltpu.roll(knew[c], 64, 1), lo8, low)
            vall[slot(bi, h), WINDOW:WINDOW + L] = _dup_half(vnew[c], pltpu.roll(vnew[c], 64, 1), lo8, low)

    scores = []
    for bi, h in pairs:
        r0 = bi * L
        qs = []
        for gi in range(A_GROUP):
            i = h * A_GROUP + gi
            sel = lo8 if i % 2 == 0 else jnp.logical_not(lo8)
            qs.append(jnp.where(sel, qch[i // 2][r0:r0 + L], 0.0))
        qm = jnp.concatenate(qs, axis=0).astype(BF)
        scores.append(_dot_nt(qm, kall[slot(bi, h)].astype(BF)) + bias)

    probs, dens = [], []
    for (bi, h), s in zip(pairs, scores):
        sink = jnp.zeros((rows, 1), F32)
        for gi in range(A_GROUP):
            sink = jnp.where(hrow == gi, sinks_ref[l, h * A_GROUP + gi], sink)
        m = jnp.maximum(jnp.max(s, axis=-1, keepdims=True), sink)
        p = jnp.exp(s - m)
        dens.append(jnp.sum(p, axis=-1, keepdims=True) + jnp.exp(sink - m))
        probs.append(p.astype(BF))

    pvs = [_dot(p, vall[slot(bi, h)].astype(BF)) / den for (bi, h), p, den in zip(pairs, probs, dens)]
    o_rows = []
    for bi in range(SMP_BS):
        o_chunks = []
        for h in range(A_KV_HEADS):
            pv = pvs[slot(bi, h)]
            for cc in range(2):
                o_chunks.append(jnp.where(lo8, pv[(2 * cc) * L:(2 * cc + 1) * L],
                                          pv[(2 * cc + 1) * L:(2 * cc + 2) * L]))
        o = jnp.concatenate(o_chunks, axis=1)
        o_rows.append(o * _silu(z_all[bi * L:(bi + 1) * L]))
    o_ref[...] = jnp.concatenate(o_rows, axis=0).astype(o_ref.dtype)


def _attn_sample(oa, nk_buf, nv_buf, proj_a, l, cache_k, cache_v, sinks, cos, sin, bd, qg, kg):
    R = SMP_BS * DEC_SEQ
    base = T_P // R
    alias_in = [oa] + ([nk_buf, nv_buf] if l > 0 else [])
    n_alias = len(alias_in)
    kern = lambda *refs: _attn_sample_kernel(l, n_alias, *refs)
    cache_spec = pl.BlockSpec((None, SMP_BS, KV_W, WINDOW), lambda b: (l, b, 0, 0))
    return pl.pallas_call(
        kern,
        grid=(DEC_BATCH // SMP_BS,),
        in_specs=[_any_spec()] * n_alias + [
            pl.BlockSpec(memory_space=pltpu.SMEM),
            pl.BlockSpec((R, BRANCH), lambda b: (base + b, A_AQ // BRANCH)),
            pl.BlockSpec((R, 512), lambda b: (base + b, A_AZ // 512)),
            pl.BlockSpec((R, 512), lambda b: (base + b, A_AZ // 512 + 1)),
            pl.BlockSpec((R, KV_W), lambda b: (base + b, A_AK // KV_W)),
            pl.BlockSpec((R, KV_W), lambda b: (base + b, A_AV // KV_W)),
            cache_spec, cache_spec,
            pl.BlockSpec((R, 128), lambda b: (0, 0)),
            pl.BlockSpec((R, 128), lambda b: (0, 0)),
            pl.BlockSpec((256, 256), lambda b: (0, 0)),
            _row_spec(l, 256),
            _row_spec(l, 256),
        ],
        out_specs=[pl.BlockSpec((R, BRANCH), lambda b: (base + b, 0)), cache_spec, cache_spec],
        out_shape=[
            jax.ShapeDtypeStruct((T_ALL, BRANCH), BF),
            jax.ShapeDtypeStruct((DEPTH, DEC_BATCH, KV_W, WINDOW), F32),
            jax.ShapeDtypeStruct((DEPTH, DEC_BATCH, KV_W, WINDOW), F32),
        ],
        scratch_shapes=[pltpu.VMEM((SMP_BS * A_KV_HEADS, 2 * WINDOW, 128), F32),
                        pltpu.VMEM((SMP_BS * A_KV_HEADS, 2 * WINDOW, 128), F32)],
        input_output_aliases={i: i for i in range(n_alias)},
        compiler_params=_cparams(1),
    )(*alias_in, sinks, proj_a, proj_a, proj_a, proj_a, proj_a, cache_k, cache_v, cos, sin, bd, qg, kg)


def _pool_compute(ext, u_rows, start_pos, wp_ref, scale_ref, z, n_rows):
    pos = start_pos + lax.broadcasted_iota(jnp.int32, (n_rows, 1), 0)
    outs = []
    for g, w in enumerate(POOL_WINDOWS):
        sl = slice(g * POOL_GROUP_DIM, (g + 1) * POOL_GROUP_DIM)
        acc = ext[:, sl]
        span = 1
        while span < w:
            acc = acc + pltpu.roll(acc, span, 0)
            span *= 2
        win_sum = acc[POOL_HALO:]
        count = jnp.minimum(w, pos + 1).astype(F32)
        d = (win_sum / count - u_rows[:, sl]).astype(BF)
        outs.append(_dot(d, wp_ref[g]))
    y = jnp.concatenate(outs, axis=1) * scale_ref[...]
    return y * _silu(z)


def _pool_prompt_kernel(u0_ref, u1_ref, z0_ref, z1_ref, wp_ref, scale_ref, o_ref, tail_ref, ext):
    t = pl.program_id(1)

    @pl.when(t == 0)
    def _():
        ext[0:POOL_HALO] = jnp.zeros((POOL_HALO, BRANCH), F32)

    u = jnp.concatenate([u0_ref[...], u1_ref[...]], axis=1).astype(F32)
    z = jnp.concatenate([z0_ref[...], z1_ref[...]], axis=1).astype(F32)
    ext[POOL_HALO:] = u
    out = _pool_compute(ext[...], u, t * POOL_TP, wp_ref, scale_ref, z, POOL_TP)
    o_ref[...] = out.astype(o_ref.dtype)
    tail = ext[POOL_TP:POOL_TP + POOL_HALO]
    tail_ref[0] = tail
    ext[0:POOL_HALO] = tail


def _pool_w_specs(l):
    ng = len(POOL_WINDOWS)
    return [pl.BlockSpec((None, ng, POOL_GROUP_DIM, POOL_GROUP_DIM), lambda *a: (l, 0, 0, 0)),
            _row_spec(l, BRANCH)]


def _pool_prompt(proj_a, l, w_pool, scale):
    nt = SEQ // POOL_TP
    half = lambda off, k: pl.BlockSpec((POOL_TP, 512), lambda b, t: (b * nt + t, off // 512 + k))
    return pl.pallas_call(
        _pool_prompt_kernel,
        grid=(BATCH, nt),
        in_specs=[half(A_BU, 0), half(A_BU, 1), half(A_BZ, 0), half(A_BZ, 1)] + _pool_w_specs(l),
        out_specs=[
            pl.BlockSpec((POOL_TP, BRANCH), lambda b, t: (b * nt + t, 0)),
            pl.BlockSpec((1, POOL_HALO, BRANCH), lambda b, t: (b, 0, 0)),
        ],
        out_shape=[
            jax.ShapeDtypeStruct((T_ALL, BRANCH), BF),
            jax.ShapeDtypeStruct((BATCH, POOL_HALO, BRANCH), F32),
        ],
        scratch_shapes=[pltpu.VMEM((POOL_HALO + POOL_TP, BRANCH), F32)],
        compiler_params=_cparams(2),
    )(proj_a, proj_a, proj_a, proj_a, w_pool, scale)


def _pool_sample_kernel(ob_in_ref, u0_ref, u1_ref, z0_ref, z1_ref, prev_ref, wp_ref, scale_ref, o_ref, tail_ref):
    del ob_in_ref
    L = DEC_SEQ
    u_all = jnp.concatenate([u0_ref[...], u1_ref[...]], axis=1).astype(F32)
    z_all = jnp.concatenate([z0_ref[...], z1_ref[...]], axis=1).astype(F32)
    outs = []
    for bi in range(SMP_BS):
        u = u_all[bi * L:(bi + 1) * L]
        ext = jnp.concatenate([prev_ref[bi], u], axis=0)
        outs.append(_pool_compute(ext, u, PAST_LEN, wp_ref, scale_ref, z_all[bi * L:(bi + 1) * L], L))
        tail_ref[bi] = ext[L:L + POOL_HALO]
    o_ref[...] = jnp.concatenate(outs, axis=0).astype(o_ref.dtype)


def _pool_sample(ob, proj_a, l, prev16, w_pool, scale):
    R = SMP_BS * DEC_SEQ
    base = T_P // R
    half = lambda off, k: pl.BlockSpec((R, 512), lambda b: (base + b, off // 512 + k))
    return pl.pallas_call(
        _pool_sample_kernel,
        grid=(DEC_BATCH // SMP_BS,),
        in_specs=[_any_spec(), half(A_BU, 0), half(A_BU, 1), half(A_BZ, 0), half(A_BZ, 1),
                  pl.BlockSpec((None, SMP_BS, POOL_HALO, BRANCH), lambda b: (l, b, 0, 0))] + _pool_w_specs(l),
        out_specs=[
            pl.BlockSpec((R, BRANCH), lambda b: (base + b, 0)),
            pl.BlockSpec((SMP_BS, POOL_HALO, BRANCH), lambda b: (b, 0, 0)),
        ],
        out_shape=[
            jax.ShapeDtypeStruct((T_ALL, BRANCH), BF),
            jax.ShapeDtypeStruct((DEC_BATCH, POOL_HALO, BRANCH), F32),
        ],
        input_output_aliases={0: 0},
        compiler_params=_cparams(1),
    )(ob, proj_a, proj_a, proj_a, proj_a, prev16, w_pool, scale)


def _chunk_cumsum(x, chunk):
    pos = lax.broadcasted_iota(jnp.int32, x.shape, 0) & (chunk - 1)
    span = 1
    while span < chunk:
        x = x + jnp.where(pos >= span, pltpu.roll(x, span, 0), 0.0)
        span *= 2
    return x


def _chunk_row(x, chunk, r):
    n = x.shape[0] // chunk
    return jnp.concatenate(
        [jnp.broadcast_to(x[c * chunk + r:c * chunk + r + 1], (chunk, x.shape[1])) for c in range(n)], axis=0)


def _gla_prompt_kernel(n_alias, *refs):
    refs = refs[n_alias:]
    q_ref, k_ref = refs[:2]
    v_refs = refs[2:2 + GLA_VBLKS]
    z_ref, la_ref, g_ref, o_ref, snew_ref, st_scr = refs[2 + GLA_VBLKS:]
    step = pl.program_id(2)

    @pl.when(step == 0)
    def _():
        st_scr[...] = jnp.zeros_like(st_scr)

    C = GLA_C
    chunks = [slice(c * C, (c + 1) * C) for c in range(GLA_NSUB)]
    row = lax.broadcasted_iota(jnp.int32, (GLA_ATT, GLA_ATT), 0)
    col = lax.broadcasted_iota(jnp.int32, (GLA_ATT, GLA_ATT), 1)
    visible = (row - col).astype(jnp.uint32) <= (row & (C - 1)).astype(jnp.uint32)
    g_row = g_ref[...]

    finals = []
    for hh in range(GLA_HPS):
        kcols = slice(hh * C_DK, (hh + 1) * C_DK)
        vcols = slice(hh * C_DV, (hh + 1) * C_DV)
        b = _chunk_cumsum(la_ref[:, kcols], C)
        b_last = _chunk_row(b, C, C - 1)
        b_mid = _chunk_row(b, C, C // 2 - 1)
        q = q_ref[:, kcols].astype(F32) * (C_DK ** -0.5)
        k = k_ref[:, kcols].astype(F32)
        v0 = (hh * C_DV) % GLA_VW
        v = v_refs[hh * C_DV // GLA_VW][:, v0:v0 + C_DV].astype(BF)
        kd = (k * jnp.exp(b_last - b)).astype(BF)
        qe = (q * jnp.exp(b)).astype(BF)
        q2 = (q * jnp.exp(b - b_mid)).astype(BF)
        k2 = (k * jnp.exp(b_mid - b)).astype(BF)

        o_parts = []
        for r0 in range(0, GLA_SC, GLA_ATT):
            rs = slice(r0, r0 + GLA_ATT)
            att = jnp.where(visible, _dot_nt(q2[rs], k2[rs]), 0.0).astype(BF)
            o_parts.append(_dot(att, v[rs]))
        o = jnp.concatenate(o_parts, axis=0)

        ut = [_dot_tn(v[sl], kd[sl]) for sl in chunks]
        st = st_scr[hh]
        st_before = []
        for c, sl in enumerate(chunks):
            st_before.append(st.astype(BF))
            st = st * jnp.exp(b[(c + 1) * C - 1:(c + 1) * C]) + ut[c]
        st_scr[hh] = st
        finals.append(st)

        o = o + jnp.concatenate([_dot_nt(qe[sl], st_before[c]) for c, sl in enumerate(chunks)], axis=0)
        ms = jnp.mean(o * o, axis=-1, keepdims=True)
        on = o * lax.rsqrt(ms + EPS) * g_row
        o_ref[:, vcols] = (on * _silu(z_ref[:, vcols].astype(F32))).astype(o_ref.dtype)

    @pl.when(step == pl.num_programs(2) - 1)
    def _():
        for hh in range(GLA_HPS):
            snew_ref[0, hh] = finals[hh].T


def _gla_prompt(s_buf, proj_a, proj_b, loga, l, g):
    ns = SEQ // GLA_SC
    rb = lambda b, h, s: b * ns + s
    alias_in = [s_buf] if l > 0 else []
    n_alias = len(alias_in)
    kern = lambda *refs: _gla_prompt_kernel(n_alias, *refs)
    kw, vw = GLA_HPS * C_DK, GLA_HPS * C_DV

    def cols(width, offset, stride=1, extra=0):
        assert offset % width == 0
        return pl.BlockSpec((GLA_SC, width), lambda b, h, s: (rb(b, h, s), offset // width + stride * h + extra))

    return pl.pallas_call(
        kern,
        grid=(BATCH, C_HEADS // GLA_HPS, ns),
        in_specs=[_any_spec()] * n_alias + [cols(kw, A_CQ), cols(kw, A_CK)]
        + [cols(GLA_VW, A_CV, GLA_VBLKS, j) for j in range(GLA_VBLKS)]
        + [cols(vw, B_CZ), cols(kw, 0), _row_spec(l, C_DV)],
        out_specs=[
            pl.BlockSpec((GLA_SC, vw), lambda b, h, s: (rb(b, h, s), h)),
            pl.BlockSpec((None, 1, GLA_HPS, C_DK, C_DV), lambda b, h, s: (l, b, h, 0, 0)),
        ],
        out_shape=[
            jax.ShapeDtypeStruct((T_ALL, BRANCH), BF),
            jax.ShapeDtypeStruct((DEPTH, BATCH, C_HEADS, C_DK, C_DV), F32),
        ],
        scratch_shapes=[pltpu.VMEM((GLA_HPS, C_DV, C_DK), F32)],
        input_output_aliases={0: 1} if l > 0 else {},
        compiler_params=_cparams(3),
    )(*alias_in, *([proj_a] * (2 + GLA_VBLKS)), proj_b, loga, g)


def _gla_sample_kernel(n_alias, *refs):
    refs = refs[n_alias:]
    q_ref, k_ref = refs[:2]
    v_refs = refs[2:2 + GLA_VBLKS]
    z_ref, la_ref, s0_ref, ones_ref, g_ref, o_ref, snew_ref = refs[2 + GLA_VBLKS:]
    C = SMP_GB * DEC_SEQ
    L = DEC_SEQ
    ones = ones_ref[...]
    g_row = g_ref[...]
    row = lax.broadcasted_iota(jnp.int32, (C, C), 0)
    col = lax.broadcasted_iota(jnp.int32, (C, C), 1)
    visible = (row - col).astype(jnp.uint32) <= (row & (L - 1)).astype(jnp.uint32)
    rowk = lax.broadcasted_iota(jnp.int32, (C, C_DK), 0) >> (L.bit_length() - 1)

    for hh in range(GLA_HPS):
        kcols = slice(hh * C_DK, (hh + 1) * C_DK)
        vcols = slice(hh * C_DV, (hh + 1) * C_DV)
        v0 = (hh * C_DV) % GLA_VW
        la = la_ref[:, kcols]
        b = _chunk_cumsum(la, L)
        b_last = _chunk_row(b, L, L - 1)
        b_mid = _chunk_row(b, L, L // 2 - 1)
        q = q_ref[:, kcols].astype(F32) * (C_DK ** -0.5)
        k = k_ref[:, kcols].astype(F32)
        v = v_refs[hh * C_DV // GLA_VW][:, v0:v0 + C_DV].astype(BF)
        kd = k * jnp.exp(b_last - b)
        qe = q * jnp.exp(b)

        o_inter = jnp.zeros((C, C_DV), F32)
        for gi in range(SMP_GB):
            own = rowk == gi
            s0 = s0_ref[gi, hh]
            o_inter = o_inter + _dot(jnp.where(own, qe, 0.0).astype(BF), s0.astype(BF))
            u = _dot_tn(jnp.where(own, kd, 0.0).astype(BF), v)
            own_parts = _split3(jnp.where(own, la, 0.0))
            dsum = _dot_tn(own_parts[0], ones) + _dot_tn(own_parts[1], ones) + _dot_tn(own_parts[2], ones)
            d = jnp.exp(dsum)
            snew_ref[gi, hh] = jnp.concatenate([d, d], axis=1) * s0 + u

        q2 = (q * jnp.exp(b - b_mid)).astype(BF)
        k2 = (k * jnp.exp(b_mid - b)).astype(BF)
        att = jnp.where(visible, _dot_nt(q2, k2), 0.0).astype(BF)
        o = o_inter + _dot(att, v)
        ms = jnp.mean(o * o, axis=-1, keepdims=True)
        on = o * lax.rsqrt(ms + EPS) * g_row
        o_ref[:, vcols] = (on * _silu(z_ref[:, vcols].astype(F32))).astype(o_ref.dtype)


def _gla_sample(oc, s_buf, proj_a, proj_b, loga, l, s0, ones, g):
    C = SMP_GB * DEC_SEQ
    base = T_P // C
    alias_in = [oc] + ([s_buf] if l > 0 else [])
    n_alias = len(alias_in)
    kern = lambda *refs: _gla_sample_kernel(n_alias, *refs)
    kw, vw = GLA_HPS * C_DK, GLA_HPS * C_DV
    state_spec = pl.BlockSpec((None, SMP_GB, GLA_HPS, C_DK, C_DV), lambda b, h: (l, b, h, 0, 0))

    def cols(width, offset, stride=1, extra=0):
        assert offset % width == 0
        return pl.BlockSpec((C, width), lambda b, h: (base + b, offset // width + stride * h + extra))

    return pl.pallas_call(
        kern,
        grid=(DEC_BATCH // SMP_GB, C_HEADS // GLA_HPS),
        in_specs=[_any_spec()] * n_alias + [cols(kw, A_CQ), cols(kw, A_CK)]
        + [cols(GLA_VW, A_CV, GLA_VBLKS, j) for j in range(GLA_VBLKS)]
        + [cols(vw, B_CZ), cols(kw, 0), state_spec,
           pl.BlockSpec((C, 128), lambda b, h: (0, 0)), _row_spec(l, C_DV)],
        out_specs=[cols(vw, 0), state_spec],
        out_shape=[
            jax.ShapeDtypeStruct((T_ALL, BRANCH), BF),
            jax.ShapeDtypeStruct((DEPTH, DEC_BATCH, C_HEADS, C_DK, C_DV), F32),
        ],
        input_output_aliases={i: i for i in range(n_alias)},
        compiler_params=_cparams(2),
    )(*alias_in, *([proj_a] * (2 + GLA_VBLKS)), proj_b, loga, s0, ones, g)


def _merged_residual(ba_ref, bb_ref, bc_ref, g0_ref, g1_ref, g2_ref, x, wb_ref, wo_ref):
    acc = _sigmoid(g0_ref[...].astype(F32)) * _dot(ba_ref[...], wb_ref[0])
    acc = acc + _sigmoid(g1_ref[...].astype(F32)) * _dot(bb_ref[...], wb_ref[1])
    acc = acc + _sigmoid(g2_ref[...].astype(F32)) * _dot(bc_ref[...], wb_ref[2])
    return x + _dot(acc.astype(BF), wo_ref[...])


def _merge_last_kernel(ba_ref, bb_ref, bc_ref, g0_ref, g1_ref, g2_ref, x_ref, wb_ref, wo_ref, yp_ref, ys_ref):
    y = _merged_residual(ba_ref, bb_ref, bc_ref, g0_ref, g1_ref, g2_ref, x_ref[...], wb_ref, wo_ref)
    i = pl.program_id(0)

    @pl.when(i < N_P_TILES)
    def _():
        yp_ref[...] = y

    @pl.when(i == N_P_TILES)
    def _():
        ys_ref[...] = y


def _merge_next_kernel(first, *refs):
    ba_ref, bb_ref, bc_ref, g0_ref, g1_ref, g2_ref = refs[:6]
    refs = refs[6:]
    if first:
        x = _pick_rows(refs[0], refs[1])
        refs = refs[2:]
    else:
        x = refs[0][...]
        refs = refs[1:]
    (wb_ref, wo_ref, g_ref, wlr_ref, wg_ref, bg_ref, wb_next_ref, wo_next_ref,
     y_ref, h_ref, loga_ref, wb_bf_ref, wo_bf_ref) = refs
    wb_bf_ref[...] = wb_next_ref[...].astype(BF)
    wo_bf_ref[...] = wo_next_ref[...].astype(BF)
    y = _merged_residual(ba_ref, bb_ref, bc_ref, g0_ref, g1_ref, g2_ref, x, wb_ref, wo_ref)
    y_ref[...] = y
    h, loga = _norm_gate(y, g_ref, wlr_ref, wg_ref, bg_ref)
    h_ref[...] = h
    loga_ref[...] = loga


def _merge_out(ba, bb, bc, proj_b, x, l, wb_bf, wo_bf, w_branch, w_out, norm_g, w_lr, w_g, b_gate):
    first, last = l == 0, l + 1 == DEPTH
    resident = pl.Buffered(1)
    w_specs = [
        pl.BlockSpec((N_BRANCH, BRANCH, D_MODEL), lambda i: (0, 0, 0), pipeline_mode=resident),
        pl.BlockSpec((D_MODEL, D_MODEL), lambda i: (0, 0), pipeline_mode=resident),
    ]
    row = lambda w: pl.BlockSpec((TM_OUT, w), lambda i: (i, 0))
    gate = lambda n: pl.BlockSpec((TM_OUT, D_MODEL), lambda i: (i, B_G // D_MODEL + n))
    mix_specs = [row(BRANCH)] * N_BRANCH + [gate(n) for n in range(N_BRANCH)]
    if last:
        return pl.pallas_call(
            _merge_last_kernel,
            grid=(N_TILES,),
            in_specs=mix_specs + [row(D_MODEL)] + w_specs,
            out_specs=_split_rows_specs(D_MODEL),
            out_shape=[jax.ShapeDtypeStruct((T_P, D_MODEL), F32), jax.ShapeDtypeStruct((T_S, D_MODEL), F32)],
            compiler_params=_cparams(1),
        )(ba, bb, bc, proj_b, proj_b, proj_b, x, wb_bf, wo_bf)

    wb_rows = N_BRANCH * BRANCH // N_P_TILES
    wo_rows = D_MODEL // N_P_TILES
    piece = lambda i: jnp.minimum(i, N_P_TILES - 1)
    x_specs = _split_rows_specs(D_MODEL) if first else [row(D_MODEL)]
    kern = lambda *refs: _merge_next_kernel(first, *refs)
    outs = pl.pallas_call(
        kern,
        grid=(N_TILES,),
        in_specs=mix_specs + x_specs + w_specs + _norm_specs(l + 1) + [
            pl.BlockSpec((None, wb_rows, D_MODEL), lambda i: (l + 1, piece(i), 0)),
            pl.BlockSpec((None, wo_rows, D_MODEL), lambda i: (l + 1, piece(i), 0)),
        ],
        out_specs=[row(D_MODEL), row(D_MODEL), row(C_KEY),
                   pl.BlockSpec((wb_rows, D_MODEL), lambda i: (piece(i), 0)),
                   pl.BlockSpec((wo_rows, D_MODEL), lambda i: (piece(i), 0))],
        out_shape=[jax.ShapeDtypeStruct((T_ALL, D_MODEL), F32), jax.ShapeDtypeStruct((T_ALL, D_MODEL), BF),
                   jax.ShapeDtypeStruct((T_ALL, C_KEY), F32),
                   jax.ShapeDtypeStruct((N_BRANCH * BRANCH, D_MODEL), BF),
                   jax.ShapeDtypeStruct((D_MODEL, D_MODEL), BF)],
        compiler_params=_cparams(1),
    )(ba, bb, bc, proj_b, proj_b, proj_b, *(list(x) if first else [x]), wb_bf, wo_bf,
      norm_g, w_lr, w_g, b_gate, w_branch.reshape(DEPTH, N_BRANCH * BRANCH, D_MODEL), w_out)
    y, h, loga, wb_next, wo_next = outs
    return y, h, loga, wb_next.reshape(N_BRANCH, BRANCH, D_MODEL), wo_next


def _rope_tables(pos):
    half = A_HEAD_DIM // 2
    inv = jnp.power(ROPE_THETA, -jnp.arange(half, dtype=F32) * (2.0 / A_HEAD_DIM))
    ang = pos.astype(F32)[:, None] * inv[None, :]
    cos, sin = jnp.cos(ang), jnp.sin(ang)
    cos128 = jnp.concatenate([cos, cos, cos, cos], axis=1)
    sin128 = jnp.concatenate([-sin, sin, -sin, sin], axis=1)
    return cos128, sin128


def kernel(x_prompt, x_sample, cache_a_k, cache_a_v, state_pool, state_gla, norm_g, w_in, q_norm_g, k_norm_g,
           sinks, w_pool, pool_scale, w_gate_lr, b_gate, gla_norm_g, w_branch, w_out):
    w_lr = jnp.pad(w_in[:, :, A_COLS:A_COLS + C_GATE_RANK],
                   ((0, 0), (0, 0), (0, LANES - C_GATE_RANK))).astype(BF)
    w_g = jnp.pad(w_gate_lr, ((0, 0), (0, LANES - C_GATE_RANK), (0, 0))).astype(BF)
    wb_bf = w_branch[0].astype(BF)
    wo_bf = w_out[0].astype(BF)
    w_pool_bf = w_pool.astype(BF)
    qg = (jnp.tile(q_norm_g, (1, 256 // A_HEAD_DIM)) * (A_HEAD_DIM ** -0.5))[:, None, :]
    kg = jnp.tile(k_norm_g, (1, 256 // A_HEAD_DIM))[:, None, :]
    norm_g = norm_g[:, None, :]
    b_gate = b_gate[:, None, :]
    pool_scale = pool_scale[:, None, :]
    gla_norm_g = gla_norm_g[:, None, :]
    cos_p, sin_p = _rope_tables(jnp.arange(SEQ))
    cos_s, sin_s = _rope_tables(PAST_LEN + jnp.arange(DEC_SEQ))
    cos_s = jnp.tile(cos_s, (SMP_BS, 1))
    sin_s = jnp.tile(sin_s, (SMP_BS, 1))
    lane = np.arange(256)
    bd = jnp.asarray((lane[:, None] // A_HEAD_DIM == lane[None, :] // A_HEAD_DIM).astype(np.float32), dtype=BF)
    ones_c = jnp.ones((GLA_C, 128), BF)
    to_hd_pos = lambda c: jnp.transpose(c, (0, 1, 3, 4, 2)).reshape(DEPTH, DEC_BATCH, KV_W, WINDOW)
    from_hd_pos = lambda c: jnp.transpose(
        c.reshape(DEPTH, DEC_BATCH, A_KV_HEADS, A_HEAD_DIM, WINDOW), (0, 1, 4, 2, 3))
    cache_k = to_hd_pos(cache_a_k)
    cache_v = to_hd_pos(cache_a_v)
    pool_prev = jnp.pad(state_pool, ((0, 0), (0, 0), (1, 0), (0, 0)))

    w_t = jnp.swapaxes(w_in, 1, 2)
    x = (x_prompt.reshape(T_P, D_MODEL), x_sample.reshape(T_S, D_MODEL))
    h, loga = _norm(x[0], x[1], 0, norm_g, w_lr, w_g, b_gate)
    pk, pv, pp, sp = [], [], [], []
    nk_buf = nv_buf = sp_buf = ss_buf = None
    for l in range(DEPTH):
        proj_a = _inproj_a(h, w_t, l)
        proj_b = _inproj_b(h, w_t, l)

        oa, knew_p, vnew_p = _attn_prompt(proj_a, l, sinks, cos_p, sin_p, bd, qg, kg)
        oa, nk_buf, nv_buf = _attn_sample(oa, nk_buf, nv_buf, proj_a, l, cache_k, cache_v, sinks,
                                          cos_s, sin_s, bd, qg, kg)
        ob, tail_p = _pool_prompt(proj_a, l, w_pool_bf, pool_scale)
        ob, tail_s = _pool_sample(ob, proj_a, l, pool_prev, w_pool_bf, pool_scale)
        oc, sp_buf = _gla_prompt(sp_buf, proj_a, proj_b, loga, l, gla_norm_g)
        oc, ss_buf = _gla_sample(oc, ss_buf, proj_a, proj_b, loga, l, state_gla, ones_c, gla_norm_g)

        outs = _merge_out(oa, ob, oc, proj_b, x, l, wb_bf, wo_bf, w_branch, w_out, norm_g, w_lr, w_g, b_gate)
        if l + 1 < DEPTH:
            x, h, loga, wb_bf, wo_bf = outs
        else:
            yp, ys = outs

        pk.append(knew_p)
        pv.append(vnew_p)
        pp.append(tail_p[:, 1:])
        sp.append(tail_s[:, 1:])

    kv5 = lambda a, nb: a.reshape(DEPTH, nb, WINDOW, A_KV_HEADS, A_HEAD_DIM)
    yp = yp.reshape(BATCH, SEQ, D_MODEL)
    ys = ys.reshape(DEC_BATCH, DEC_SEQ, D_MODEL)
    return (yp, ys, kv5(jnp.stack(pk), BATCH), kv5(jnp.stack(pv), BATCH), jnp.stack(pp), sp_buf,
            from_hd_pos(nk_buf), from_hd_pos(nv_buf), jnp.stack(sp), ss_buf)
```

```python
import numpy as np
import jax
import jax.numpy as jnp
from jax import lax
from jax.experimental import pallas as pl
from jax.experimental.pallas import tpu as pltpu

F32 = jnp.float32
BF = jnp.bfloat16

D_MODEL = 2048
BATCH = 4
SEQ = 2048
DEPTH = 4
DEC_BATCH = 32
DEC_SEQ = 8
PAST_LEN = 16384
BRANCH = D_MODEL // 2
N_BRANCH = 3
A_HEADS = 16
A_KV_HEADS = 4
A_HEAD_DIM = 64
A_GROUP = A_HEADS // A_KV_HEADS
WINDOW = 128
ROPE_THETA = 10000.0
POOL_WINDOWS = (2, 4, 8, 16)
POOL_GROUP_DIM = 256
POOL_HIST = 15
C_HEADS = 4
C_KEY = 512
C_DK = 128
C_DV = 256
C_GATE_RANK = 16
C_GATE_TAU = 16.0
EPS = 1e-6
KV_W = A_KV_HEADS * A_HEAD_DIM
IN_COLS = 13840

T_P = BATCH * SEQ
T_S = DEC_BATCH * DEC_SEQ
T_ALL = T_P + T_S

A_AQ, A_AK, A_AV, A_AZ = 0, 1024, 1280, 1536
A_BU = 2560
A1_COLS = 3584
A2_BZ, A2_CQ, A2_CK, A2_CV = 0, 1024, 1536, 2048
A_COLS = 6656
LR_SHIFT = C_GATE_RANK
B_COLS = IN_COLS - A_COLS - LR_SHIFT
B_G, B_CZ = 0, N_BRANCH * D_MODEL

LANES = 128
TM = 1056
TM_B = 1408
TN_A = 1792
TM_A = 768
TN_A2 = 1536
TN_B = 1024
W_SUB = 512
CAST_ROWS = 128
TM_OUT = 256
N_P_TILES = T_P // TM_OUT
N_TILES = T_ALL // TM_OUT
ATT_BLK = WINDOW
ATT_NB = 1
POOL_TP = 1024
POOL_HALO = 16
GLA_C = 64
GLA_NSUB = 16
GLA_SC = GLA_C * GLA_NSUB
GLA_ATT = 256
GLA_HPS = 4
GLA_VW = 512
GLA_VBLKS = GLA_HPS * C_DV // GLA_VW
SMP_BS = 8
SMP_GB = 8
VMEM_LIMIT = 56 * 1024 * 1024
NEG_BIG = -1e30


def _cparams(n_axes):
    return pltpu.CompilerParams(
        dimension_semantics=("arbitrary",) * n_axes, vmem_limit_bytes=VMEM_LIMIT)


def _sigmoid(x):
    return 0.5 * jnp.tanh(0.5 * x) + 0.5


def _silu(x):
    return x * _sigmoid(x)


def _dot(a, b):
    return jnp.dot(a, b, preferred_element_type=F32)


def _dot_nt(a, b):
    return lax.dot_general(a, b, (((1,), (1,)), ((), ())), preferred_element_type=F32)


def _dot_tn(a, b):
    return lax.dot_general(a, b, (((0,), (0,)), ((), ())), preferred_element_type=F32)


def _split3(x):
    hi = x.astype(BF)
    r1 = x - hi.astype(F32)
    mid = r1.astype(BF)
    lo = (r1 - mid.astype(F32)).astype(BF)
    return hi, mid, lo


def _any_spec():
    return pl.BlockSpec(memory_space=pl.ANY)


def _row_spec(l, width):
    return pl.BlockSpec((None, 1, width), lambda *a: (l, 0, 0))


def _norm_gate(xf, g_ref, wlr_ref, wg_ref, bg_ref):
    ms = jnp.mean(xf * xf, axis=-1, keepdims=True)
    h = (xf * lax.rsqrt(ms + EPS) * g_ref[...]).astype(BF)
    lr = _dot(h, wlr_ref[...])
    z = _dot(lr.astype(BF), wg_ref[...]) + bg_ref[...]
    log_sig = jnp.minimum(z, 0.0) - jnp.log1p(jnp.exp(-jnp.abs(z)))
    return h, log_sig * (1.0 / C_GATE_TAU)


def _norm_specs(l):
    return [
        _row_spec(l, D_MODEL),
        pl.BlockSpec((None, D_MODEL, LANES), lambda *a: (l, 0, 0)),
        pl.BlockSpec((None, LANES, C_KEY), lambda *a: (l, 0, 0)),
        _row_spec(l, C_KEY),
    ]


def _split_rows_specs(width):
    return [pl.BlockSpec((TM_OUT, width), lambda i: (jnp.minimum(i, N_P_TILES - 1), 0)),
            pl.BlockSpec((TM_OUT, width), lambda i: (0, 0))]


def _pick_rows(xp_ref, xs_ref):
    return jnp.where(pl.program_id(0) < N_P_TILES, xp_ref[...], xs_ref[...])


def _norm_kernel(xp_ref, xs_ref, g_ref, wlr_ref, wg_ref, bg_ref, h_ref, loga_ref):
    h, loga = _norm_gate(_pick_rows(xp_ref, xs_ref), g_ref, wlr_ref, wg_ref, bg_ref)
    h_ref[...] = h
    loga_ref[...] = loga


def _norm(xp, xs, l, norm_g, w_lr, w_g, b_gate):
    return pl.pallas_call(
        _norm_kernel,
        grid=(N_TILES,),
        in_specs=_split_rows_specs(D_MODEL) + _norm_specs(l),
        out_specs=[pl.BlockSpec((TM_OUT, D_MODEL), lambda i: (i, 0)),
                   pl.BlockSpec((TM_OUT, C_KEY), lambda i: (i, 0))],
        out_shape=[jax.ShapeDtypeStruct((T_ALL, D_MODEL), BF),
                   jax.ShapeDtypeStruct((T_ALL, C_KEY), F32)],
        compiler_params=_cparams(1),
    )(xp, xs, norm_g, w_lr, w_g, b_gate)


def _inproj_a_kernel(h_ref, w_ref, o_ref, w_scr):
    @pl.when(pl.program_id(1) == 0)
    def _():
        for r in range(0, TN_A, CAST_ROWS):
            w_scr[r:r + CAST_ROWS, :] = w_ref[r:r + CAST_ROWS, :].astype(BF)

    o_ref[...] = _dot_nt(h_ref[...], w_scr[...]).astype(o_ref.dtype)


def _inproj_a(h, w_t, l):
    return pl.pallas_call(
        _inproj_a_kernel,
        grid=(A1_COLS // TN_A, T_ALL // TM_A),
        in_specs=[
            pl.BlockSpec((TM_A, D_MODEL), lambda j, i: (i, 0)),
            pl.BlockSpec((None, TN_A, D_MODEL), lambda j, i: (l, j, 0)),
        ],
        out_specs=pl.BlockSpec((TM_A, TN_A), lambda j, i: (i, j)),
        out_shape=jax.ShapeDtypeStruct((T_ALL, A1_COLS), BF),
        scratch_shapes=[pltpu.VMEM((TN_A, D_MODEL), BF)],
        compiler_params=_cparams(2),
    )(h, w_t)


def _inproj_a2_kernel(h_ref, w0_ref, w1_ref, w2_ref, o_ref, w_scr):
    @pl.when(pl.program_id(1) == 0)
    def _():
        for j, w_ref in enumerate((w0_ref, w1_ref, w2_ref)):
            _copy_cast_rows(w_scr, j * W_SUB, w_ref, 0, W_SUB)

    o_ref[...] = _dot_nt(h_ref[...], w_scr[...]).astype(o_ref.dtype)


def _inproj_a2(h, w_t, l):
    assert TN_A2 == 3 * W_SUB and A1_COLS % W_SUB == 0
    sub0 = A1_COLS // W_SUB
    return pl.pallas_call(
        _inproj_a2_kernel,
        grid=((A_COLS - A1_COLS) // TN_A2, T_ALL // TM),
        in_specs=[pl.BlockSpec((TM, D_MODEL), lambda j, i: (i, 0))] + [
            pl.BlockSpec((None, W_SUB, D_MODEL), lambda j, i, k=k: (l, sub0 + 3 * j + k, 0)) for k in range(3)],
        out_specs=pl.BlockSpec((TM, TN_A2), lambda j, i: (i, j)),
        out_shape=jax.ShapeDtypeStruct((T_ALL, A_COLS - A1_COLS), BF),
        scratch_shapes=[pltpu.VMEM((TN_A2, D_MODEL), BF)],
        compiler_params=_cparams(2),
    )(h, w_t, w_t, w_t)


def _copy_cast_rows(dst, dst0, src, src0, n):
    for r in range(0, n, CAST_ROWS):
        m = min(CAST_ROWS, n - r)
        dst[dst0 + r:dst0 + r + m, :] = src[src0 + r:src0 + r + m, :].astype(BF)


def _inproj_b_kernel(h_ref, w0_ref, w1_ref, w2_ref, o_ref, w_scr):
    @pl.when(pl.program_id(1) == 0)
    def _():
        _copy_cast_rows(w_scr, 0, w0_ref, LR_SHIFT, W_SUB - LR_SHIFT)
        _copy_cast_rows(w_scr, W_SUB - LR_SHIFT, w1_ref, 0, W_SUB)
        _copy_cast_rows(w_scr, 2 * W_SUB - LR_SHIFT, w2_ref, 0, LR_SHIFT)

    o_ref[...] = _dot_nt(h_ref[...], w_scr[...]).astype(o_ref.dtype)


def _inproj_b(h, w_t, l):
    assert TN_B == 2 * W_SUB
    nt = B_COLS // TN_B
    sub0 = A_COLS // W_SUB
    tail0 = (A_COLS + TN_B) // LR_SHIFT
    return pl.pallas_call(
        _inproj_b_kernel,
        grid=(nt, T_ALL // TM_B),
        in_specs=[
            pl.BlockSpec((TM_B, D_MODEL), lambda j, i: (i, 0)),
            pl.BlockSpec((None, W_SUB, D_MODEL), lambda j, i: (l, sub0 + 2 * j, 0)),
            pl.BlockSpec((None, W_SUB, D_MODEL), lambda j, i: (l, sub0 + 2 * j + 1, 0)),
            pl.BlockSpec((None, LR_SHIFT, D_MODEL), lambda j, i: (l, tail0 + (TN_B // LR_SHIFT) * j, 0)),
        ],
        out_specs=pl.BlockSpec((TM_B, TN_B), lambda j, i: (i, (j + nt - 1) % nt)),
        out_shape=jax.ShapeDtypeStruct((T_ALL, B_COLS), BF),
        scratch_shapes=[pltpu.VMEM((TN_B, D_MODEL), BF)],
        compiler_params=_cparams(2),
    )(h, w_t, w_t, w_t)


def _head_norm(x, bd, g_row):
    outs = []
    for c in range(x.shape[1] // 256):
        xc = x[:, c * 256:(c + 1) * 256]
        ss = _dot((xc * xc).astype(BF), bd)
        outs.append(xc * lax.rsqrt(ss * (1.0 / A_HEAD_DIM) + EPS) * g_row)
    return outs


def _rope128(xc, cos, sin, first_half):
    swapped = jnp.where(first_half, pltpu.roll(xc, 96, 1), pltpu.roll(xc, 32, 1))
    return xc * cos + swapped * sin


def _norm_rope(x, bd, g_row, cos, sin, first_half):
    chunks = []
    for blk in _head_norm(x, bd, g_row):
        for c in range(2):
            chunks.append(_rope128(blk[:, c * 128:(c + 1) * 128], cos, sin, first_half))
    return chunks


def _dup_half(chunk, rolled, lo_half, use_low):
    return jnp.where(lo_half, chunk, rolled) if use_low else jnp.where(lo_half, rolled, chunk)


def _attn_prompt_kernel(l, sinks_ref, q_ref, z0_ref, z1_ref, k_ref, v_ref, cos_ref, sin_ref, bd_ref, qg_ref,
                        kg_ref, o_ref, knew_ref, vnew_ref, kprev, vprev_e, vprev_o):
    n = pl.program_id(1)
    R = ATT_BLK
    RR = ATT_NB * R
    lane_all = lax.broadcasted_iota(jnp.int32, (RR, 128), 1)
    first_half = (lane_all & 32) == 0
    lane = lax.broadcasted_iota(jnp.int32, (R, 128), 1)
    lo_half = lane < 64
    hi_half = jnp.logical_not(lo_half)
    cos = cos_ref[...]
    sin = sin_ref[...]
    bd = bd_ref[...]

    @pl.when(n == 0)
    def _():
        kprev[...] = jnp.zeros_like(kprev)
        vprev_e[...] = jnp.zeros_like(vprev_e)
        vprev_o[...] = jnp.zeros_like(vprev_o)

    kch_all = _norm_rope(k_ref[...].astype(F32), bd, kg_ref[...], cos, sin, first_half)
    knew_ref[0] = jnp.concatenate([kc[RR - R:] for kc in kch_all], axis=1)
    vraw = v_ref[...].astype(F32)
    vnew_ref[0] = vraw[RR - R:]
    vch_all = [vraw[:, :128], vraw[:, 128:]]
    keys = [[kprev[h] for h in range(A_KV_HEADS)]]
    vals_e = [[vprev_e[h] for h in range(A_KV_HEADS)]]
    vals_o = [[vprev_o[h] for h in range(A_KV_HEADS)]]
    for blk in range(ATT_NB):
        rows = slice(blk * R, (blk + 1) * R)
        kcur, vcur_e, vcur_o = [], [], []
        for h in range(A_KV_HEADS):
            c, low = h // 2, (h % 2 == 0)
            kc, vc = kch_all[c][rows], vch_all[c][rows]
            kcur.append(_dup_half(kc, pltpu.roll(kc, 64, 1), lo_half, low).astype(BF))
            v_lo = vc if low else pltpu.roll(vc, 64, 1)
            v_hi = pltpu.roll(vc, 64, 1) if low else vc
            vcur_e.append(jnp.where(lo_half, v_lo, 1.0).astype(BF))
            vcur_o.append(jnp.where(lo_half, 1.0, v_hi).astype(BF))
        keys.append(kcur)
        vals_e.append(vcur_e)
        vals_o.append(vcur_o)

    qch_all = _norm_rope(q_ref[...].astype(F32), bd, qg_ref[...], cos, sin, first_half)

    G = A_GROUP * R
    srow = lax.broadcasted_iota(jnp.int32, (G, 128), 0)
    scol = lax.broadcasted_iota(jnp.int32, (G, 128), 1)
    qi = srow & (R - 1)
    bias_prev_any = jnp.where(scol >= qi, 0.0, NEG_BIG)
    bias_prev_first = jnp.where(jnp.logical_and(scol >= qi, n > 0), 0.0, NEG_BIG)
    bias_cur = jnp.where(scol <= qi, 0.0, NEG_BIG)

    def scores(blk, h):
        rows = slice(blk * R, (blk + 1) * R)
        qa = qch_all[2 * h][rows]
        qb = qch_all[2 * h + 1][rows]
        q_stack = jnp.concatenate([jnp.where(lo_half, qa, 0.0), jnp.where(lo_half, qb, 0.0),
                                   jnp.where(hi_half, qa, 0.0), jnp.where(hi_half, qb, 0.0)],
                                  axis=0).astype(BF)
        bias_prev = bias_prev_first if blk == 0 else bias_prev_any
        return _dot_nt(q_stack, keys[blk][h]) + bias_prev, _dot_nt(q_stack, keys[blk + 1][h]) + bias_cur

    problems = [(blk, h) for blk in range(ATT_NB) for h in range(A_KV_HEADS)]
    ahead = scores(*problems[0])
    for idx, (blk, h) in enumerate(problems):
        s_p, s_c = ahead
        if idx + 1 < len(problems):
            ahead = scores(*problems[idx + 1])
        vprev_e_h, vcur_e_h = vals_e[blk][h], vals_e[blk + 1][h]
        vprev_o_h, vcur_o_h = vals_o[blk][h], vals_o[blk + 1][h]
        head_of_blk = (4 * h, 4 * h + 2, 4 * h + 1, 4 * h + 3)
        row_max = jnp.max(jnp.maximum(s_p, s_c), axis=-1, keepdims=True)
        m_blk = [jnp.maximum(row_max[bi * R:(bi + 1) * R], sinks_ref[l, hd]) for bi, hd in enumerate(head_of_blk)]
        e_sink = [jnp.exp(sinks_ref[l, hd] - m_blk[bi]) for bi, hd in enumerate(head_of_blk)]
        m = jnp.concatenate(m_blk, axis=0)
        p_p = jnp.exp(s_p - m).astype(BF)
        p_c = jnp.exp(s_c - m).astype(BF)
        half = G // 2
        pv_e = _dot(p_p[:half], vprev_e_h) + _dot(p_c[:half], vcur_e_h)
        pv_o = _dot(p_p[half:], vprev_o_h) + _dot(p_c[half:], vcur_o_h)
        out_rows = slice(blk * R, (blk + 1) * R)
        for cc in range(2):
            rows_e = slice(cc * R, (cc + 1) * R)
            ev, od = pv_e[rows_e], pv_o[rows_e]
            num = jnp.where(lo_half, ev, od)
            den = pltpu.roll(jnp.where(lo_half, od, ev), 64, 1)
            den = den + jnp.where(lo_half, e_sink[cc], e_sink[2 + cc])
            c = 2 * h + cc
            z_ref = z0_ref if c < 4 else z1_ref
            zc = z_ref[out_rows, (c % 4) * 128:(c % 4 + 1) * 128].astype(F32)
            o_ref[out_rows, c * 128:(c + 1) * 128] = (num / den * _silu(zc)).astype(o_ref.dtype)

    for h in range(A_KV_HEADS):
        kprev[h] = keys[ATT_NB][h]
        vprev_e[h] = vals_e[ATT_NB][h]
        vprev_o[h] = vals_o[ATT_NB][h]


def _attn_prompt(proj_a, l, sinks, cos, sin, bd, qg, kg):
    rows = ATT_NB * ATT_BLK
    nb = SEQ // rows
    rb = lambda b, n: b * nb + n
    kern = lambda *refs: _attn_prompt_kernel(l, *refs)
    return pl.pallas_call(
        kern,
        grid=(BATCH, nb),
        in_specs=[
            pl.BlockSpec(memory_space=pltpu.SMEM),
            pl.BlockSpec((rows, BRANCH), lambda b, n: (rb(b, n), A_AQ // BRANCH)),
            pl.BlockSpec((rows, 512), lambda b, n: (rb(b, n), A_AZ // 512)),
            pl.BlockSpec((rows, 512), lambda b, n: (rb(b, n), A_AZ // 512 + 1)),
            pl.BlockSpec((rows, KV_W), lambda b, n: (rb(b, n), A_AK // KV_W)),
            pl.BlockSpec((rows, KV_W), lambda b, n: (rb(b, n), A_AV // KV_W)),
            pl.BlockSpec((rows, 128), lambda b, n: (n, 0)),
            pl.BlockSpec((rows, 128), lambda b, n: (n, 0)),
            pl.BlockSpec((256, 256), lambda b, n: (0, 0)),
            _row_spec(l, 256),
            _row_spec(l, 256),
        ],
        out_specs=[
            pl.BlockSpec((rows, BRANCH), lambda b, n: (rb(b, n), 0)),
            pl.BlockSpec((1, ATT_BLK, KV_W), lambda b, n: (b, 0, 0)),
            pl.BlockSpec((1, ATT_BLK, KV_W), lambda b, n: (b, 0, 0)),
        ],
        out_shape=[
            jax.ShapeDtypeStruct((T_ALL, BRANCH), BF),
            jax.ShapeDtypeStruct((BATCH, WINDOW, KV_W), F32),
            jax.ShapeDtypeStruct((BATCH, WINDOW, KV_W), F32),
        ],
        scratch_shapes=[pltpu.VMEM((A_KV_HEADS, ATT_BLK, 128), BF),
                        pltpu.VMEM((A_KV_HEADS, ATT_BLK, 128), BF),
                        pltpu.VMEM((A_KV_HEADS, ATT_BLK, 128), BF)],
        compiler_params=_cparams(2),
    )(sinks, proj_a, proj_a, proj_a, proj_a, proj_a, cos, sin, bd, qg, kg)


def _attn_sample_kernel(l, n_alias, *refs):
    refs = refs[n_alias:]
    (sinks_ref, q_ref, z0_ref, z1_ref, k_ref, v_ref, ck_ref, cv_ref, cos_ref, sin_ref,
     bd_ref, qg_ref, kg_ref, o_ref, nk_ref, nv_ref, kall, vall) = refs
    L = DEC_SEQ
    R = SMP_BS * L
    lane = lax.broadcasted_iota(jnp.int32, (R, 128), 1)
    first_half = (lane & 32) == 0
    lane8 = lax.broadcasted_iota(jnp.int32, (L, 128), 1)
    lo8 = lane8 < 64
    lane_c = lax.broadcasted_iota(jnp.int32, (WINDOW, 128), 1)
    lo_c = lane_c < 64
    cos = cos_ref[...]
    sin = sin_ref[...]
    bd = bd_ref[...]

    @pl.when(pl.program_id(0) == 0)
    def _():
        kall[...] = jnp.zeros_like(kall)
        vall[...] = jnp.zeros_like(vall)

    kch = _norm_rope(k_ref[...].astype(F32), bd, kg_ref[...], cos, sin, first_half)
    vraw = v_ref[...].astype(F32)
    vch = [vraw[:, :128], vraw[:, 128:]]
    qch = _norm_rope(q_ref[...].astype(F32), bd, qg_ref[...], cos, sin, first_half)

    rows = A_GROUP * L
    srow = lax.broadcasted_iota(jnp.int32, (rows, 2 * WINDOW), 0)
    scol = lax.broadcasted_iota(jnp.int32, (rows, 2 * WINDOW), 1)
    t = srow & (L - 1)
    mask = jnp.logical_or(jnp.logical_and(scol < WINDOW, scol >= t),
                          jnp.logical_and(scol >= WINDOW, (scol - WINDOW) <= t))
    hrow = lax.broadcasted_iota(jnp.int32, (rows, 1), 0) >> (L.bit_length() - 1)
    z_all = jnp.concatenate([z0_ref[...], z1_ref[...]], axis=1).astype(F32)
    bias = jnp.where(mask, 0.0, NEG_BIG)
    pairs = [(bi, h) for bi in range(SMP_BS) for h in range(A_KV_HEADS)]
    slot = lambda bi, h: bi * A_KV_HEADS + h

    for bi in range(SMP_BS):
        r0 = bi * L
        knew = [kc[r0:r0 + L] for kc in kch]
        vnew = [vc[r0:r0 + L] for vc in vch]
        ck = ck_ref[bi].T
        cv = cv_ref[bi].T
        nk_ref[bi] = jnp.concatenate([ck[L:WINDOW], jnp.concatenate(knew, axis=1)], axis=0).T
        nv_ref[bi] = jnp.concatenate([cv[L:WINDOW], jnp.concatenate(vnew, axis=1)], axis=0).T
        for h in range(A_KV_HEADS):
            c, low = h // 2, (h % 2 == 0)
            ckc = ck[:, c * 128:(c + 1) * 128]
            cvc = cv[:, c * 128:(c + 1) * 128]
            kall[slot(bi, h), 0:WINDOW] = _dup_half(ckc, pltpu.roll(ckc, 64, 1), lo_c, low)
            vall[slot(bi, h), 0:WINDOW] = _dup_half(cvc, pltpu.roll(cvc, 64, 1), lo_c, low)
            kall[slot(bi, h), WINDOW:WINDOW + L] = _dup_half(knew[c], pltpu.roll(knew[c], 64, 1), lo8, low)
            vall[slot(bi, h), WINDOW:WINDOW + L] = _dup_half(vnew[c], pltpu.roll(vnew[c], 64, 1), lo8, low)

    scores = []
    for bi, h in pairs:
        r0 = bi * L
        qs = []
        for gi in range(A_GROUP):
            i = h * A_GROUP + gi
            sel = lo8 if i % 2 == 0 else jnp.logical_not(lo8)
            qs.append(jnp.where(sel, qch[i // 2][r0:r0 + L], 0.0))
        qm = jnp.concatenate(qs, axis=0).astype(BF)
        scores.append(_dot_nt(qm, kall[slot(bi, h)].astype(BF)) + bias)

    probs, dens = [], []
    for (bi, h), s in zip(pairs, scores):
        sink = jnp.zeros((rows, 1), F32)
        for gi in range(A_GROUP):
            sink = jnp.where(hrow == gi, sinks_ref[l, h * A_GROUP + gi], sink)
        m = jnp.maximum(jnp.max(s, axis=-1, keepdims=True), sink)
        p = jnp.exp(s - m)
        dens.append(jnp.sum(p, axis=-1, keepdims=True) + jnp.exp(sink - m))
        probs.append(p.astype(BF))

    pvs = [_dot(p, vall[slot(bi, h)].astype(BF)) / den for (bi, h), p, den in zip(pairs, probs, dens)]
    o_rows = []
    for bi in range(SMP_BS):
        o_chunks = []
        for h in range(A_KV_HEADS):
            pv = pvs[slot(bi, h)]
            for cc in range(2):
                o_chunks.append(jnp.where(lo8, pv[(2 * cc) * L:(2 * cc + 1) * L],
                                          pv[(2 * cc + 1) * L:(2 * cc + 2) * L]))
        o = jnp.concatenate(o_chunks, axis=1)
        o_rows.append(o * _silu(z_all[bi * L:(bi + 1) * L]))
    o_ref[...] = jnp.concatenate(o_rows, axis=0).astype(o_ref.dtype)


def _attn_sample(oa, nk_buf, nv_buf, proj_a, l, cache_k, cache_v, sinks, cos, sin, bd, qg, kg):
    R = SMP_BS * DEC_SEQ
    base = T_P // R
    alias_in = [oa] + ([nk_buf, nv_buf] if l > 0 else [])
    n_alias = len(alias_in)
    kern = lambda *refs: _attn_sample_kernel(l, n_alias, *refs)
    cache_spec = pl.BlockSpec((None, SMP_BS, KV_W, WINDOW), lambda b: (l, b, 0, 0))
    return pl.pallas_call(
        kern,
        grid=(DEC_BATCH // SMP_BS,),
        in_specs=[_any_spec()] * n_alias + [
            pl.BlockSpec(memory_space=pltpu.SMEM),
            pl.BlockSpec((R, BRANCH), lambda b: (base + b, A_AQ // BRANCH)),
            pl.BlockSpec((R, 512), lambda b: (base + b, A_AZ // 512)),
            pl.BlockSpec((R, 512), lambda b: (base + b, A_AZ // 512 + 1)),
            pl.BlockSpec((R, KV_W), lambda b: (base + b, A_AK // KV_W)),
            pl.BlockSpec((R, KV_W), lambda b: (base + b, A_AV // KV_W)),
            cache_spec, cache_spec,
            pl.BlockSpec((R, 128), lambda b: (0, 0)),
            pl.BlockSpec((R, 128), lambda b: (0, 0)),
            pl.BlockSpec((256, 256), lambda b: (0, 0)),
            _row_spec(l, 256),
            _row_spec(l, 256),
        ],
        out_specs=[pl.BlockSpec((R, BRANCH), lambda b: (base + b, 0)), cache_spec, cache_spec],
        out_shape=[
            jax.ShapeDtypeStruct((T_ALL, BRANCH), BF),
            jax.ShapeDtypeStruct((DEPTH, DEC_BATCH, KV_W, WINDOW), F32),
            jax.ShapeDtypeStruct((DEPTH, DEC_BATCH, KV_W, WINDOW), F32),
        ],
        scratch_shapes=[pltpu.VMEM((SMP_BS * A_KV_HEADS, 2 * WINDOW, 128), F32),
                        pltpu.VMEM((SMP_BS * A_KV_HEADS, 2 * WINDOW, 128), F32)],
        input_output_aliases={i: i for i in range(n_alias)},
        compiler_params=_cparams(1),
    )(*alias_in, sinks, proj_a, proj_a, proj_a, proj_a, proj_a, cache_k, cache_v, cos, sin, bd, qg, kg)


def _pool_compute(ext, u_rows, start_pos, wp_ref, scale_ref, z, n_rows):
    pos = start_pos + lax.broadcasted_iota(jnp.int32, (n_rows, 1), 0)
    outs = []
    for g, w in enumerate(POOL_WINDOWS):
        sl = slice(g * POOL_GROUP_DIM, (g + 1) * POOL_GROUP_DIM)
        acc = ext[:, sl]
        span = 1
        while span < w:
            acc = acc + pltpu.roll(acc, span, 0)
            span *= 2
        win_sum = acc[POOL_HALO:]
        count = jnp.minimum(w, pos + 1).astype(F32)
        d = (win_sum / count - u_rows[:, sl]).astype(BF)
        outs.append(_dot(d, wp_ref[g]))
    y = jnp.concatenate(outs, axis=1) * scale_ref[...]
    return y * _silu(z)


def _pool_prompt_kernel(u0_ref, u1_ref, z0_ref, z1_ref, wp_ref, scale_ref, o_ref, tail_ref, ext):
    t = pl.program_id(1)

    @pl.when(t == 0)
    def _():
        ext[0:POOL_HALO] = jnp.zeros((POOL_HALO, BRANCH), F32)

    u = jnp.concatenate([u0_ref[...], u1_ref[...]], axis=1).astype(F32)
    z = jnp.concatenate([z0_ref[...], z1_ref[...]], axis=1).astype(F32)
    ext[POOL_HALO:] = u
    out = _pool_compute(ext[...], u, t * POOL_TP, wp_ref, scale_ref, z, POOL_TP)
    o_ref[...] = out.astype(o_ref.dtype)
    tail = ext[POOL_TP:POOL_TP + POOL_HALO]
    tail_ref[0] = tail
    ext[0:POOL_HALO] = tail


def _pool_w_specs(l):
    ng = len(POOL_WINDOWS)
    return [pl.BlockSpec((None, ng, POOL_GROUP_DIM, POOL_GROUP_DIM), lambda *a: (l, 0, 0, 0)),
            _row_spec(l, BRANCH)]


def _pool_prompt(proj_a, proj_a2, l, w_pool, scale):
    nt = SEQ // POOL_TP
    half = lambda off, k: pl.BlockSpec((POOL_TP, 512), lambda b, t: (b * nt + t, off // 512 + k))
    return pl.pallas_call(
        _pool_prompt_kernel,
        grid=(BATCH, nt),
        in_specs=[half(A_BU, 0), half(A_BU, 1), half(A2_BZ, 0), half(A2_BZ, 1)] + _pool_w_specs(l),
        out_specs=[
            pl.BlockSpec((POOL_TP, BRANCH), lambda b, t: (b * nt + t, 0)),
            pl.BlockSpec((1, POOL_HALO, BRANCH), lambda b, t: (b, 0, 0)),
        ],
        out_shape=[
            jax.ShapeDtypeStruct((T_ALL, BRANCH), BF),
            jax.ShapeDtypeStruct((BATCH, POOL_HALO, BRANCH), F32),
        ],
        scratch_shapes=[pltpu.VMEM((POOL_HALO + POOL_TP, BRANCH), F32)],
        compiler_params=_cparams(2),
    )(proj_a, proj_a, proj_a2, proj_a2, w_pool, scale)


def _pool_sample_kernel(ob_in_ref, u0_ref, u1_ref, z0_ref, z1_ref, prev_ref, wp_ref, scale_ref, o_ref, tail_ref):
    del ob_in_ref
    L = DEC_SEQ
    u_all = jnp.concatenate([u0_ref[...], u1_ref[...]], axis=1).astype(F32)
    z_all = jnp.concatenate([z0_ref[...], z1_ref[...]], axis=1).astype(F32)
    outs = []
    for bi in range(SMP_BS):
        u = u_all[bi * L:(bi + 1) * L]
        ext = jnp.concatenate([prev_ref[bi], u], axis=0)
        outs.append(_pool_compute(ext, u, PAST_LEN, wp_ref, scale_ref, z_all[bi * L:(bi + 1) * L], L))
        tail_ref[bi] = ext[L:L + POOL_HALO]
    o_ref[...] = jnp.concatenate(outs, axis=0).astype(o_ref.dtype)


def _pool_sample(ob, proj_a, proj_a2, l, prev16, w_pool, scale):
    R = SMP_BS * DEC_SEQ
    base = T_P // R
    half = lambda off, k: pl.BlockSpec((R, 512), lambda b: (base + b, off // 512 + k))
    return pl.pallas_call(
        _pool_sample_kernel,
        grid=(DEC_BATCH // SMP_BS,),
        in_specs=[_any_spec(), half(A_BU, 0), half(A_BU, 1), half(A2_BZ, 0), half(A2_BZ, 1),
                  pl.BlockSpec((None, SMP_BS, POOL_HALO, BRANCH), lambda b: (l, b, 0, 0))] + _pool_w_specs(l),
        out_specs=[
            pl.BlockSpec((R, BRANCH), lambda b: (base + b, 0)),
            pl.BlockSpec((SMP_BS, POOL_HALO, BRANCH), lambda b: (b, 0, 0)),
        ],
        out_shape=[
            jax.ShapeDtypeStruct((T_ALL, BRANCH), BF),
            jax.ShapeDtypeStruct((DEC_BATCH, POOL_HALO, BRANCH), F32),
        ],
        input_output_aliases={0: 0},
        compiler_params=_cparams(1),
    )(ob, proj_a, proj_a, proj_a2, proj_a2, prev16, w_pool, scale)


def _chunk_cumsum(x, chunk):
    pos = lax.broadcasted_iota(jnp.int32, x.shape, 0) & (chunk - 1)
    span = 1
    while span < chunk:
        x = x + jnp.where(pos >= span, pltpu.roll(x, span, 0), 0.0)
        span *= 2
    return x


def _chunk_row(x, chunk, r):
    n = x.shape[0] // chunk
    return jnp.concatenate(
        [jnp.broadcast_to(x[c * chunk + r:c * chunk + r + 1], (chunk, x.shape[1])) for c in range(n)], axis=0)


def _gla_prompt_kernel(n_alias, *refs):
    refs = refs[n_alias:]
    q_ref, k_ref = refs[:2]
    v_refs = refs[2:2 + GLA_VBLKS]
    z_ref, la_ref, g_ref, o_ref, snew_ref, st_scr = refs[2 + GLA_VBLKS:]
    step = pl.program_id(2)

    @pl.when(step == 0)
    def _():
        st_scr[...] = jnp.zeros_like(st_scr)

    C = GLA_C
    chunks = [slice(c * C, (c + 1) * C) for c in range(GLA_NSUB)]
    row = lax.broadcasted_iota(jnp.int32, (GLA_ATT, GLA_ATT), 0)
    col = lax.broadcasted_iota(jnp.int32, (GLA_ATT, GLA_ATT), 1)
    visible = (row - col).astype(jnp.uint32) <= (row & (C - 1)).astype(jnp.uint32)
    g_row = g_ref[...]

    finals = []
    for hh in range(GLA_HPS):
        kcols = slice(hh * C_DK, (hh + 1) * C_DK)
        vcols = slice(hh * C_DV, (hh + 1) * C_DV)
        b = _chunk_cumsum(la_ref[:, kcols], C)
        b_last = _chunk_row(b, C, C - 1)
        b_mid = _chunk_row(b, C, C // 2 - 1)
        q = q_ref[:, kcols].astype(F32) * (C_DK ** -0.5)
        k = k_ref[:, kcols].astype(F32)
        v0 = (hh * C_DV) % GLA_VW
        v = v_refs[hh * C_DV // GLA_VW][:, v0:v0 + C_DV].astype(BF)
        kd = (k * jnp.exp(b_last - b)).astype(BF)
        qe = (q * jnp.exp(b)).astype(BF)
        q2 = (q * jnp.exp(b - b_mid)).astype(BF)
        k2 = (k * jnp.exp(b_mid - b)).astype(BF)

        o_parts = []
        for r0 in range(0, GLA_SC, GLA_ATT):
            rs = slice(r0, r0 + GLA_ATT)
            att = jnp.where(visible, _dot_nt(q2[rs], k2[rs]), 0.0).astype(BF)
            o_parts.append(_dot(att, v[rs]))
        o = jnp.concatenate(o_parts, axis=0)

        ut = [_dot_tn(v[sl], kd[sl]) for sl in chunks]
        st = st_scr[hh]
        st_before = []
        for c, sl in enumerate(chunks):
            st_before.append(st.astype(BF))
            st = st * jnp.exp(b[(c + 1) * C - 1:(c + 1) * C]) + ut[c]
        st_scr[hh] = st
        finals.append(st)

        o = o + jnp.concatenate([_dot_nt(qe[sl], st_before[c]) for c, sl in enumerate(chunks)], axis=0)
        ms = jnp.mean(o * o, axis=-1, keepdims=True)
        on = o * lax.rsqrt(ms + EPS) * g_row
        o_ref[:, vcols] = (on * _silu(z_ref[:, vcols].astype(F32))).astype(o_ref.dtype)

    @pl.when(step == pl.num_programs(2) - 1)
    def _():
        for hh in range(GLA_HPS):
            snew_ref[0, hh] = finals[hh].T


def _gla_prompt(s_buf, proj_a, proj_b, loga, l, g):
    ns = SEQ // GLA_SC
    rb = lambda b, h, s: b * ns + s
    alias_in = [s_buf] if l > 0 else []
    n_alias = len(alias_in)
    kern = lambda *refs: _gla_prompt_kernel(n_alias, *refs)
    kw, vw = GLA_HPS * C_DK, GLA_HPS * C_DV

    def cols(width, offset, stride=1, extra=0):
        assert offset % width == 0
        return pl.BlockSpec((GLA_SC, width), lambda b, h, s: (rb(b, h, s), offset // width + stride * h + extra))

    return pl.pallas_call(
        kern,
        grid=(BATCH, C_HEADS // GLA_HPS, ns),
        in_specs=[_any_spec()] * n_alias + [cols(kw, A2_CQ), cols(kw, A2_CK)]
        + [cols(GLA_VW, A2_CV, GLA_VBLKS, j) for j in range(GLA_VBLKS)]
        + [cols(vw, B_CZ), cols(kw, 0), _row_spec(l, C_DV)],
        out_specs=[
            pl.BlockSpec((GLA_SC, vw), lambda b, h, s: (rb(b, h, s), h)),
            pl.BlockSpec((None, 1, GLA_HPS, C_DK, C_DV), lambda b, h, s: (l, b, h, 0, 0)),
        ],
        out_shape=[
            jax.ShapeDtypeStruct((T_ALL, BRANCH), BF),
            jax.ShapeDtypeStruct((DEPTH, BATCH, C_HEADS, C_DK, C_DV), F32),
        ],
        scratch_shapes=[pltpu.VMEM((GLA_HPS, C_DV, C_DK), F32)],
        input_output_aliases={0: 1} if l > 0 else {},
        compiler_params=_cparams(3),
    )(*alias_in, *([proj_a] * (2 + GLA_VBLKS)), proj_b, loga, g)


def _gla_sample_kernel(n_alias, *refs):
    refs = refs[n_alias:]
    q_ref, k_ref = refs[:2]
    v_refs = refs[2:2 + GLA_VBLKS]
    z_ref, la_ref, s0_ref, ones_ref, g_ref, o_ref, snew_ref = refs[2 + GLA_VBLKS:]
    C = SMP_GB * DEC_SEQ
    L = DEC_SEQ
    ones = ones_ref[...]
    g_row = g_ref[...]
    row = lax.broadcasted_iota(jnp.int32, (C, C), 0)
    col = lax.broadcasted_iota(jnp.int32, (C, C), 1)
    visible = (row - col).astype(jnp.uint32) <= (row & (L - 1)).astype(jnp.uint32)
    rowk = lax.broadcasted_iota(jnp.int32, (C, C_DK), 0) >> (L.bit_length() - 1)

    for hh in range(GLA_HPS):
        kcols = slice(hh * C_DK, (hh + 1) * C_DK)
        vcols = slice(hh * C_DV, (hh + 1) * C_DV)
        v0 = (hh * C_DV) % GLA_VW
        la = la_ref[:, kcols]
        b = _chunk_cumsum(la, L)
        b_last = _chunk_row(b, L, L - 1)
        b_mid = _chunk_row(b, L, L // 2 - 1)
        q = q_ref[:, kcols].astype(F32) * (C_DK ** -0.5)
        k = k_ref[:, kcols].astype(F32)
        v = v_refs[hh * C_DV // GLA_VW][:, v0:v0 + C_DV].astype(BF)
        kd = k * jnp.exp(b_last - b)
        qe = q * jnp.exp(b)

        o_inter = jnp.zeros((C, C_DV), F32)
        for gi in range(SMP_GB):
            own = rowk == gi
            s0 = s0_ref[gi, hh]
            o_inter = o_inter + _dot(jnp.where(own, qe, 0.0).astype(BF), s0.astype(BF))
            u = _dot_tn(jnp.where(own, kd, 0.0).astype(BF), v)
            own_parts = _split3(jnp.where(own, la, 0.0))
            dsum = _dot_tn(own_parts[0], ones) + _dot_tn(own_parts[1], ones) + _dot_tn(own_parts[2], ones)
            d = jnp.exp(dsum)
            snew_ref[gi, hh] = jnp.concatenate([d, d], axis=1) * s0 + u

        q2 = (q * jnp.exp(b - b_mid)).astype(BF)
        k2 = (k * jnp.exp(b_mid - b)).astype(BF)
        att = jnp.where(visible, _dot_nt(q2, k2), 0.0).astype(BF)
        o = o_inter + _dot(att, v)
        ms = jnp.mean(o * o, axis=-1, keepdims=True)
        on = o * lax.rsqrt(ms + EPS) * g_row
        o_ref[:, vcols] = (on * _silu(z_ref[:, vcols].astype(F32))).astype(o_ref.dtype)


def _gla_sample(oc, s_buf, proj_a, proj_b, loga, l, s0, ones, g):
    C = SMP_GB * DEC_SEQ
    base = T_P // C
    alias_in = [oc] + ([s_buf] if l > 0 else [])
    n_alias = len(alias_in)
    kern = lambda *refs: _gla_sample_kernel(n_alias, *refs)
    kw, vw = GLA_HPS * C_DK, GLA_HPS * C_DV
    state_spec = pl.BlockSpec((None, SMP_GB, GLA_HPS, C_DK, C_DV), lambda b, h: (l, b, h, 0, 0))

    def cols(width, offset, stride=1, extra=0):
        assert offset % width == 0
        return pl.BlockSpec((C, width), lambda b, h: (base + b, offset // width + stride * h + extra))

    return pl.pallas_call(
        kern,
        grid=(DEC_BATCH // SMP_GB, C_HEADS // GLA_HPS),
        in_specs=[_any_spec()] * n_alias + [cols(kw, A2_CQ), cols(kw, A2_CK)]
        + [cols(GLA_VW, A2_CV, GLA_VBLKS, j) for j in range(GLA_VBLKS)]
        + [cols(vw, B_CZ), cols(kw, 0), state_spec,
           pl.BlockSpec((C, 128), lambda b, h: (0, 0)), _row_spec(l, C_DV)],
        out_specs=[cols(vw, 0), state_spec],
        out_shape=[
            jax.ShapeDtypeStruct((T_ALL, BRANCH), BF),
            jax.ShapeDtypeStruct((DEPTH, DEC_BATCH, C_HEADS, C_DK, C_DV), F32),
        ],
        input_output_aliases={i: i for i in range(n_alias)},
        compiler_params=_cparams(2),
    )(*alias_in, *([proj_a] * (2 + GLA_VBLKS)), proj_b, loga, s0, ones, g)


def _merged_residual(ba_ref, bb_ref, bc_ref, g0_ref, g1_ref, g2_ref, x, wb_ref, wo_ref):
    acc = _sigmoid(g0_ref[...].astype(F32)) * _dot(ba_ref[...], wb_ref[0])
    acc = acc + _sigmoid(g1_ref[...].astype(F32)) * _dot(bb_ref[...], wb_ref[1])
    acc = acc + _sigmoid(g2_ref[...].astype(F32)) * _dot(bc_ref[...], wb_ref[2])
    return x + _dot(acc.astype(BF), wo_ref[...])


def _merge_last_kernel(ba_ref, bb_ref, bc_ref, g0_ref, g1_ref, g2_ref, x_ref, wb_ref, wo_ref, yp_ref, ys_ref):
    y = _merged_residual(ba_ref, bb_ref, bc_ref, g0_ref, g1_ref, g2_ref, x_ref[...], wb_ref, wo_ref)
    i = pl.program_id(0)

    @pl.when(i < N_P_TILES)
    def _():
        yp_ref[...] = y

    @pl.when(i == N_P_TILES)
    def _():
        ys_ref[...] = y


def _merge_next_kernel(first, *refs):
    ba_ref, bb_ref, bc_ref, g0_ref, g1_ref, g2_ref = refs[:6]
    refs = refs[6:]
    if first:
        x = _pick_rows(refs[0], refs[1])
        refs = refs[2:]
    else:
        x = refs[0][...]
        refs = refs[1:]
    (wb_ref, wo_ref, g_ref, wlr_ref, wg_ref, bg_ref, wb_next_ref, wo_next_ref,
     y_ref, h_ref, loga_ref, wb_bf_ref, wo_bf_ref) = refs
    wb_bf_ref[...] = wb_next_ref[...].astype(BF)
    wo_bf_ref[...] = wo_next_ref[...].astype(BF)
    y = _merged_residual(ba_ref, bb_ref, bc_ref, g0_ref, g1_ref, g2_ref, x, wb_ref, wo_ref)
    y_ref[...] = y
    h, loga = _norm_gate(y, g_ref, wlr_ref, wg_ref, bg_ref)
    h_ref[...] = h
    loga_ref[...] = loga


def _merge_out(ba, bb, bc, proj_b, x, l, wb_bf, wo_bf, w_branch, w_out, norm_g, w_lr, w_g, b_gate):
    first, last = l == 0, l + 1 == DEPTH
    resident = pl.Buffered(1)
    w_specs = [
        pl.BlockSpec((N_BRANCH, BRANCH, D_MODEL), lambda i: (0, 0, 0), pipeline_mode=resident),
        pl.BlockSpec((D_MODEL, D_MODEL), lambda i: (0, 0), pipeline_mode=resident),
    ]
    row = lambda w: pl.BlockSpec((TM_OUT, w), lambda i: (i, 0))
    gate = lambda n: pl.BlockSpec((TM_OUT, D_MODEL), lambda i: (i, B_G // D_MODEL + n))
    mix_specs = [row(BRANCH)] * N_BRANCH + [gate(n) for n in range(N_BRANCH)]
    if last:
        return pl.pallas_call(
            _merge_last_kernel,
            grid=(N_TILES,),
            in_specs=mix_specs + [row(D_MODEL)] + w_specs,
            out_specs=_split_rows_specs(D_MODEL),
            out_shape=[jax.ShapeDtypeStruct((T_P, D_MODEL), F32), jax.ShapeDtypeStruct((T_S, D_MODEL), F32)],
            compiler_params=_cparams(1),
        )(ba, bb, bc, proj_b, proj_b, proj_b, x, wb_bf, wo_bf)

    wb_rows = N_BRANCH * BRANCH // N_P_TILES
    wo_rows = D_MODEL // N_P_TILES
    piece = lambda i: jnp.minimum(i, N_P_TILES - 1)
    x_specs = _split_rows_specs(D_MODEL) if first else [row(D_MODEL)]
    kern = lambda *refs: _merge_next_kernel(first, *refs)
    outs = pl.pallas_call(
        kern,
        grid=(N_TILES,),
        in_specs=mix_specs + x_specs + w_specs + _norm_specs(l + 1) + [
            pl.BlockSpec((None, wb_rows, D_MODEL), lambda i: (l + 1, piece(i), 0)),
            pl.BlockSpec((None, wo_rows, D_MODEL), lambda i: (l + 1, piece(i), 0)),
        ],
        out_specs=[row(D_MODEL), row(D_MODEL), row(C_KEY),
                   pl.BlockSpec((wb_rows, D_MODEL), lambda i: (piece(i), 0)),
                   pl.BlockSpec((wo_rows, D_MODEL), lambda i: (piece(i), 0))],
        out_shape=[jax.ShapeDtypeStruct((T_ALL, D_MODEL), F32), jax.ShapeDtypeStruct((T_ALL, D_MODEL), BF),
                   jax.ShapeDtypeStruct((T_ALL, C_KEY), F32),
                   jax.ShapeDtypeStruct((N_BRANCH * BRANCH, D_MODEL), BF),
                   jax.ShapeDtypeStruct((D_MODEL, D_MODEL), BF)],
        compiler_params=_cparams(1),
    )(ba, bb, bc, proj_b, proj_b, proj_b, *(list(x) if first else [x]), wb_bf, wo_bf,
      norm_g, w_lr, w_g, b_gate, w_branch.reshape(DEPTH, N_BRANCH * BRANCH, D_MODEL), w_out)
    y, h, loga, wb_next, wo_next = outs
    return y, h, loga, wb_next.reshape(N_BRANCH, BRANCH, D_MODEL), wo_next


def _rope_tables(pos):
    half = A_HEAD_DIM // 2
    inv = jnp.power(ROPE_THETA, -jnp.arange(half, dtype=F32) * (2.0 / A_HEAD_DIM))
    ang = pos.astype(F32)[:, None] * inv[None, :]
    cos, sin = jnp.cos(ang), jnp.sin(ang)
    cos128 = jnp.concatenate([cos, cos, cos, cos], axis=1)
    sin128 = jnp.concatenate([-sin, sin, -sin, sin], axis=1)
    return cos128, sin128


def kernel(x_prompt, x_sample, cache_a_k, cache_a_v, state_pool, state_gla, norm_g, w_in, q_norm_g, k_norm_g,
           sinks, w_pool, pool_scale, w_gate_lr, b_gate, gla_norm_g, w_branch, w_out):
    w_lr = jnp.pad(w_in[:, :, A_COLS:A_COLS + C_GATE_RANK],
                   ((0, 0), (0, 0), (0, LANES - C_GATE_RANK))).astype(BF)
    w_g = jnp.pad(w_gate_lr, ((0, 0), (0, LANES - C_GATE_RANK), (0, 0))).astype(BF)
    wb_bf = w_branch[0].astype(BF)
    wo_bf = w_out[0].astype(BF)
    w_pool_bf = w_pool.astype(BF)
    qg = (jnp.tile(q_norm_g, (1, 256 // A_HEAD_DIM)) * (A_HEAD_DIM ** -0.5))[:, None, :]
    kg = jnp.tile(k_norm_g, (1, 256 // A_HEAD_DIM))[:, None, :]
    norm_g = norm_g[:, None, :]
    b_gate = b_gate[:, None, :]
    pool_scale = pool_scale[:, None, :]
    gla_norm_g = gla_norm_g[:, None, :]
    cos_p, sin_p = _rope_tables(jnp.arange(SEQ))
    cos_s, sin_s = _rope_tables(PAST_LEN + jnp.arange(DEC_SEQ))
    cos_s = jnp.tile(cos_s, (SMP_BS, 1))
    sin_s = jnp.tile(sin_s, (SMP_BS, 1))
    lane = np.arange(256)
    bd = jnp.asarray((lane[:, None] // A_HEAD_DIM == lane[None, :] // A_HEAD_DIM).astype(np.float32), dtype=BF)
    ones_c = jnp.ones((GLA_C, 128), BF)
    to_hd_pos = lambda c: jnp.transpose(c, (0, 1, 3, 4, 2)).reshape(DEPTH, DEC_BATCH, KV_W, WINDOW)
    from_hd_pos = lambda c: jnp.transpose(
        c.reshape(DEPTH, DEC_BATCH, A_KV_HEADS, A_HEAD_DIM, WINDOW), (0, 1, 4, 2, 3))
    cache_k = to_hd_pos(cache_a_k)
    cache_v = to_hd_pos(cache_a_v)
    pool_prev = jnp.pad(state_pool, ((0, 0), (0, 0), (1, 0), (0, 0)))

    w_t = jnp.swapaxes(w_in, 1, 2)
    x = (x_prompt.reshape(T_P, D_MODEL), x_sample.reshape(T_S, D_MODEL))
    h, loga = _norm(x[0], x[1], 0, norm_g, w_lr, w_g, b_gate)
    pk, pv, pp, sp = [], [], [], []
    nk_buf = nv_buf = sp_buf = ss_buf = None
    for l in range(DEPTH):
        proj_a = _inproj_a(h, w_t, l)
        proj_a2 = _inproj_a2(h, w_t, l)
        proj_b = _inproj_b(h, w_t, l)

        oa, knew_p, vnew_p = _attn_prompt(proj_a, l, sinks, cos_p, sin_p, bd, qg, kg)
        oa, nk_buf, nv_buf = _attn_sample(oa, nk_buf, nv_buf, proj_a, l, cache_k, cache_v, sinks,
                                          cos_s, sin_s, bd, qg, kg)
        ob, tail_p = _pool_prompt(proj_a, proj_a2, l, w_pool_bf, pool_scale)
        ob, tail_s = _pool_sample(ob, proj_a, proj_a2, l, pool_prev, w_pool_bf, pool_scale)
        oc, sp_buf = _gla_prompt(sp_buf, proj_a2, proj_b, loga, l, gla_norm_g)
        oc, ss_buf = _gla_sample(oc, ss_buf, proj_a2, proj_b, loga, l, state_gla, ones_c, gla_norm_g)

        outs = _merge_out(oa, ob, oc, proj_b, x, l, wb_bf, wo_bf, w_branch, w_out, norm_g, w_lr, w_g, b_gate)
        if l + 1 < DEPTH:
            x, h, loga, wb_bf, wo_bf = outs
        else:
            yp, ys = outs

        pk.append(knew_p)
        pv.append(vnew_p)
        pp.append(tail_p[:, 1:])
        sp.append(tail_s[:, 1:])

    kv5 = lambda a, nb: a.reshape(DEPTH, nb, WINDOW, A_KV_HEADS, A_HEAD_DIM)
    yp = yp.reshape(BATCH, SEQ, D_MODEL)
    ys = ys.reshape(DEC_BATCH, DEC_SEQ, D_MODEL)
    return (yp, ys, kv5(jnp.stack(pk), BATCH), kv5(jnp.stack(pv), BATCH), jnp.stack(pp), sp_buf,
            from_hd_pos(nk_buf), from_hd_pos(nv_buf), jnp.stack(sp), ss_buf)
```
